```python
import jax, jax.numpy as jnp
from jax import lax
import numpy as np

D_MODEL = 1024
BATCH = 4
SEQ = 4096
DEPTH = 2
DEC_BATCH = 32
DEC_SEQ = 8
PAST_LEN = 16384
PAGE_SIZE = 128

D_RNN = D_MODEL
N_RNN_BLOCKS = 16
RNN_BLOCK = D_RNN // N_RNN_BLOCKS
CONV_W = 4
LRU_C = 8.0
N_HEADS = 16
HEAD_DIM = 64
KV_HEADS = 4
GROUP = N_HEADS // KV_HEADS
CMP_LEN = 32
CMP_STRIDE = 16
SEL_BLOCK = 64
N_SELECT = 16
WINDOW = 512
Q_BLOCK = 32
FORCE_BONUS = 1e4
N_GROUPS = 4
EXP_PER_GROUP = 8
N_EXPERTS = N_GROUPS * EXP_PER_GROUP
D_EXPERT = 256
TOP_IN_GROUP = 2
ALPHA = (2.0 * DEPTH) ** 0.25
BETA = (8.0 * DEPTH) ** -0.25
LN_EPS = 1e-5
KV_W = 2 * KV_HEADS * HEAD_DIM
IN_SIZES = [D_RNN, D_RNN, N_HEADS * HEAD_DIM, KV_W, KV_W, KV_W, 3 * N_HEADS, 2 * D_MODEL]
N_IN = sum(IN_SIZES)

kernel_name = 'hawk_nsa_hier_moe_deepnorm_step'


def layer_norm(x, g, b):
    xf = x.astype(jnp.float32)
    mu = jnp.mean(xf, axis=-1, keepdims=True)
    var = jnp.mean(jnp.square(xf - mu), axis=-1, keepdims=True)
    return ((xf - mu) * lax.rsqrt(var + LN_EPS) * g.astype(jnp.float32) + b.astype(jnp.float32)).astype(x.dtype)


def alibi_slopes():
    return 2.0 ** (-8.0 * jnp.arange(1, N_HEADS + 1, dtype=jnp.float32) / N_HEADS)


def masked_softmax(s, mask, axis):
    s = jnp.where(mask, s, -jnp.inf)
    m = jnp.max(s, axis=axis, keepdims=True)
    m = jnp.where(jnp.isfinite(m), m, 0.0)
    p = jnp.exp(s - m)
    return p / jnp.maximum(jnp.sum(p, axis=axis, keepdims=True), 1e-30)


def project_in(x, w_in):
    B, T = x.shape[:2]
    u = jnp.einsum('btd,dn->btn', x, w_in)
    splits = [int(v) for v in np.cumsum(IN_SIZES)[:-1]]
    xr, gr, q, kvc, kvs, kvw, ng, mg = jnp.split(u, splits, axis=-1)
    q = q.reshape(B, T, KV_HEADS, GROUP, HEAD_DIM)
    kvc = kvc.reshape(B, T, 2, KV_HEADS, HEAD_DIM)
    kvs = kvs.reshape(B, T, 2, KV_HEADS, HEAD_DIM)
    kvw = kvw.reshape(B, T, 2, KV_HEADS, HEAD_DIM)
    ng = jax.nn.sigmoid(ng.reshape(B, T, 3, KV_HEADS, GROUP))[..., None]
    mg = jax.nn.sigmoid(mg.reshape(B, T, 2, D_MODEL))
    return xr, gr, q, kvc, kvs, kvw, ng, mg


def rglru_branch(xr, gr, conv_buf, h0, pos, conv_w, conv_b, wa, ba, wx, bx, lam):
    B, T = xr.shape[:2]
    xc = jnp.concatenate([conv_buf.astype(xr.dtype), xr], axis=1)
    conv = conv_b + sum(conv_w[k] * xc[:, k:k + T] for k in range(CONV_W))
    new_buf = xc[:, -(CONV_W - 1):]
    xb = conv.reshape(B, T, N_RNN_BLOCKS, RNN_BLOCK)
    r = jax.nn.sigmoid((jnp.einsum('btni,nij->btnj', xb, wa).reshape(B, T, D_RNN) + ba).astype(jnp.float32))
    i = jax.nn.sigmoid((jnp.einsum('btni,nij->btnj', xb, wx).reshape(B, T, D_RNN) + bx).astype(jnp.float32))
    log_a = -LRU_C * r * jax.nn.softplus(-lam.astype(jnp.float32))
    a = jnp.exp(log_a)
    mult = jnp.where(pos[None, :, None] == 0, 1.0, jnp.sqrt(-jnp.expm1(2.0 * log_a)))
    b_in = mult * i * conv.astype(jnp.float32)

    def comb(left, right):
        a1, b1 = left
        a2, b2 = right
        return a1 * a2, a2 * b1 + b2

    a_cum, h = lax.associative_scan(comb, (a, b_in), axis=1)
    h = h + a_cum * h0.astype(jnp.float32)[:, None]
    y = (jax.nn.gelu(gr.astype(jnp.float32)) * h).astype(xr.dtype)
    return y, new_buf, h[:, -1].astype(xr.dtype)


def compress_kv(kv, cmp_w):
    B, Tk = kv.shape[:2]
    nc = (Tk - CMP_LEN) // CMP_STRIDE + 1
    segs = kv[:, :(nc + 1) * CMP_STRIDE].reshape(B, nc + 1, CMP_STRIDE, 2, KV_HEADS, HEAD_DIM)
    c = (jnp.einsum('bnsxgd,sxd->bnxgd', segs[:, :-1], cmp_w[:CMP_STRIDE])
         + jnp.einsum('bnsxgd,sxd->bnxgd', segs[:, 1:], cmp_w[CMP_STRIDE:]))
    return c[:, :, 0], c[:, :, 1]


def nsa_core(q, q_pos, gates, ck, cv, fetch, n_blk, kw, kw_pos):
    B, Q = q.shape[:2]
    f32 = jnp.float32
    slopes = alibi_slopes().reshape(KV_HEADS, GROUP)
    sl5 = slopes[None, None, :, :, None]
    scale = HEAD_DIM ** -0.5
    nc = ck.shape[1]
    c_end = jnp.arange(nc, dtype=jnp.int32) * CMP_STRIDE + (CMP_LEN - 1)
    d_c = (q_pos[:, None] - c_end[None, :])[None, :, None, None, :]
    s_c = jnp.einsum('bqgrd,bcgd->bqgrc', q, ck).astype(f32) * scale - sl5 * d_c.astype(f32)
    p_c = masked_softmax(s_c, d_c >= 0, -1)
    o_c = jnp.einsum('bqgrc,bcgd->bqgrd', p_c.astype(cv.dtype), cv)
    per = SEL_BLOCK // CMP_STRIDE
    imp = jnp.sum(p_c, axis=3)
    imp = jnp.pad(imp, ((0, 0), (0, 0), (0, 0), (0, n_blk * per - nc))).reshape(B, Q, KV_HEADS, n_blk, per).sum(-1)
    blk = jnp.arange(n_blk, dtype=jnp.int32)[None, :]
    cur = (q_pos // SEL_BLOCK)[:, None]
    forced = (blk == 0) | (blk == cur) | (blk == cur - 1)
    score = jnp.where(forced[None, :, None, :], FORCE_BONUS, imp)
    score = jnp.where((blk <= cur)[None, :, None, :], score, -jnp.inf)
    _, idx = lax.top_k(score, min(N_SELECT, n_blk))
    kv_s = fetch(idx)
    k_pos = idx[..., None] * SEL_BLOCK + jnp.arange(SEL_BLOCK, dtype=jnp.int32)
    d_s = q_pos[None, :, None, None, None] - k_pos
    s_s = (jnp.einsum('bqgrd,bqgnld->bqgrnl', q, kv_s[..., 0, :]).astype(f32) * scale
           - slopes[None, None, :, :, None, None] * d_s[:, :, :, None].astype(f32))
    p_s = masked_softmax(s_s, (d_s >= 0)[:, :, :, None], (-2, -1))
    o_s = jnp.einsum('bqgrnl,bqgnld->bqgrd', p_s.astype(kv_s.dtype), kv_s[..., 1, :])
    d_w = q_pos[:, None] - kw_pos[None, :]
    m_w = (d_w >= 0) & (d_w < WINDOW) & (kw_pos[None, :] >= 0)
    s_w = jnp.einsum('bqgrd,bkgd->bqgrk', q, kw[:, :, 0]).astype(f32) * scale - sl5 * d_w[None, :, None, None, :].astype(f32)
    p_w = masked_softmax(s_w, m_w[None, :, None, None, :], -1)
    o_w = jnp.einsum('bqgrk,bkgd->bqgrd', p_w.astype(kw.dtype), kw[:, :, 1])
    return gates[:, :, 0] * o_c + gates[:, :, 1] * o_s + gates[:, :, 2] * o_w


def blocked_queries(fn, q, gates, qb):
    B, T = q.shape[:2]
    nb = T // qb
    q_b = jnp.moveaxis(q.reshape(B, nb, qb, *q.shape[2:]), 1, 0)
    g_b = jnp.moveaxis(gates.reshape(B, nb, qb, *gates.shape[2:]), 1, 0)
    starts = jnp.arange(nb, dtype=jnp.int32) * qb
    o = lax.map(lambda a: fn(a[0], a[1], a[2]), (q_b, g_b, starts))
    return jnp.moveaxis(o, 0, 1).reshape(B, T, N_HEADS * HEAD_DIM)


def nsa_prompt(q, gates, kv_cmp, kv_sel, kv_win, cmp_w):
    B, T = q.shape[:2]
    ck, cv = compress_kv(kv_cmp, cmp_w)
    n_blk = T // SEL_BLOCK
    blocks = kv_sel.reshape(B, n_blk, SEL_BLOCK, 2, KV_HEADS, HEAD_DIM)
    bi = jnp.arange(B)[:, None, None, None]
    gi = jnp.arange(KV_HEADS)[None, None, :, None]

    def fetch(idx):
        return blocks[bi, idx, :, :, gi, :]

    kw_pad = jnp.pad(kv_win, ((0, 0), (WINDOW, 0), (0, 0), (0, 0), (0, 0)))

    def run(qb, gb, s0):
        q_pos = s0 + jnp.arange(Q_BLOCK, dtype=jnp.int32)
        kw = lax.dynamic_slice_in_dim(kw_pad, s0, WINDOW + Q_BLOCK, axis=1)
        kw_pos = s0 - WINDOW + jnp.arange(WINDOW + Q_BLOCK, dtype=jnp.int32)
        return nsa_core(qb, q_pos, gb, ck, cv, fetch, n_blk, kw, kw_pos)

    return blocked_queries(run, q, gates, Q_BLOCK)


def nsa_sample(q, gates, kv_cmp, kv_sel, kv_win, cache_cmp, cache_sel, win_buf, page_table, layer, cmp_w):
    B, T = q.shape[:2]
    past = page_table.shape[1] * PAGE_SIZE
    past_cmp = cache_cmp[layer, page_table].reshape(B, past, 2, KV_HEADS, HEAD_DIM)
    ck, cv = compress_kv(jnp.concatenate([past_cmp.astype(kv_cmp.dtype), kv_cmp], axis=1), cmp_w)
    n_blk = -(-(past + T) // SEL_BLOCK)
    bpp = PAGE_SIZE // SEL_BLOCK
    n_past_blk = past // SEL_BLOCK
    pool_blocks = cache_sel.reshape(cache_sel.shape[0], cache_sel.shape[1], bpp, SEL_BLOCK, 2, KV_HEADS, HEAD_DIM)
    n_tail = -(-T // SEL_BLOCK)
    tail = jnp.pad(kv_sel, ((0, 0), (0, n_tail * SEL_BLOCK - T), (0, 0), (0, 0), (0, 0)))
    tail = tail.reshape(B, n_tail, SEL_BLOCK, 2, KV_HEADS, HEAD_DIM)
    bi = jnp.arange(B)[:, None, None, None]
    gi = jnp.arange(KV_HEADS)[None, None, :, None]

    def fetch(idx):
        pidx = jnp.minimum(idx, n_past_blk - 1)
        phys = page_table[bi, pidx // bpp]
        kv_past = pool_blocks[layer, phys, pidx % bpp, :, :, gi, :].astype(tail.dtype)
        kv_tail = tail[bi, jnp.clip(idx - n_past_blk, 0, n_tail - 1), :, :, gi, :]
        return jnp.where((idx >= n_past_blk)[..., None, None, None], kv_tail, kv_past)

    kw = jnp.concatenate([win_buf.astype(kv_win.dtype), kv_win], axis=1)
    wb = win_buf.shape[1]
    kw_pos = past - wb + jnp.arange(wb + T, dtype=jnp.int32)

    def run(qb, gb, s0):
        q_pos = past + s0 + jnp.arange(1, dtype=jnp.int32)
        return nsa_core(qb, q_pos, gb, ck, cv, fetch, n_blk, kw, kw_pos)

    return blocked_queries(run, q, gates, 1), kw[:, T:]


def merge_out(y_rnn, y_att, mg, w_br_a, w_br_b, w_out):
    merged = mg[:, :, 0] * jnp.einsum('btr,rd->btd', y_rnn, w_br_a) + mg[:, :, 1] * jnp.einsum('bth,hd->btd', y_att, w_br_b)
    return jnp.einsum('btd,de->bte', merged, w_out)


def hier_moe(x, wg, bg, we, be, w1, w3, w2):
    B, T, D = x.shape
    xf = x.reshape(B * T, D)
    glog = (xf @ wg + bg).astype(jnp.float32)
    gp = jax.nn.softmax(glog, axis=-1)
    grp = jnp.argmax(glog, axis=-1)
    g_oh = jax.nn.one_hot(grp, N_GROUPS, dtype=jnp.float32)
    gw = jnp.sum(gp * g_oh, axis=-1, keepdims=True)
    elog = (xf @ we + be).astype(jnp.float32).reshape(-1, N_GROUPS, EXP_PER_GROUP)
    elog_g = jnp.sum(elog * g_oh[:, :, None], axis=1)
    ev, ei = lax.top_k(elog_g, TOP_IN_GROUP)
    ew = jax.nn.softmax(ev, axis=-1) * gw
    eid = grp[:, None] * EXP_PER_GROUP + ei
    combine = jnp.sum(jax.nn.one_hot(eid, N_EXPERTS, dtype=jnp.float32) * ew[..., None], axis=1)
    h = jax.nn.silu(jnp.einsum('nd,edf->nef', xf, w1)) * jnp.einsum('nd,edf->nef', xf, w3)
    h = h * combine[:, :, None].astype(h.dtype)
    return jnp.einsum('nef,efd->nd', h, w2).reshape(B, T, D)


def setup_inputs(seed: int = 0) -> dict:
    key = jax.random.key(seed)
    ks = jax.random.split(key, 40)
    n_pages = PAST_LEN // PAGE_SIZE
    n_used = DEC_BATCH * n_pages
    n_pool = n_used + n_used // 4
    wb = min(WINDOW, PAST_LEN)

    def nrm(k, shape, s):
        return jax.random.normal(k, shape, jnp.float32) * s

    page_table = jax.random.permutation(ks[0], n_pool)[:n_used].reshape(DEC_BATCH, n_pages).astype(jnp.int32)
    a0 = jax.random.uniform(ks[1], (DEPTH, D_RNN), jnp.float32, 0.9, 0.999)
    s_a = a0 ** (1.0 / LRU_C)
    return {
        'x_prompt': nrm(ks[2], (BATCH, SEQ, D_MODEL), 1.0),
        'x_sample': nrm(ks[3], (DEC_BATCH, DEC_SEQ, D_MODEL), 1.0),
        'cache_cmp': nrm(ks[4], (DEPTH, n_pool, PAGE_SIZE, 2, KV_HEADS, HEAD_DIM), 1.0),
        'cache_sel': nrm(ks[5], (DEPTH, n_pool, PAGE_SIZE, 2, KV_HEADS, HEAD_DIM), 1.0),
        'cache_win': nrm(ks[6], (DEPTH, DEC_BATCH, wb, 2, KV_HEADS, HEAD_DIM), 1.0),
        'state_conv': nrm(ks[7], (DEPTH, DEC_BATCH, CONV_W - 1, D_RNN), 1.0),
        'state_lru': nrm(ks[8], (DEPTH, DEC_BATCH, D_RNN), 0.5),
        'page_table': page_table,
        'w_in': nrm(ks[9], (DEPTH, D_MODEL, N_IN), D_MODEL ** -0.5),
        'conv_w': nrm(ks[10], (DEPTH, CONV_W, D_RNN), CONV_W ** -0.5),
        'conv_b': nrm(ks[11], (DEPTH, D_RNN), 0.01),
        'lru_wa': nrm(ks[12], (DEPTH, N_RNN_BLOCKS, RNN_BLOCK, RNN_BLOCK), RNN_BLOCK ** -0.5),
        'lru_ba': nrm(ks[13], (DEPTH, D_RNN), 0.01),
        'lru_wx': nrm(ks[14], (DEPTH, N_RNN_BLOCKS, RNN_BLOCK, RNN_BLOCK), RNN_BLOCK ** -0.5),
        'lru_bx': nrm(ks[15], (DEPTH, D_RNN), 0.01),
        'lru_lambda': jnp.log(s_a) - jnp.log1p(-s_a),
        'cmp_w': (1.0 + nrm(ks[16], (DEPTH, CMP_LEN, 2, HEAD_DIM), 0.1)) / CMP_LEN,
        'w_br_a': nrm(ks[17], (DEPTH, D_RNN, D_MODEL), D_RNN ** -0.5),
        'w_br_b': nrm(ks[18], (DEPTH, N_HEADS * HEAD_DIM, D_MODEL), (N_HEADS * HEAD_DIM) ** -0.5),
        'w_out': nrm(ks[19], (DEPTH, D_MODEL, D_MODEL), BETA * D_MODEL ** -0.5),
        'ln1_g': 1.0 + nrm(ks[20], (DEPTH, D_MODEL), 0.02),
        'ln1_b': nrm(ks[21], (DEPTH, D_MODEL), 0.02),
        'ln2_g': 1.0 + nrm(ks[22], (DEPTH, D_MODEL), 0.02),
        'ln2_b': nrm(ks[23], (DEPTH, D_MODEL), 0.02),
        'router_wg': nrm(ks[24], (DEPTH, D_MODEL, N_GROUPS), D_MODEL ** -0.5),
        'router_bg': nrm(ks[25], (DEPTH, N_GROUPS), 0.01),
        'router_we': nrm(ks[26], (DEPTH, D_MODEL, N_EXPERTS), D_MODEL ** -0.5),
        'router_be': nrm(ks[27], (DEPTH, N_EXPERTS), 0.01),
        'exp_w1': nrm(ks[28], (DEPTH, N_EXPERTS, D_MODEL, D_EXPERT), D_MODEL ** -0.5),
        'exp_w3': nrm(ks[29], (DEPTH, N_EXPERTS, D_MODEL, D_EXPERT), D_MODEL ** -0.5),
        'exp_w2': nrm(ks[30], (DEPTH, N_EXPERTS, D_EXPERT, D_MODEL), BETA * D_EXPERT ** -0.5),
    }


def reference(x_prompt, x_sample, cache_cmp, cache_sel, cache_win, state_conv, state_lru, page_table,
              w_in, conv_w, conv_b, lru_wa, lru_ba, lru_wx, lru_bx, lru_lambda, cmp_w,
              w_br_a, w_br_b, w_out, ln1_g, ln1_b, ln2_g, ln2_b,
              router_wg, router_bg, router_we, router_be, exp_w1, exp_w3, exp_w2):
    B, T = x_prompt.shape[:2]
    TS = x_sample.shape[1]
    past = page_table.shape[1] * PAGE_SIZE
    pos_p = jnp.arange(T, dtype=jnp.int32)
    pos_s = past + jnp.arange(TS, dtype=jnp.int32)
    xp, xs = x_prompt, x_sample
    cmp_p, cmp_s, sel_p, sel_s, win_p, win_s = [], [], [], [], [], []
    conv_p, conv_s, lru_p, lru_s = [], [], [], []
    for l in range(DEPTH):
        lru_par = (conv_w[l], conv_b[l], lru_wa[l], lru_ba[l], lru_wx[l], lru_bx[l], lru_lambda[l])
        moe_par = (router_wg[l], router_bg[l], router_we[l], router_be[l], exp_w1[l], exp_w3[l], exp_w2[l])
        xr, gr, q, kvc, kvs, kvw, ng, mg = project_in(xp, w_in[l])
        y_rnn, cb, hT = rglru_branch(xr, gr, jnp.zeros((B, CONV_W - 1, D_RNN), xp.dtype),
                                     jnp.zeros((B, D_RNN), jnp.float32), pos_p, *lru_par)
        y_att = nsa_prompt(q, ng, kvc, kvs, kvw, cmp_w[l])
        xp = layer_norm(ALPHA * xp + merge_out(y_rnn, y_att, mg, w_br_a[l], w_br_b[l], w_out[l]), ln1_g[l], ln1_b[l])
        xp = layer_norm(ALPHA * xp + hier_moe(xp, *moe_par), ln2_g[l], ln2_b[l])
        cmp_p.append(kvc)
        sel_p.append(kvs)
        win_p.append(kvw[:, -min(WINDOW, T):])
        conv_p.append(cb)
        lru_p.append(hT)
        xr, gr, q, kvc, kvs, kvw, ng, mg = project_in(xs, w_in[l])
        y_rnn, cb, hT = rglru_branch(xr, gr, state_conv[l], state_lru[l], pos_s, *lru_par)
        y_att, nwin = nsa_sample(q, ng, kvc, kvs, kvw, cache_cmp, cache_sel, cache_win[l], page_table, l, cmp_w[l])
        xs = layer_norm(ALPHA * xs + merge_out(y_rnn, y_att, mg, w_br_a[l], w_br_b[l], w_out[l]), ln1_g[l], ln1_b[l])
        xs = layer_norm(ALPHA * xs + hier_moe(xs, *moe_par), ln2_g[l], ln2_b[l])
        cmp_s.append(kvc)
        sel_s.append(kvs)
        win_s.append(nwin)
        conv_s.append(cb)
        lru_s.append(hT)
    return (xp, xs, jnp.stack(cmp_p), jnp.stack(cmp_s), jnp.stack(sel_p), jnp.stack(sel_s),
            jnp.stack(win_p), jnp.stack(win_s), jnp.stack(conv_p), jnp.stack(conv_s),
            jnp.stack(lru_p), jnp.stack(lru_s))
```

```python
import functools

import jax
import jax.numpy as jnp
import numpy as np
from jax import lax
from jax.experimental import pallas as pl
from jax.experimental.pallas import tpu as pltpu

F32 = jnp.float32
BF16 = jnp.bfloat16
NEG_INF = float("-inf")

D_MODEL = 1024
D_RNN = 1024
N_RNN_BLOCKS = 16
RNN_BLOCK = D_RNN // N_RNN_BLOCKS
CONV_W = 4
LRU_C = 8.0
N_HEADS = 16
HEAD_DIM = 64
KV_HEADS = 4
GROUP = N_HEADS // KV_HEADS
CMP_LEN = 32
CMP_STRIDE = 16
SEL_BLOCK = 64
N_SELECT = 16
WINDOW = 512
FORCE_BONUS = 1e4
PAGE_SIZE = 128
N_GROUPS = 4
EXP_PER_GROUP = 8
N_EXPERTS = N_GROUPS * EXP_PER_GROUP
D_EXPERT = 256
DEPTH = 2
ALPHA = (2.0 * DEPTH) ** 0.25
LN_EPS = 1e-5
KV_W = 2 * KV_HEADS * HEAD_DIM
QK_SCALE = HEAD_DIM ** -0.5
PER_SEL = SEL_BLOCK // CMP_STRIDE
SLOPES = [2.0 ** (-8.0 * (h + 1) / N_HEADS) for h in range(N_HEADS)]

LANES = 128
VMEM_LIMIT_BYTES = 56 * 1024 * 1024

PROJ_TM = 256
LRU_TC = 256
MERGE_TM = 512
MOE_TM = 1024
ATT_QB = 128
ATT_KC = 512
PAGES_PER_STEP = 8


def _cparams(sem):
    return pltpu.CompilerParams(dimension_semantics=sem, vmem_limit_bytes=VMEM_LIMIT_BYTES)


def _full_spec(shape):
    nd = len(shape)
    return pl.BlockSpec(shape, lambda *_: (0,) * nd)


def _dot(a, b):
    return jnp.dot(a, b, preferred_element_type=F32)


def _dot_nt(a, b):
    return lax.dot_general(a, b, (((1,), (1,)), ((), ())), preferred_element_type=F32)


def _sigmoid(x):
    return 1.0 / (1.0 + jnp.exp(-x))


def _proj_kernel(x_ref, wm_ref, wng_ref, wmg_ref,
                 xr_ref, gr_ref, q_ref, kvc_ref, kvs_ref, kvw_ref, kvsb_ref, kvwb_ref, ng_ref, mg_ref):
    xb = x_ref[...].astype(BF16)
    xr_ref[...] = _dot(xb, wm_ref[:, 0:D_RNN])
    gr_ref[...] = _dot(xb, wm_ref[:, D_RNN:2 * D_RNN])
    q_ref[...] = _dot(xb, wm_ref[:, 2 * D_RNN:3 * D_RNN]).astype(BF16)
    o = 3 * D_RNN
    kvc_ref[...] = _dot(xb, wm_ref[:, o:o + KV_W])
    kvs = _dot(xb, wm_ref[:, o + KV_W:o + 2 * KV_W])
    kvs_ref[...] = kvs
    kvsb_ref[...] = kvs.astype(BF16)
    kvw = _dot(xb, wm_ref[:, o + 2 * KV_W:o + 3 * KV_W])
    kvw_ref[...] = kvw
    kvwb_ref[...] = kvw.astype(BF16)
    ng_ref[...] = _sigmoid(_dot(xb, wng_ref[...]))
    mg_ref[...] = _sigmoid(_dot(xb, wmg_ref[...]))


def _project_in(x2d, wm, wng, wmg):
    n = x2d.shape[0]
    tm = min(PROJ_TM, n)
    row = lambda w: pl.BlockSpec((tm, w), lambda i: (i, 0))
    outs = [(D_RNN, F32), (D_RNN, F32), (N_HEADS * HEAD_DIM, BF16), (KV_W, F32), (KV_W, F32), (KV_W, F32),
            (KV_W, BF16), (KV_W, BF16), (LANES, F32), (2 * D_MODEL, F32)]
    return pl.pallas_call(
        _proj_kernel,
        grid=(n // tm,),
        in_specs=[row(D_MODEL), _full_spec(wm.shape), _full_spec(wng.shape), _full_spec(wmg.shape)],
        out_specs=[row(w) for w, _ in outs],
        out_shape=[jax.ShapeDtypeStruct((n, w), dt) for w, dt in outs],
        compiler_params=_cparams(("arbitrary",)),
        name="project_in",
    )(x2d, wm, wng, wmg)


def _shift_rows(x, s, fill):
    row = lax.broadcasted_iota(jnp.int32, x.shape, 0)
    return jnp.where(row >= s, pltpu.roll(x, s, axis=0), fill)


def _gelu_tanh(x):
    c = np.float32(np.sqrt(2.0 / np.pi))
    return 0.5 * x * (1.0 + jnp.tanh(c * (x + np.float32(0.044715) * (x * x * x))))


def _rglru_kernel(xr_ref, gr_ref, cbuf_ref, h0_ref, cw_ref, cb_ref, wa_ref, ba_ref, wx_ref, bx_ref, lam_ref,
                  y_ref, nbuf_ref, ht_ref, xbuf, hcar, *, tc, pos0):
    c = pl.program_id(1)
    nc = pl.num_programs(1)

    @pl.when(c == 0)
    def _():
        xbuf[0:8, :] = jnp.zeros((8, D_RNN), F32)
        xbuf[8 - (CONV_W - 1):8, :] = cbuf_ref[...]
        hcar[...] = h0_ref[...]

    x = xr_ref[...]
    xbuf[8:8 + tc, :] = x
    conv = cb_ref[...] + cw_ref[CONV_W - 1:CONV_W, :] * x
    for j in range(1, CONV_W):
        conv = conv + cw_ref[CONV_W - 1 - j:CONV_W - j, :] * xbuf[8 - j:8 - j + tc, :]
    nbuf = xbuf[8 + tc - (CONV_W - 1):8 + tc, :]
    xbuf[0:8, :] = xbuf[tc:tc + 8, :]

    cvb = conv.astype(BF16)
    nb4 = D_RNN // 256
    za = jnp.concatenate([_dot(cvb[:, k * 256:(k + 1) * 256], wa_ref[k]) for k in range(nb4)], axis=1)
    zx = jnp.concatenate([_dot(cvb[:, k * 256:(k + 1) * 256], wx_ref[k]) for k in range(nb4)], axis=1)
    r = _sigmoid(za + ba_ref[...])
    i = _sigmoid(zx + bx_ref[...])
    nl = -lam_ref[...]
    softplus = jnp.maximum(nl, 0.0) + jnp.log(1.0 + jnp.exp(-jnp.abs(nl)))
    log_a = (-LRU_C) * r * softplus
    a = jnp.exp(log_a)
    pos = pos0 + c * tc + lax.broadcasted_iota(jnp.int32, (tc, 1), 0)
    mult = jnp.where(pos == 0, 1.0, jnp.sqrt(1.0 - jnp.exp(2.0 * log_a)))
    b = mult * i * conv

    s = 1
    while s < tc:
        a_sh = _shift_rows(a, s, 1.0)
        b_sh = _shift_rows(b, s, 0.0)
        b = a * b_sh + b
        a = a * a_sh
        s *= 2
    h = b + a * hcar[...]
    hcar[...] = h[tc - 1:tc, :]
    y_ref[...] = (_gelu_tanh(gr_ref[...]) * h).astype(BF16)

    @pl.when(c == nc - 1)
    def _():
        nbuf_ref[...] = nbuf
        ht_ref[...] = h[tc - 1:tc, :]


def _rglru(xr, gr, conv_buf, h0, cw, cb, wa_bd, ba, wx_bd, bx, lam, pos0):
    bsz, t, _ = xr.shape
    tc = min(LRU_TC, t)
    seq = pl.BlockSpec((None, tc, D_RNN), lambda b, c: (b, c, 0))
    per_b = lambda r: pl.BlockSpec((None, r, D_RNN), lambda b, c: (b, 0, 0))
    vec = pl.BlockSpec((1, D_RNN), lambda b, c: (0, 0))
    bd = pl.BlockSpec(wa_bd.shape, lambda b, c: (0, 0, 0))
    return pl.pallas_call(
        functools.partial(_rglru_kernel, tc=tc, pos0=pos0),
        grid=(bsz, t // tc),
        in_specs=[seq, seq, per_b(CONV_W - 1), per_b(1),
                  pl.BlockSpec((CONV_W, D_RNN), lambda b, c: (0, 0)), vec, bd, vec, bd, vec, vec],
        out_specs=[seq, per_b(CONV_W - 1), per_b(1)],
        out_shape=[jax.ShapeDtypeStruct((bsz, t, D_RNN), BF16),
                   jax.ShapeDtypeStruct((bsz, CONV_W - 1, D_RNN), F32),
                   jax.ShapeDtypeStruct((bsz, 1, D_RNN), F32)],
        scratch_shapes=[pltpu.VMEM((tc + 8, D_RNN), F32), pltpu.VMEM((1, D_RNN), F32)],
        compiler_params=_cparams(("arbitrary", "arbitrary")),
        name="rglru",
    )(xr, gr, conv_buf, h0, cw, cb, wa_bd, ba, wx_bd, bx, lam)


def _segment_sums(x, w1, w2):
    r = x.shape[0]
    x3 = x.reshape(r // CMP_STRIDE, CMP_STRIDE, KV_W)
    return jnp.sum(x3 * w1[None], axis=1), jnp.sum(x3 * w2[None], axis=1)


def _compress_kernel(kv_ref, w_ref, a_ref, b_ref):
    a, b = _segment_sums(kv_ref[...], w_ref[0:CMP_STRIDE, :], w_ref[CMP_STRIDE:CMP_LEN, :])
    a_ref[...] = a
    b_ref[...] = b


def _compress_rows(kv, w):
    bsz, t, _ = kv.shape
    tr = min(1024, t)
    ns = tr // CMP_STRIDE
    return pl.pallas_call(
        _compress_kernel,
        grid=(bsz, t // tr),
        in_specs=[pl.BlockSpec((None, tr, KV_W), lambda b, c: (b, c, 0)), _full_spec(w.shape)],
        out_specs=[pl.BlockSpec((None, ns, KV_W), lambda b, c: (b, c, 0))] * 2,
        out_shape=[jax.ShapeDtypeStruct((bsz, t // CMP_STRIDE, KV_W), F32)] * 2,
        compiler_params=_cparams(("arbitrary", "arbitrary")),
        name="compress_rows",
    )(kv, w)


def _compress_pages_kernel(pt_ref, *refs):
    del pt_ref
    pages = refs[:PAGES_PER_STEP]
    w_ref, a_ref, b_ref = refs[PAGES_PER_STEP:]
    spp = PAGE_SIZE // CMP_STRIDE
    for k, p in enumerate(pages):
        a, b = _segment_sums(p[...], w_ref[0:CMP_STRIDE, :], w_ref[CMP_STRIDE:CMP_LEN, :])
        a_ref[k * spp:(k + 1) * spp, :] = a
        b_ref[k * spp:(k + 1) * spp, :] = b


def _page_specs(layer, n_pages):
    def spec(k):
        return pl.BlockSpec((None, None, PAGE_SIZE, KV_W),
                            lambda b, j, pt: (layer, pt[b * n_pages + j * PAGES_PER_STEP + k], 0, 0))
    return [spec(k) for k in range(PAGES_PER_STEP)]


def _compress_pages(cache, pt_flat, w, layer, bsz, n_pages):
    spp = PAGE_SIZE // CMP_STRIDE
    ns = PAGES_PER_STEP * spp
    steps = n_pages // PAGES_PER_STEP
    grid_spec = pltpu.PrefetchScalarGridSpec(
        num_scalar_prefetch=1,
        grid=(bsz, steps),
        in_specs=_page_specs(layer, n_pages) + [pl.BlockSpec(w.shape, lambda b, j, pt: (0, 0))],
        out_specs=[pl.BlockSpec((None, ns, KV_W), lambda b, j, pt: (b, j, 0))] * 2,
    )
    return pl.pallas_call(
        _compress_pages_kernel,
        grid_spec=grid_spec,
        out_shape=[jax.ShapeDtypeStruct((bsz, n_pages * spp, KV_W), F32)] * 2,
        compiler_params=_cparams(("arbitrary", "arbitrary")),
        name="compress_pages",
    )(pt_flat, *([cache] * PAGES_PER_STEP), w)


def _row_slopes(g, n_q):
    r = lax.broadcasted_iota(jnp.int32, (GROUP * n_q, 1), 0) // n_q
    out = jnp.full((GROUP * n_q, 1), np.float32(SLOPES[g * GROUP + GROUP - 1]), F32)
    for k in range(GROUP - 2, -1, -1):
        out = jnp.where(r == k, np.float32(SLOPES[g * GROUP + k]), out)
    return out


def _stack_heads(q, g):
    return jnp.concatenate(
        [q[:, (g * GROUP + r) * HEAD_DIM:(g * GROUP + r + 1) * HEAD_DIM] for r in range(GROUP)], axis=0)


def _masked_softmax_rows(s, valid):
    s = jnp.where(valid, s, NEG_INF)
    m = jnp.max(s, axis=-1, keepdims=True)
    m = jnp.where(m == NEG_INF, 0.0, m)
    p = jnp.exp(s - m)
    return p / jnp.maximum(jnp.sum(p, axis=-1, keepdims=True), 1e-30)


def _flash_step(carry, s, valid, v):
    m, l, acc = carry
    s = jnp.where(valid, s, NEG_INF)
    m_new = jnp.maximum(m, jnp.max(s, axis=-1, keepdims=True))
    m_safe = jnp.where(m_new == NEG_INF, 0.0, m_new)
    alpha = jnp.exp(m - m_safe)
    p = jnp.exp(s - m_safe)
    l = alpha * l + jnp.sum(p, axis=-1, keepdims=True)
    acc = alpha * acc + _dot(p.astype(BF16), v)
    return m_new, l, acc


def _flash_init(rows):
    return (jnp.full((rows, 1), NEG_INF, F32), jnp.zeros((rows, 1), F32), jnp.zeros((rows, HEAD_DIM), F32))


def _flash_out(carry):
    _, l, acc = carry
    return acc / jnp.maximum(l, 1e-30)


def _select_blocks(score, n_extra_above=None):
    n_blk = score.shape[1]
    lane = lax.broadcasted_iota(jnp.int32, score.shape, 1)
    cnt = jnp.zeros(score.shape, jnp.int32) if n_extra_above is None else n_extra_above
    for k in range(n_blk):
        col = score[:, k:k + 1]
        cnt = cnt + jnp.where(lane > k, jnp.where(col >= score, 1, 0), jnp.where(col > score, 1, 0))
    return cnt < N_SELECT


def _block_sum_matrix(n_cmp, n_blk):
    c = lax.broadcasted_iota(jnp.int32, (n_cmp, n_blk), 0)
    j = lax.broadcasted_iota(jnp.int32, (n_cmp, n_blk), 1)
    return jnp.where(c // PER_SEL == j, 1.0, 0.0).astype(F32)


def _expand_matrix(n_blk, blk0, k0, kc):
    j = lax.broadcasted_iota(jnp.int32, (n_blk, kc), 0) + blk0
    c = lax.broadcasted_iota(jnp.int32, (n_blk, kc), 1) + k0
    return jnp.where(c // SEL_BLOCK == j, 1.0, 0.0).astype(BF16)


def _compressed_branch(qg, ckv, g, qpos, slope, n_valid):
    n_cmp = ckv.shape[0]
    ck = ckv[:, g * HEAD_DIM:(g + 1) * HEAD_DIM]
    cv = ckv[:, KV_W // 2 + g * HEAD_DIM:KV_W // 2 + (g + 1) * HEAD_DIM]
    col = lax.broadcasted_iota(jnp.int32, (1, n_cmp), 1)
    d = qpos - (col * CMP_STRIDE + (CMP_LEN - 1))
    s = _dot_nt(qg, ck) * QK_SCALE - slope * d.astype(F32)
    p = _masked_softmax_rows(s, (d >= 0) & (col < n_valid))
    return _dot(p.astype(BF16), cv), p


def _sum_heads(p, n_q):
    out = p[0:n_q]
    for r in range(1, GROUP):
        out = out + p[r * n_q:(r + 1) * n_q]
    return out


def _tile_rows(x):
    return jnp.concatenate([x] * GROUP, axis=0)


def _gate_and_pack(ng, branches, n_q):
    pieces = []
    for g in range(KV_HEADS):
        for r in range(GROUP):
            idx = g * GROUP + r
            acc = None
            for br, o in enumerate(branches[g]):
                term = ng[:, br * N_HEADS + idx:br * N_HEADS + idx + 1] * o[r * n_q:(r + 1) * n_q]
                acc = term if acc is None else acc + term
            pieces.append(acc)
    return jnp.concatenate(pieces, axis=1)


def _nsa_prompt_kernel(q_ref, ng_ref, ca_ref, cb_ref, ks_ref, kw_ref, o_ref, m_s, l_s, acc_s, *, t, qb, kc, wk):
    s0 = pl.program_id(1) * qb
    n_seg = ca_ref.shape[0]
    n_cmp_valid = (t - CMP_LEN) // CMP_STRIDE + 1
    n_blk = t // SEL_BLOCK
    rows = GROUP * qb
    qrow = s0 + lax.broadcasted_iota(jnp.int32, (qb, 1), 0)
    qpos = _tile_rows(qrow)

    ckv = (ca_ref[...] + pltpu.roll(cb_ref[...], n_seg - 1, axis=0)).astype(BF16)
    q = q_ref[...]
    ng = ng_ref[...]
    gsum = _block_sum_matrix(n_seg, n_blk)
    blk = lax.broadcasted_iota(jnp.int32, (1, n_blk), 1)
    cur = qrow // SEL_BLOCK
    forced = (blk == 0) | (blk == cur) | (blk == cur - 1)

    kstart = pl.multiple_of(jnp.clip(s0 + qb - wk, 0, t - wk), qb)
    n_chunks = (s0 + qb + kc - 1) // kc

    branches = []
    for g in range(KV_HEADS):
        qg = _stack_heads(q, g)
        slope = _row_slopes(g, qb)
        o_c, p_c = _compressed_branch(qg, ckv, g, qpos, slope, n_cmp_valid)

        imp = jnp.dot(_sum_heads(p_c, qb), gsum, preferred_element_type=F32, precision=lax.Precision.HIGHEST)
        score = jnp.where(forced, FORCE_BONUS, imp)
        score = jnp.where(blk <= cur, score, NEG_INF)
        selb = jnp.where(_select_blocks(score), 1.0, 0.0).astype(BF16)

        def body(c, carry, g=g, qg=qg, slope=slope, selb=selb):
            k0 = pl.multiple_of(c * kc, kc)
            kblk = ks_ref[pl.ds(k0, kc), :]
            k = kblk[:, g * HEAD_DIM:(g + 1) * HEAD_DIM]
            v = kblk[:, KV_W // 2 + g * HEAD_DIM:KV_W // 2 + (g + 1) * HEAD_DIM]
            d = qpos - (k0 + lax.broadcasted_iota(jnp.int32, (1, kc), 1))
            s = _dot_nt(qg, k) * QK_SCALE - slope * d.astype(F32)
            selk = _tile_rows(_dot(selb, _expand_matrix(n_blk, 0, k0, kc)))
            m, l, acc = _flash_step((m_s[...], l_s[...], acc_s[...]), s, (d >= 0) & (selk > 0.5), v)
            m_s[...] = m
            l_s[...] = l
            acc_s[...] = acc
            return carry

        m_s[...], l_s[...], acc_s[...] = _flash_init(rows)
        lax.fori_loop(0, n_chunks, body, 0)
        o_s = _flash_out((m_s[...], l_s[...], acc_s[...]))

        wblk = kw_ref[pl.ds(kstart, wk), :]
        kw_k = wblk[:, g * HEAD_DIM:(g + 1) * HEAD_DIM]
        kw_v = wblk[:, KV_W // 2 + g * HEAD_DIM:KV_W // 2 + (g + 1) * HEAD_DIM]
        d = qpos - (kstart + lax.broadcasted_iota(jnp.int32, (1, wk), 1))
        s = _dot_nt(qg, kw_k) * QK_SCALE - slope * d.astype(F32)
        p_w = _masked_softmax_rows(s, (d >= 0) & (d < WINDOW))
        o_w = _dot(p_w.astype(BF16), kw_v)
        branches.append([o_c, o_s, o_w])

    o_ref[...] = _gate_and_pack(ng, branches, qb).astype(BF16)


def _nsa_prompt(q, ng, ca, cb, ksb, kwb):
    bsz, t, _ = q.shape
    qb = min(ATT_QB, t)
    kc = min(ATT_KC, t)
    wk = min(WINDOW + qb, t)
    n_seg = ca.shape[1]
    per_b = lambda r, w: pl.BlockSpec((None, r, w), lambda b, i: (b, 0, 0))
    blk = lambda w: pl.BlockSpec((None, qb, w), lambda b, i: (b, i, 0))
    return pl.pallas_call(
        functools.partial(_nsa_prompt_kernel, t=t, qb=qb, kc=kc, wk=wk),
        grid=(bsz, t // qb),
        in_specs=[blk(N_HEADS * HEAD_DIM), blk(LANES), per_b(n_seg, KV_W), per_b(n_seg, KV_W),
                  per_b(t, KV_W), per_b(t, KV_W)],
        out_specs=blk(N_HEADS * HEAD_DIM),
        out_shape=jax.ShapeDtypeStruct((bsz, t, N_HEADS * HEAD_DIM), BF16),
        scratch_shapes=[pltpu.VMEM((GROUP * qb, 1), F32), pltpu.VMEM((GROUP * qb, 1), F32),
                        pltpu.VMEM((GROUP * qb, HEAD_DIM), F32)],
        compiler_params=_cparams(("arbitrary", "arbitrary")),
        name="nsa_prompt",
    )(q, ng, ca, cb, ksb, kwb)


def _nsa_sample_select_kernel(q_ref, ca_ref, cb_ref, oc_ref, sel_ref, *, ts, past):
    n_seg = ca_ref.shape[0]
    n_cmp_valid = n_seg - 1
    n_pblk = past // SEL_BLOCK
    qrow = past + lax.broadcasted_iota(jnp.int32, (ts, 1), 0)
    qpos = _tile_rows(qrow)
    ckv = (ca_ref[...] + pltpu.roll(cb_ref[...], n_seg - 1, axis=0)).astype(BF16)
    q = q_ref[...]
    gsum = _block_sum_matrix(n_seg, n_pblk)
    blk = lax.broadcasted_iota(jnp.int32, (1, n_pblk), 1)
    cur = qrow // SEL_BLOCK
    forced = (blk == 0) | (blk == cur) | (blk == cur - 1)
    outs = []
    for g in range(KV_HEADS):
        qg = _stack_heads(q, g)
        o_c, p_c = _compressed_branch(qg, ckv, g, qpos, _row_slopes(g, ts), n_cmp_valid)
        outs.append(o_c)
        imp = jnp.dot(_sum_heads(p_c, ts), gsum, preferred_element_type=F32, precision=lax.Precision.HIGHEST)
        score = jnp.where(forced, FORCE_BONUS, imp)
        above = jnp.where(FORCE_BONUS > score, 1, 0)
        sel_ref[g * ts:(g + 1) * ts, :] = jnp.where(_select_blocks(score, above), 1.0, 0.0)
    for g in range(KV_HEADS):
        oc_ref[g * GROUP * ts:(g + 1) * GROUP * ts, :] = outs[g]


def _nsa_sample_select(q, ca, cb, past):
    bsz, ts, _ = q.shape
    n_seg = ca.shape[1]
    n_pblk = past // SEL_BLOCK
    per_b = lambda r, w: pl.BlockSpec((None, r, w), lambda b: (b, 0, 0))
    return pl.pallas_call(
        functools.partial(_nsa_sample_select_kernel, ts=ts, past=past),
        grid=(bsz,),
        in_specs=[per_b(ts, N_HEADS * HEAD_DIM), per_b(n_seg, KV_W), per_b(n_seg, KV_W)],
        out_specs=[per_b(N_HEADS * ts, HEAD_DIM), per_b(KV_HEADS * ts, n_pblk)],
        out_shape=[jax.ShapeDtypeStruct((bsz, N_HEADS * ts, HEAD_DIM), F32),
                   jax.ShapeDtypeStruct((bsz, KV_HEADS * ts, n_pblk), F32)],
        compiler_params=_cparams(("arbitrary",)),
        name="nsa_sample_select",
    )(q, ca, cb)


def _nsa_sample_attend_kernel(pt_ref, *refs, ts, past, pad):
    del pt_ref
    pages = refs[:PAGES_PER_STEP]
    (q_ref, ng_ref, oc_ref, sel_ref, tail_ref, wbuf_ref, wnew_ref, o_ref,
     kbuf, m_s, l_s, acc_s) = refs[PAGES_PER_STEP:]
    j = pl.program_id(1)
    nj = pl.num_programs(1)
    rows = GROUP * ts
    kc = PAGES_PER_STEP * PAGE_SIZE
    bps = kc // SEL_BLOCK
    qrow = past + lax.broadcasted_iota(jnp.int32, (ts, 1), 0)
    qpos = _tile_rows(qrow)
    q = q_ref[...]

    @pl.when(j == 0)
    def _():
        m_s[...] = jnp.full(m_s.shape, NEG_INF, F32)
        l_s[...] = jnp.zeros(l_s.shape, F32)
        acc_s[...] = jnp.zeros(acc_s.shape, F32)

    for k, p in enumerate(pages):
        kbuf[k * PAGE_SIZE:(k + 1) * PAGE_SIZE, :] = p[...].astype(BF16)
    k0 = j * kc
    d = qpos - (k0 + lax.broadcasted_iota(jnp.int32, (1, kc), 1))
    expand = _expand_matrix(bps, 0, 0, kc)
    for g in range(KV_HEADS):
        qg = _stack_heads(q, g)
        slope = _row_slopes(g, ts)
        k = kbuf[:, g * HEAD_DIM:(g + 1) * HEAD_DIM]
        v = kbuf[:, KV_W // 2 + g * HEAD_DIM:KV_W // 2 + (g + 1) * HEAD_DIM]
        s = _dot_nt(qg, k) * QK_SCALE - slope * d.astype(F32)
        selk = _tile_rows(_dot(sel_ref[g * ts:(g + 1) * ts, :].astype(BF16), expand))
        sl = slice(g * rows, (g + 1) * rows)
        m, l, acc = _flash_step((m_s[sl, :], l_s[sl, :], acc_s[sl, :]), s, (d >= 0) & (selk > 0.5), v)
        m_s[sl, :] = m
        l_s[sl, :] = l
        acc_s[sl, :] = acc

    @pl.when(j == nj - 1)
    def _():
        ng = ng_ref[...]
        tail = tail_ref[...]
        wnew = wnew_ref[...]
        wbuf = wbuf_ref[...].astype(BF16)
        wb = wbuf.shape[0]
        colp = lax.broadcasted_iota(jnp.int32, (1, pad), 1)
        d_new = qpos - (past + colp)
        ok_new = (d_new >= 0) & (colp < ts)
        d_old = qpos - (past - wb + lax.broadcasted_iota(jnp.int32, (1, wb), 1))
        ok_old = (d_old >= 0) & (d_old < WINDOW)
        branches = []
        for g in range(KV_HEADS):
            qg = _stack_heads(q, g)
            slope = _row_slopes(g, ts)
            ksl = slice(g * HEAD_DIM, (g + 1) * HEAD_DIM)
            vsl = slice(KV_W // 2 + g * HEAD_DIM, KV_W // 2 + (g + 1) * HEAD_DIM)
            sl = slice(g * rows, (g + 1) * rows)
            s = _dot_nt(qg, tail[:, ksl]) * QK_SCALE - slope * d_new.astype(F32)
            o_s = _flash_out(_flash_step((m_s[sl, :], l_s[sl, :], acc_s[sl, :]), s, ok_new, tail[:, vsl]))
            s = _dot_nt(qg, wbuf[:, ksl]) * QK_SCALE - slope * d_old.astype(F32)
            carry = _flash_step(_flash_init(rows), s, ok_old, wbuf[:, vsl])
            s = _dot_nt(qg, wnew[:, ksl]) * QK_SCALE - slope * d_new.astype(F32)
            o_w = _flash_out(_flash_step(carry, s, ok_new & (d_new < WINDOW), wnew[:, vsl]))
            branches.append([oc_ref[sl, :], o_s, o_w])
        o_ref[...] = _gate_and_pack(ng, branches, ts).astype(BF16)


def _nsa_sample_attend(cache, pt_flat, layer, q, ng, oc, sel4, tail, wbuf, wnew, past):
    bsz, ts, _ = q.shape
    n_pages = past // PAGE_SIZE
    steps = n_pages // PAGES_PER_STEP
    pad = tail.shape[1]
    wb = wbuf.shape[2]
    bps = PAGES_PER_STEP * PAGE_SIZE // SEL_BLOCK
    per_b = lambda r, w: pl.BlockSpec((None, r, w), lambda b, j, pt: (b, 0, 0))
    grid_spec = pltpu.PrefetchScalarGridSpec(
        num_scalar_prefetch=1,
        grid=(bsz, steps),
        in_specs=_page_specs(layer, n_pages) + [
            per_b(ts, N_HEADS * HEAD_DIM), per_b(ts, LANES), per_b(N_HEADS * ts, HEAD_DIM),
            pl.BlockSpec((None, None, KV_HEADS * ts, bps), lambda b, j, pt: (b, j, 0, 0)),
            per_b(pad, KV_W),
            pl.BlockSpec((None, None, wb, KV_W), lambda b, j, pt: (layer, b, 0, 0)),
            per_b(pad, KV_W)],
        out_specs=per_b(ts, N_HEADS * HEAD_DIM),
        scratch_shapes=[pltpu.VMEM((PAGES_PER_STEP * PAGE_SIZE, KV_W), BF16),
                        pltpu.VMEM((N_HEADS * ts, 1), F32), pltpu.VMEM((N_HEADS * ts, 1), F32),
                        pltpu.VMEM((N_HEADS * ts, HEAD_DIM), F32)],
    )
    return pl.pallas_call(
        functools.partial(_nsa_sample_attend_kernel, ts=ts, past=past, pad=pad),
        grid_spec=grid_spec,
        out_shape=jax.ShapeDtypeStruct((bsz, ts, N_HEADS * HEAD_DIM), BF16),
        compiler_params=_cparams(("arbitrary", "arbitrary")),
        name="nsa_sample_attend",
    )(pt_flat, *([cache] * PAGES_PER_STEP), q, ng, oc, sel4, tail, wbuf, wnew)


def _layer_norm(x, g, b):
    mu = jnp.mean(x, axis=-1, keepdims=True)
    xc = x - mu
    var = jnp.mean(xc * xc, axis=-1, keepdims=True)
    return xc * lax.rsqrt(var + LN_EPS) * g + b


def _merge_kernel(x_ref, yr_ref, ya_ref, mg_ref, wa_ref, wb_ref, wo_ref, g_ref, b_ref, wr_ref, br_ref,
                  o_ref, ob_ref, comb_ref):
    a = _dot(yr_ref[...], wa_ref[...])
    b = _dot(ya_ref[...], wb_ref[...])
    merged = mg_ref[:, 0:D_MODEL] * a + mg_ref[:, D_MODEL:2 * D_MODEL] * b
    mix = _dot(merged.astype(BF16), wo_ref[...])
    x1 = _layer_norm(ALPHA * x_ref[...] + mix, g_ref[...], b_ref[...])
    o_ref[...] = x1
    x1b = x1.astype(BF16)
    ob_ref[...] = x1b
    comb_ref[...] = _route(_dot(x1b, wr_ref[...]) + br_ref[...])


def _route(logits):
    n = logits.shape[0]
    lane = lax.broadcasted_iota(jnp.int32, (n, LANES), 1)
    is_g = lane < N_GROUPS
    gl = jnp.where(is_g, logits, NEG_INF)
    gmax = jnp.max(gl, axis=-1, keepdims=True)
    grp = jnp.min(jnp.where(gl == gmax, lane, LANES), axis=-1, keepdims=True)
    gw = 1.0 / jnp.sum(jnp.exp(gl - gmax), axis=-1, keepdims=True)
    e_idx = lane - N_GROUPS
    in_grp = (e_idx >= grp * EXP_PER_GROUP) & (e_idx < (grp + 1) * EXP_PER_GROUP)
    el = jnp.where(in_grp, logits, NEG_INF)
    v1 = jnp.max(el, axis=-1, keepdims=True)
    i1 = jnp.min(jnp.where(el == v1, lane, LANES), axis=-1, keepdims=True)
    el2 = jnp.where(lane == i1, NEG_INF, el)
    v2 = jnp.max(el2, axis=-1, keepdims=True)
    i2 = jnp.min(jnp.where(el2 == v2, lane, LANES), axis=-1, keepdims=True)
    e2 = jnp.exp(v2 - v1)
    w1 = gw / (1.0 + e2)
    w2 = gw * e2 / (1.0 + e2)
    comb = jnp.where(lane == i1, w1, 0.0) + jnp.where(lane == i2, w2, 0.0)
    return pltpu.roll(comb, LANES - N_GROUPS, axis=1)


def _merge_out(x2d, yr, ya, mg, wa, wb, wo, g, b, wr, br):
    n = x2d.shape[0]
    tm = min(MERGE_TM, n)
    row = lambda w: pl.BlockSpec((tm, w), lambda i: (i, 0))
    sq = _full_spec((D_MODEL, D_MODEL))
    vec = _full_spec((1, D_MODEL))
    return pl.pallas_call(
        _merge_kernel,
        grid=(n // tm,),
        in_specs=[row(D_MODEL), row(D_RNN), row(N_HEADS * HEAD_DIM), row(2 * D_MODEL), sq, sq, sq, vec, vec,
                  _full_spec((D_MODEL, LANES)), _full_spec((1, LANES))],
        out_specs=[row(D_MODEL), row(D_MODEL), row(LANES)],
        out_shape=[jax.ShapeDtypeStruct((n, D_MODEL), F32), jax.ShapeDtypeStruct((n, D_MODEL), BF16),
                   jax.ShapeDtypeStruct((n, LANES), F32)],
        compiler_params=_cparams(("arbitrary",)),
        name="merge_out",
    )(x2d, yr, ya, mg, wa, wb, wo, g, b, wr, br)


def _moe_kernel(x_ref, xb_ref, comb_ref, w1_ref, w3_ref, w2_ref, g_ref, b_ref, o_ref, acc):
    e = pl.program_id(1)
    ne = pl.num_programs(1)

    @pl.when(e == 0)
    def _():
        acc[...] = jnp.zeros(acc.shape, F32)

    xb = xb_ref[...]
    lane = lax.broadcasted_iota(jnp.int32, comb_ref.shape, 1)
    cw = jnp.sum(jnp.where(lane == e, comb_ref[...], 0.0), axis=-1, keepdims=True)
    h1 = _dot(xb, w1_ref[...])
    h3 = _dot(xb, w3_ref[...])
    h = (h1 * _sigmoid(h1)) * h3 * cw
    acc[...] += _dot(h.astype(BF16), w2_ref[...])

    @pl.when(e == ne - 1)
    def _():
        o_ref[...] = _layer_norm(ALPHA * x_ref[...] + acc[...], g_ref[...], b_ref[...])


def _moe(x1, x1b, comb, w1, w3, w2, g, b):
    n = x1.shape[0]
    tm = min(MOE_TM, n)
    row = lambda w: pl.BlockSpec((tm, w), lambda i, e: (i, 0))
    vec = pl.BlockSpec((1, D_MODEL), lambda i, e: (0, 0))
    return pl.pallas_call(
        _moe_kernel,
        grid=(n // tm, N_EXPERTS),
        in_specs=[row(D_MODEL), row(D_MODEL), row(LANES),
                  pl.BlockSpec((None, D_MODEL, D_EXPERT), lambda i, e: (e, 0, 0)),
                  pl.BlockSpec((None, D_MODEL, D_EXPERT), lambda i, e: (e, 0, 0)),
                  pl.BlockSpec((None, D_EXPERT, D_MODEL), lambda i, e: (e, 0, 0)),
                  vec, vec],
        out_specs=row(D_MODEL),
        out_shape=jax.ShapeDtypeStruct((n, D_MODEL), F32),
        scratch_shapes=[pltpu.VMEM((tm, D_MODEL), F32)],
        compiler_params=_cparams(("arbitrary", "arbitrary")),
        name="moe",
    )(x1, x1b, comb, w1, w3, w2, g, b)


def _block_diag4(w):
    w4 = w.reshape(N_RNN_BLOCKS // 4, 4, RNN_BLOCK, RNN_BLOCK)
    eye = jnp.eye(4, dtype=w.dtype)
    return jnp.einsum("kaij,ab->kaibj", w4, eye).reshape(N_RNN_BLOCKS // 4, 256, 256).astype(BF16)


def _layer_params(l, w_in, conv_w, conv_b, lru_wa, lru_ba, lru_wx, lru_bx, lru_lambda, cmp_w,
                  w_br_a, w_br_b, w_out, ln1_g, ln1_b, ln2_g, ln2_b,
                  router_wg, router_bg, router_we, router_be, exp_w1, exp_w3, exp_w2):
    n_main = 3 * D_RNN + 3 * KV_W
    n_ng = 3 * N_HEADS
    w = w_in[l]
    vec = lambda a: a[l].reshape(1, -1)
    wr = jnp.concatenate([router_wg[l], router_we[l]], axis=1)
    br = jnp.concatenate([router_bg[l], router_be[l]])
    npad = LANES - wr.shape[1]
    return dict(
        wm=w[:, :n_main].astype(BF16),
        wng=jnp.pad(w[:, n_main:n_main + n_ng], ((0, 0), (0, LANES - n_ng))).astype(BF16),
        wmg=w[:, n_main + n_ng:].astype(BF16),
        cw=conv_w[l], cb=vec(conv_b),
        wa=_block_diag4(lru_wa[l]), ba=vec(lru_ba), wx=_block_diag4(lru_wx[l]), bx=vec(lru_bx),
        lam=vec(lru_lambda),
        cmpw=jnp.broadcast_to(cmp_w[l][:, :, None, :], (CMP_LEN, 2, KV_HEADS, HEAD_DIM)).reshape(CMP_LEN, KV_W),
        wbra=w_br_a[l].astype(BF16), wbrb=w_br_b[l].astype(BF16), wout=w_out[l].astype(BF16),
        ln1g=vec(ln1_g), ln1b=vec(ln1_b), ln2g=vec(ln2_g), ln2b=vec(ln2_b),
        wr=jnp.pad(wr, ((0, 0), (0, npad))).astype(BF16), br=jnp.pad(br, (0, npad)).reshape(1, LANES),
        w1=exp_w1[l].astype(BF16), w3=exp_w3[l].astype(BF16), w2=exp_w2[l].astype(BF16),
    )


def _mixer_tail(p, x2d, y_rnn, y_att, mg):
    n = x2d.shape[0]
    x1, x1b, comb = _merge_out(x2d, y_rnn.reshape(n, D_RNN), y_att.reshape(n, -1), mg,
                               p["wbra"], p["wbrb"], p["wout"], p["ln1g"], p["ln1b"], p["wr"], p["br"])
    return _moe(x1, x1b, comb, p["w1"], p["w3"], p["w2"], p["ln2g"], p["ln2b"])


def kernel(x_prompt, x_sample, cache_cmp, cache_sel, cache_win, state_conv, state_lru, page_table,
           w_in, conv_w, conv_b, lru_wa, lru_ba, lru_wx, lru_bx, lru_lambda, cmp_w,
           w_br_a, w_br_b, w_out, ln1_g, ln1_b, ln2_g, ln2_b,
           router_wg, router_bg, router_we, router_be, exp_w1, exp_w3, exp_w2):
    bp, tp, _ = x_prompt.shape
    bs, ts, _ = x_sample.shape
    n_pages = page_table.shape[1]
    past = n_pages * PAGE_SIZE
    assert ts <= SEL_BLOCK and n_pages % PAGES_PER_STEP == 0 and tp % SEL_BLOCK == 0
    depth = w_in.shape[0]
    n_pool = cache_cmp.shape[1]
    ccmp = cache_cmp.reshape(depth, n_pool, PAGE_SIZE, KV_W)
    csel = cache_sel.reshape(depth, n_pool, PAGE_SIZE, KV_W)
    wb = cache_win.shape[2]
    cwin = cache_win.reshape(depth, bs, wb, KV_W)
    pt_flat = page_table.reshape(-1).astype(jnp.int32)
    pad = LANES
    kv6 = lambda a, b, t: a.reshape(b, t, 2, KV_HEADS, HEAD_DIM)

    xp = x_prompt.reshape(bp * tp, D_MODEL)
    xs = x_sample.reshape(bs * ts, D_MODEL)
    outs = [[] for _ in range(10)]
    for l in range(depth):
        p = _layer_params(l, w_in, conv_w, conv_b, lru_wa, lru_ba, lru_wx, lru_bx, lru_lambda, cmp_w,
                          w_br_a, w_br_b, w_out, ln1_g, ln1_b, ln2_g, ln2_b,
                          router_wg, router_bg, router_we, router_be, exp_w1, exp_w3, exp_w2)
        lru = (p["cw"], p["cb"], p["wa"], p["ba"], p["wx"], p["bx"], p["lam"])

        xr, gr, q, kvc, kvs, kvw, kvsb, kvwb, ng, mg = _project_in(xp, p["wm"], p["wng"], p["wmg"])
        r3 = lambda a: a.reshape(bp, tp, -1)
        y_rnn, cbuf, ht = _rglru(r3(xr), r3(gr), jnp.zeros((bp, CONV_W - 1, D_RNN), F32),
                                 jnp.zeros((bp, 1, D_RNN), F32), *lru, pos0=0)
        ca, cb = _compress_rows(r3(kvc), p["cmpw"])
        y_att = _nsa_prompt(r3(q), r3(ng), ca, cb, r3(kvsb), r3(kvwb))
        xp = _mixer_tail(p, xp, y_rnn, y_att, mg)
        wlen = min(WINDOW, tp)
        outs[0].append(kv6(kvc, bp, tp))
        outs[2].append(kv6(kvs, bp, tp))
        outs[4].append(kv6(kvw, bp, tp)[:, tp - wlen:])
        outs[6].append(cbuf)
        outs[8].append(ht.reshape(bp, D_RNN))

        xr, gr, q, kvc, kvs, kvw, kvsb, kvwb, ng, mg = _project_in(xs, p["wm"], p["wng"], p["wmg"])
        r3 = lambda a: a.reshape(bs, ts, -1)
        y_rnn, cbuf, ht = _rglru(r3(xr), r3(gr), state_conv[l], state_lru[l].reshape(bs, 1, D_RNN),
                                 *lru, pos0=past)
        ca, cb = _compress_pages(ccmp, pt_flat, p["cmpw"], l, bs, n_pages)
        oc, sel = _nsa_sample_select(r3(q), ca, cb, past)
        steps = n_pages // PAGES_PER_STEP
        sel4 = sel.reshape(bs, KV_HEADS * ts, steps, -1).transpose(0, 2, 1, 3)
        padrows = lambda a: jnp.pad(r3(a), ((0, 0), (0, pad - ts), (0, 0)))
        y_att = _nsa_sample_attend(csel, pt_flat, l, r3(q), r3(ng), oc, sel4, padrows(kvsb), cwin, padrows(kvwb),
                                   past)
        xs = _mixer_tail(p, xs, y_rnn, y_att, mg)
        outs[1].append(kv6(kvc, bs, ts))
        outs[3].append(kv6(kvs, bs, ts))
        kw_all = jnp.concatenate([cache_win[l], kv6(kvw, bs, ts)], axis=1)
        outs[5].append(kw_all[:, ts:])
        outs[7].append(cbuf)
        outs[9].append(ht.reshape(bs, D_RNN))

    st = [jnp.stack(o) for o in outs]
    return (xp.reshape(bp, tp, D_MODEL), xs.reshape(bs, ts, D_MODEL),
            st[0], st[1], st[2], st[3], st[4], st[5], st[6], st[7], st[8], st[9])
```

```python
import functools

import jax
import jax.numpy as jnp
import numpy as np
from jax import lax
from jax.experimental import pallas as pl
from jax.experimental.pallas import tpu as pltpu

F32 = jnp.float32
BF16 = jnp.bfloat16
NEG_INF = float("-inf")

D_MODEL = 1024
D_RNN = 1024
N_RNN_BLOCKS = 16
RNN_BLOCK = D_RNN // N_RNN_BLOCKS
CONV_W = 4
LRU_C = 8.0
N_HEADS = 16
HEAD_DIM = 64
KV_HEADS = 4
GROUP = N_HEADS // KV_HEADS
CMP_LEN = 32
CMP_STRIDE = 16
SEL_BLOCK = 64
N_SELECT = 16
WINDOW = 512
FORCE_BONUS = 1e4
PAGE_SIZE = 128
N_GROUPS = 4
EXP_PER_GROUP = 8
N_EXPERTS = N_GROUPS * EXP_PER_GROUP
D_EXPERT = 256
DEPTH = 2
ALPHA = (2.0 * DEPTH) ** 0.25
LN_EPS = 1e-5
KV_W = 2 * KV_HEADS * HEAD_DIM
QK_SCALE = HEAD_DIM ** -0.5
PER_SEL = SEL_BLOCK // CMP_STRIDE
SLOPES = [2.0 ** (-8.0 * (h + 1) / N_HEADS) for h in range(N_HEADS)]

LANES = 128
VMEM_LIMIT_BYTES = 56 * 1024 * 1024

PROJ_TM = 256
LRU_TC = 256
MERGE_TM = 512
MOE_TM = 1024
ATT_QB = 128
ATT_KC = 512
PAGES_PER_STEP = 8
CMP_PAGES_PER_STEP = 16


def _cparams(sem):
    return pltpu.CompilerParams(dimension_semantics=sem, vmem_limit_bytes=VMEM_LIMIT_BYTES)


def _full_spec(shape):
    nd = len(shape)
    return pl.BlockSpec(shape, lambda *_: (0,) * nd)


def _dot(a, b):
    return jnp.dot(a, b, preferred_element_type=F32)


def _dot_nt(a, b):
    return lax.dot_general(a, b, (((1,), (1,)), ((), ())), preferred_element_type=F32)


def _sigmoid(x):
    return 1.0 / (1.0 + jnp.exp(-x))


def _proj_kernel(x_ref, wm_ref, wng_ref, wmg_ref,
                 xr_ref, gr_ref, q_ref, kvc_ref, kvs_ref, kvw_ref, kvsb_ref, kvwb_ref, ng_ref, mg_ref):
    xb = x_ref[...].astype(BF16)
    xr_ref[...] = _dot(xb, wm_ref[:, 0:D_RNN])
    gr_ref[...] = _dot(xb, wm_ref[:, D_RNN:2 * D_RNN])
    q_ref[...] = _dot(xb, wm_ref[:, 2 * D_RNN:3 * D_RNN]).astype(BF16)
    o = 3 * D_RNN
    kvc_ref[...] = _dot(xb, wm_ref[:, o:o + KV_W])
    kvs = _dot(xb, wm_ref[:, o + KV_W:o + 2 * KV_W])
    kvs_ref[...] = kvs
    kvsb_ref[...] = kvs.astype(BF16)
    kvw = _dot(xb, wm_ref[:, o + 2 * KV_W:o + 3 * KV_W])
    kvw_ref[...] = kvw
    kvwb_ref[...] = kvw.astype(BF16)
    ng_ref[...] = _sigmoid(_dot(xb, wng_ref[...]))
    mg_ref[...] = _sigmoid(_dot(xb, wmg_ref[...]))


def _project_in(x2d, wm, wng, wmg):
    n = x2d.shape[0]
    tm = min(PROJ_TM, n)
    row = lambda w: pl.BlockSpec((tm, w), lambda i: (i, 0))
    outs = [(D_RNN, F32), (D_RNN, F32), (N_HEADS * HEAD_DIM, BF16), (KV_W, F32), (KV_W, F32), (KV_W, F32),
            (KV_W, BF16), (KV_W, BF16), (LANES, F32), (2 * D_MODEL, F32)]
    return pl.pallas_call(
        _proj_kernel,
        grid=(n // tm,),
        in_specs=[row(D_MODEL), _full_spec(wm.shape), _full_spec(wng.shape), _full_spec(wmg.shape)],
        out_specs=[row(w) for w, _ in outs],
        out_shape=[jax.ShapeDtypeStruct((n, w), dt) for w, dt in outs],
        compiler_params=_cparams(("arbitrary",)),
        name="project_in",
    )(x2d, wm, wng, wmg)


def _shift_rows(x, s, fill):
    row = lax.broadcasted_iota(jnp.int32, x.shape, 0)
    return jnp.where(row >= s, pltpu.roll(x, s, axis=0), fill)


def _gelu_tanh(x):
    c = np.float32(np.sqrt(2.0 / np.pi))
    return 0.5 * x * (1.0 + jnp.tanh(c * (x + np.float32(0.044715) * (x * x * x))))


def _rglru_kernel(xr_ref, gr_ref, cbuf_ref, h0_ref, cw_ref, cb_ref, wa_ref, ba_ref, wx_ref, bx_ref, lam_ref,
                  y_ref, nbuf_ref, ht_ref, xbuf, hcar, *, tc, pos0):
    c = pl.program_id(1)
    nc = pl.num_programs(1)

    @pl.when(c == 0)
    def _():
        xbuf[0:8, :] = jnp.zeros((8, D_RNN), F32)
        xbuf[8 - (CONV_W - 1):8, :] = cbuf_ref[...]
        hcar[...] = h0_ref[...]

    x = xr_ref[...]
    xbuf[8:8 + tc, :] = x
    conv = cb_ref[...] + cw_ref[CONV_W - 1:CONV_W, :] * x
    for j in range(1, CONV_W):
        conv = conv + cw_ref[CONV_W - 1 - j:CONV_W - j, :] * xbuf[8 - j:8 - j + tc, :]
    nbuf = xbuf[8 + tc - (CONV_W - 1):8 + tc, :]
    xbuf[0:8, :] = xbuf[tc:tc + 8, :]

    cvb = conv.astype(BF16)
    nb4 = D_RNN // 256
    za = jnp.concatenate([_dot(cvb[:, k * 256:(k + 1) * 256], wa_ref[k]) for k in range(nb4)], axis=1)
    zx = jnp.concatenate([_dot(cvb[:, k * 256:(k + 1) * 256], wx_ref[k]) for k in range(nb4)], axis=1)
    r = _sigmoid(za + ba_ref[...])
    i = _sigmoid(zx + bx_ref[...])
    nl = -lam_ref[...]
    softplus = jnp.maximum(nl, 0.0) + jnp.log(1.0 + jnp.exp(-jnp.abs(nl)))
    log_a = (-LRU_C) * r * softplus
    a = jnp.exp(log_a)
    pos = pos0 + c * tc + lax.broadcasted_iota(jnp.int32, (tc, 1), 0)
    mult = jnp.where(pos == 0, 1.0, jnp.sqrt(1.0 - jnp.exp(2.0 * log_a)))
    b = mult * i * conv

    s = 1
    while s < tc:
        a_sh = _shift_rows(a, s, 1.0)
        b_sh = _shift_rows(b, s, 0.0)
        b = a * b_sh + b
        a = a * a_sh
        s *= 2
    h = b + a * hcar[...]
    hcar[...] = h[tc - 1:tc, :]
    y_ref[...] = (_gelu_tanh(gr_ref[...]) * h).astype(BF16)

    @pl.when(c == nc - 1)
    def _():
        nbuf_ref[...] = nbuf
        ht_ref[...] = h[tc - 1:tc, :]


def _rglru(xr, gr, conv_buf, h0, cw, cb, wa_bd, ba, wx_bd, bx, lam, pos0):
    bsz, t, _ = xr.shape
    tc = min(LRU_TC, t)
    seq = pl.BlockSpec((None, tc, D_RNN), lambda b, c: (b, c, 0))
    per_b = lambda r: pl.BlockSpec((None, r, D_RNN), lambda b, c: (b, 0, 0))
    vec = pl.BlockSpec((1, D_RNN), lambda b, c: (0, 0))
    bd = pl.BlockSpec(wa_bd.shape, lambda b, c: (0, 0, 0))
    return pl.pallas_call(
        functools.partial(_rglru_kernel, tc=tc, pos0=pos0),
        grid=(bsz, t // tc),
        in_specs=[seq, seq, per_b(CONV_W - 1), per_b(1),
                  pl.BlockSpec((CONV_W, D_RNN), lambda b, c: (0, 0)), vec, bd, vec, bd, vec, vec],
        out_specs=[seq, per_b(CONV_W - 1), per_b(1)],
        out_shape=[jax.ShapeDtypeStruct((bsz, t, D_RNN), BF16),
                   jax.ShapeDtypeStruct((bsz, CONV_W - 1, D_RNN), F32),
                   jax.ShapeDtypeStruct((bsz, 1, D_RNN), F32)],
        scratch_shapes=[pltpu.VMEM((tc + 8, D_RNN), F32), pltpu.VMEM((1, D_RNN), F32)],
        compiler_params=_cparams(("arbitrary", "arbitrary")),
        name="rglru",
    )(xr, gr, conv_buf, h0, cw, cb, wa_bd, ba, wx_bd, bx, lam)


def _segment_sums(x, w1, w2):
    r = x.shape[0]
    x3 = x.reshape(r // CMP_STRIDE, CMP_STRIDE, KV_W)
    return jnp.sum(x3 * w1[None], axis=1), jnp.sum(x3 * w2[None], axis=1)


def _compress_kernel(kv_ref, w_ref, a_ref, b_ref):
    a, b = _segment_sums(kv_ref[...], w_ref[0:CMP_STRIDE, :], w_ref[CMP_STRIDE:CMP_LEN, :])
    a_ref[...] = a
    b_ref[...] = b


def _compress_rows(kv, w):
    bsz, t, _ = kv.shape
    tr = min(1024, t)
    ns = tr // CMP_STRIDE
    return pl.pallas_call(
        _compress_kernel,
        grid=(bsz, t // tr),
        in_specs=[pl.BlockSpec((None, tr, KV_W), lambda b, c: (b, c, 0)), _full_spec(w.shape)],
        out_specs=[pl.BlockSpec((None, ns, KV_W), lambda b, c: (b, c, 0))] * 2,
        out_shape=[jax.ShapeDtypeStruct((bsz, t // CMP_STRIDE, KV_W), F32)] * 2,
        compiler_params=_cparams(("arbitrary", "arbitrary")),
        name="compress_rows",
    )(kv, w)


def _split_bf16(y):
    hi = y.astype(BF16)
    return hi, (y - hi.astype(F32)).astype(BF16)


def _compress_pages_kernel(pt_ref, *refs):
    del pt_ref
    pages = refs[:CMP_PAGES_PER_STEP]
    w_ref, seg_ref, a_ref, b_ref = refs[CMP_PAGES_PER_STEP:]
    seg = seg_ref[...]
    for role, o_ref in enumerate((a_ref, b_ref)):
        rows = []
        for x in range(2):
            wt = w_ref[role, x]
            for g in range(KV_HEADS):
                rows.append(jnp.concatenate([p[x, g] * wt for p in pages], axis=1))
        hi, lo = _split_bf16(jnp.concatenate(rows, axis=0))
        z = _dot(hi, seg) + _dot(lo, seg)
        for x in range(2):
            for g in range(KV_HEADS):
                i = x * KV_HEADS + g
                o_ref[x, g] = z[i * HEAD_DIM:(i + 1) * HEAD_DIM, :]


def _page_specs(layer, n_pages, per_step):
    def spec(k):
        return pl.BlockSpec((None, None, 2, KV_HEADS, HEAD_DIM, PAGE_SIZE),
                            lambda b, j, pt: (layer, pt[b * n_pages + j * per_step + k], 0, 0, 0, 0))
    return [spec(k) for k in range(per_step)]


def _compress_pages(cache_t, pt_flat, wt, layer, bsz, n_pages):
    spp = PAGE_SIZE // CMP_STRIDE
    per_step = CMP_PAGES_PER_STEP
    assert per_step * spp == LANES
    steps = n_pages // per_step
    kdim = per_step * PAGE_SIZE
    lrow = lax.broadcasted_iota(jnp.int32, (kdim, LANES), 0)
    lcol = lax.broadcasted_iota(jnp.int32, (kdim, LANES), 1)
    seg = jnp.where(lrow // CMP_STRIDE == lcol, 1.0, 0.0).astype(BF16)
    out_block = pl.BlockSpec((None, 2, KV_HEADS, HEAD_DIM, LANES), lambda b, j, pt: (b, 0, 0, 0, j))
    grid_spec = pltpu.PrefetchScalarGridSpec(
        num_scalar_prefetch=1,
        grid=(bsz, steps),
        in_specs=_page_specs(layer, n_pages, per_step) + [
            pl.BlockSpec(wt.shape, lambda b, j, pt: (0, 0, 0, 0)), pl.BlockSpec(seg.shape, lambda b, j, pt: (0, 0))],
        out_specs=[out_block] * 2,
    )
    return pl.pallas_call(
        _compress_pages_kernel,
        grid_spec=grid_spec,
        out_shape=[jax.ShapeDtypeStruct((bsz, 2, KV_HEADS, HEAD_DIM, n_pages * spp), F32)] * 2,
        compiler_params=_cparams(("arbitrary", "arbitrary")),
        name="compress_pages",
    )(pt_flat, *([cache_t] * per_step), wt, seg)


def _row_slopes(g, n_q):
    r = lax.broadcasted_iota(jnp.int32, (GROUP * n_q, 1), 0) // n_q
    out = jnp.full((GROUP * n_q, 1), np.float32(SLOPES[g * GROUP + GROUP - 1]), F32)
    for k in range(GROUP - 2, -1, -1):
        out = jnp.where(r == k, np.float32(SLOPES[g * GROUP + k]), out)
    return out


def _stack_heads(q, g):
    return jnp.concatenate(
        [q[:, (g * GROUP + r) * HEAD_DIM:(g * GROUP + r + 1) * HEAD_DIM] for r in range(GROUP)], axis=0)


def _masked_softmax_rows(s, valid):
    s = jnp.where(valid, s, NEG_INF)
    m = jnp.max(s, axis=-1, keepdims=True)
    m = jnp.where(m == NEG_INF, 0.0, m)
    p = jnp.exp(s - m)
    return p / jnp.maximum(jnp.sum(p, axis=-1, keepdims=True), 1e-30)


def _flash_step(carry, s, valid, v, v_transposed=False):
    m, l, acc = carry
    s = jnp.where(valid, s, NEG_INF)
    m_new = jnp.maximum(m, jnp.max(s, axis=-1, keepdims=True))
    m_safe = jnp.where(m_new == NEG_INF, 0.0, m_new)
    alpha = jnp.exp(m - m_safe)
    p = jnp.exp(s - m_safe)
    l = alpha * l + jnp.sum(p, axis=-1, keepdims=True)
    pv = _dot_nt(p.astype(BF16), v) if v_transposed else _dot(p.astype(BF16), v)
    return m_new, l, alpha * acc + pv


def _flash_init(rows):
    return (jnp.full((rows, 1), NEG_INF, F32), jnp.zeros((rows, 1), F32), jnp.zeros((rows, HEAD_DIM), F32))


def _flash_out(carry):
    _, l, acc = carry
    return acc / jnp.maximum(l, 1e-30)


def _select_blocks(score, n_extra_above=None):
    n_blk = score.shape[1]
    lane = lax.broadcasted_iota(jnp.int32, score.shape, 1)
    cnt = jnp.zeros(score.shape, jnp.int32) if n_extra_above is None else n_extra_above
    for k in range(n_blk):
        col = score[:, k:k + 1]
        cnt = cnt + jnp.where(lane > k, jnp.where(col >= score, 1, 0), jnp.where(col > score, 1, 0))
    return cnt < N_SELECT


def _block_sum_matrix(n_cmp, n_blk):
    c = lax.broadcasted_iota(jnp.int32, (n_cmp, n_blk), 0)
    j = lax.broadcasted_iota(jnp.int32, (n_cmp, n_blk), 1)
    return jnp.where(c // PER_SEL == j, 1.0, 0.0).astype(F32)


def _expand_matrix(n_blk, blk0, k0, kc):
    j = lax.broadcasted_iota(jnp.int32, (n_blk, kc), 0) + blk0
    c = lax.broadcasted_iota(jnp.int32, (n_blk, kc), 1) + k0
    return jnp.where(c // SEL_BLOCK == j, 1.0, 0.0).astype(BF16)


def _sum_heads(p, n_q):
    out = p[0:n_q]
    for r in range(1, GROUP):
        out = out + p[r * n_q:(r + 1) * n_q]
    return out


def _tile_rows(x):
    return jnp.concatenate([x] * GROUP, axis=0)


def _gate_and_pack(ng, branches, n_q):
    pieces = []
    for g in range(KV_HEADS):
        for r in range(GROUP):
            idx = g * GROUP + r
            acc = None
            for br, o in enumerate(branches[g]):
                term = ng[:, br * N_HEADS + idx:br * N_HEADS + idx + 1] * o[r * n_q:(r + 1) * n_q]
                acc = term if acc is None else acc + term
            pieces.append(acc)
    return jnp.concatenate(pieces, axis=1)


def _softmax_rows(s):
    m = jnp.max(s, axis=-1, keepdims=True)
    m = jnp.where(m == NEG_INF, 0.0, m)
    p = jnp.exp(s - m)
    return p / jnp.maximum(jnp.sum(p, axis=-1, keepdims=True), 1e-30)


def _add_head_bias(s, table_ref, g, mask, n_q):
    return jnp.concatenate(
        [s[r * n_q:(r + 1) * n_q] + table_ref[g * GROUP + r] + mask for r in range(GROUP)], axis=0)


def _select_blocks_t(score_t):
    n_blk = score_t.shape[0]
    slab = 8
    slabs = [score_t[i:i + slab] for i in range(0, n_blk, slab)]
    cnts = [jnp.zeros(s.shape, jnp.int32) for s in slabs]
    row = lax.broadcasted_iota(jnp.int32, slabs[0].shape, 0)
    for k in range(n_blk):
        pivot = score_t[k:k + 1, :]
        for i, s in enumerate(slabs):
            lo = i * slab
            if lo > k:
                beats = jnp.where(pivot >= s, 1, 0)
            elif lo + slab - 1 < k:
                beats = jnp.where(pivot > s, 1, 0)
            else:
                beats = jnp.where(row + lo > k, jnp.where(pivot >= s, 1, 0), jnp.where(pivot > s, 1, 0))
            cnts[i] = cnts[i] + beats
    cnt = jnp.concatenate(cnts, axis=0)
    return jnp.where(cnt < N_SELECT, 1.0, 0.0)


def _nsa_prompt_kernel(q_ref, ng_ref, ca_ref, cb_ref, ks_ref, kw_ref, o_ref,
                       bsel, bwin, bcmp, efull, qs_s, selb_s, oc_s, ow_s, m_s, l_s, acc_s, *, t, qb, kc, wk):
    n_seg = ca_ref.shape[0]
    n_cmp_valid = (t - CMP_LEN) // CMP_STRIDE + 1
    n_blk = t // SEL_BLOCK
    qi = lax.broadcasted_iota(jnp.int32, (qb, 1), 0)

    @pl.when((pl.program_id(0) == 0) & (pl.program_id(1) == 0))
    def _():
        def rel(n, stride, off):
            c = lax.broadcasted_iota(jnp.int32, (1, n), 1)
            return (qi - (c * stride + off)).astype(F32)
        d_sel, d_win, d_cmp = rel(kc, 1, 0), rel(wk, 1, 0), rel(n_seg, CMP_STRIDE, CMP_LEN - 1)
        for h in range(N_HEADS):
            slope = np.float32(SLOPES[h])
            bsel[h] = -(slope * d_sel)
            bwin[h] = -(slope * d_win)
            bcmp[h] = -(slope * d_cmp)
        j = lax.broadcasted_iota(jnp.int32, (n_blk, t), 0)
        c = lax.broadcasted_iota(jnp.int32, (n_blk, t), 1)
        efull[...] = jnp.where(c // SEL_BLOCK == j, 1.0, 0.0).astype(BF16)

    s0 = pl.program_id(1) * qb
    q = q_ref[...] * jnp.asarray(QK_SCALE, q_ref.dtype)
    ckv = (ca_ref[...] + pltpu.roll(cb_ref[...], n_seg - 1, axis=0)).astype(BF16)

    ccol = lax.broadcasted_iota(jnp.int32, (1, n_seg), 1)
    cmask = jnp.where((s0 + qi - (ccol * CMP_STRIDE + (CMP_LEN - 1)) >= 0) & (ccol < n_cmp_valid), 0.0, NEG_INF)
    kstart = pl.multiple_of(jnp.clip(s0 + qb - wk, 0, t - wk), qb)
    wd = (s0 - kstart) + qi - lax.broadcasted_iota(jnp.int32, (1, wk), 1)
    wmask = jnp.where((wd >= 0) & (wd < WINDOW), 0.0, NEG_INF)
    wblk = kw_ref[pl.ds(kstart, wk), :]

    gj = lax.broadcasted_iota(jnp.int32, (n_blk, n_seg), 0)
    gc = lax.broadcasted_iota(jnp.int32, (n_blk, n_seg), 1)
    gsum_t = jnp.where(gc // PER_SEL == gj, 1.0, 0.0).astype(F32)
    blk_t = lax.broadcasted_iota(jnp.int32, (n_blk, 1), 0)
    cur_t = (s0 + lax.broadcasted_iota(jnp.int32, (1, qb), 1)) // SEL_BLOCK
    forced_t = (blk_t == 0) | (blk_t == cur_t) | (blk_t == cur_t - 1)

    for g in range(KV_HEADS):
        ksl = slice(g * HEAD_DIM, (g + 1) * HEAD_DIM)
        vsl = slice(KV_W // 2 + g * HEAD_DIM, KV_W // 2 + (g + 1) * HEAD_DIM)
        qg = _stack_heads(q, g)
        qs_s[g] = qg
        p_c = _softmax_rows(_add_head_bias(_dot_nt(qg, ckv[:, ksl]), bcmp, g, cmask, qb))
        oc_s[g] = _dot(p_c.astype(BF16), ckv[:, vsl])
        imp_t = lax.dot_general(gsum_t, _sum_heads(p_c, qb), (((1,), (1,)), ((), ())),
                                preferred_element_type=F32, precision=lax.Precision.HIGHEST)
        score_t = jnp.where(forced_t, FORCE_BONUS, imp_t)
        score_t = jnp.where(blk_t <= cur_t, score_t, NEG_INF)
        selb_s[g] = _select_blocks_t(score_t).T.astype(BF16)
        p_w = _softmax_rows(_add_head_bias(_dot_nt(qg, wblk[:, ksl]), bwin, g, wmask, qb))
        ow_s[g] = _dot(p_w.astype(BF16), wblk[:, vsl])

    m_s[...] = jnp.full(m_s.shape, NEG_INF, F32)
    l_s[...] = jnp.zeros(l_s.shape, F32)
    acc_s[...] = jnp.zeros(acc_s.shape, F32)

    def body(c, carry):
        k0 = pl.multiple_of(c * kc, kc)
        kblk = ks_ref[pl.ds(k0, kc), :]
        expand = efull[:, pl.ds(k0, kc)]
        delta = jnp.zeros((1, 1), jnp.int32) + (s0 - k0)
        causal = (delta + qi - lax.broadcasted_iota(jnp.int32, (1, kc), 1)) >= 0
        delta_f = delta.astype(F32)
        for g in range(KV_HEADS):
            mask = jnp.where((_dot(selb_s[g], expand) > 0.5) & causal, 0.0, NEG_INF)
            s = _add_head_bias(_dot_nt(qs_s[g], kblk[:, g * HEAD_DIM:(g + 1) * HEAD_DIM]), bsel, g, mask, qb)
            off = _row_slopes(g, qb) * delta_f
            m_prev = m_s[g]
            m_new = jnp.maximum(m_prev, jnp.max(s, axis=-1, keepdims=True) - off)
            m_safe = jnp.where(m_new == NEG_INF, 0.0, m_new)
            alpha = jnp.exp(m_prev - m_safe)
            p = jnp.exp(s - (m_safe + off))
            l_s[g] = alpha * l_s[g] + jnp.sum(p, axis=-1, keepdims=True)
            v = kblk[:, KV_W // 2 + g * HEAD_DIM:KV_W // 2 + (g + 1) * HEAD_DIM]
            acc_s[g] = alpha * acc_s[g] + _dot(p.astype(BF16), v)
            m_s[g] = m_new
        return carry

    lax.fori_loop(0, (s0 + qb + kc - 1) // kc, body, 0)

    branches = [[oc_s[g], acc_s[g] / jnp.maximum(l_s[g], 1e-30), ow_s[g]] for g in range(KV_HEADS)]
    o_ref[...] = _gate_and_pack(ng_ref[...], branches, qb).astype(BF16)


def _nsa_prompt(q, ng, ca, cb, ksb, kwb):
    bsz, t, _ = q.shape
    qb = min(ATT_QB, t)
    kc = min(ATT_KC, t)
    wk = min(WINDOW + qb, t)
    n_seg = ca.shape[1]
    n_blk = t // SEL_BLOCK
    rows = GROUP * qb
    per_b = lambda r, w: pl.BlockSpec((None, r, w), lambda b, i: (b, 0, 0))
    blk = lambda w: pl.BlockSpec((None, qb, w), lambda b, i: (b, i, 0))
    return pl.pallas_call(
        functools.partial(_nsa_prompt_kernel, t=t, qb=qb, kc=kc, wk=wk),
        grid=(bsz, t // qb),
        in_specs=[blk(N_HEADS * HEAD_DIM), blk(LANES), per_b(n_seg, KV_W), per_b(n_seg, KV_W),
                  per_b(t, KV_W), per_b(t, KV_W)],
        out_specs=blk(N_HEADS * HEAD_DIM),
        out_shape=jax.ShapeDtypeStruct((bsz, t, N_HEADS * HEAD_DIM), BF16),
        scratch_shapes=[pltpu.VMEM((N_HEADS, qb, kc), F32), pltpu.VMEM((N_HEADS, qb, wk), F32),
                        pltpu.VMEM((N_HEADS, qb, n_seg), F32), pltpu.VMEM((n_blk, t), BF16),
                        pltpu.VMEM((KV_HEADS, rows, HEAD_DIM), BF16), pltpu.VMEM((KV_HEADS, qb, n_blk), BF16),
                        pltpu.VMEM((KV_HEADS, rows, HEAD_DIM), F32), pltpu.VMEM((KV_HEADS, rows, HEAD_DIM), F32),
                        pltpu.VMEM((KV_HEADS, rows, 1), F32), pltpu.VMEM((KV_HEADS, rows, 1), F32),
                        pltpu.VMEM((KV_HEADS, rows, HEAD_DIM), F32)],
        compiler_params=_cparams(("arbitrary", "arbitrary")),
        name="nsa_prompt",
    )(q, ng, ca, cb, ksb, kwb)


def _nsa_sample_select_kernel(q_ref, ca_ref, cb_ref, oc_ref, sel_ref, *, ts, past):
    n_seg = ca_ref.shape[-1]
    n_cmp_valid = n_seg - 1
    n_pblk = past // SEL_BLOCK
    qrow = past + lax.broadcasted_iota(jnp.int32, (ts, 1), 0)
    qpos = _tile_rows(qrow)
    q = q_ref[...]
    gsum = _block_sum_matrix(n_seg, n_pblk)
    blk = lax.broadcasted_iota(jnp.int32, (1, n_pblk), 1)
    cur = qrow // SEL_BLOCK
    forced = (blk == 0) | (blk == cur) | (blk == cur - 1)
    col = lax.broadcasted_iota(jnp.int32, (1, n_seg), 1)
    d = qpos - (col * CMP_STRIDE + (CMP_LEN - 1))
    valid = (d >= 0) & (col < n_cmp_valid)
    df = d.astype(F32)
    outs = []
    for g in range(KV_HEADS):
        qg = _stack_heads(q, g)
        ck_t = (ca_ref[0, g] + pltpu.roll(cb_ref[0, g], n_seg - 1, axis=1)).astype(BF16)
        cv_t = (ca_ref[1, g] + pltpu.roll(cb_ref[1, g], n_seg - 1, axis=1)).astype(BF16)
        s = _dot(qg, ck_t) * QK_SCALE - _row_slopes(g, ts) * df
        p_c = _masked_softmax_rows(s, valid)
        outs.append(_dot_nt(p_c.astype(BF16), cv_t))
        imp = jnp.dot(_sum_heads(p_c, ts), gsum, preferred_element_type=F32, precision=lax.Precision.HIGHEST)
        score = jnp.where(forced, FORCE_BONUS, imp)
        above = jnp.where(FORCE_BONUS > score, 1, 0)
        sel_ref[g * ts:(g + 1) * ts, :] = jnp.where(_select_blocks(score, above), 1.0, 0.0)
    for g in range(KV_HEADS):
        oc_ref[g * GROUP * ts:(g + 1) * GROUP * ts, :] = outs[g]


def _nsa_sample_select(q, ca, cb, past):
    bsz, ts, _ = q.shape
    n_seg = ca.shape[-1]
    n_pblk = past // SEL_BLOCK
    per_b = lambda r, w: pl.BlockSpec((None, r, w), lambda b: (b, 0, 0))
    seg_t = pl.BlockSpec((None, 2, KV_HEADS, HEAD_DIM, n_seg), lambda b: (b, 0, 0, 0, 0))
    return pl.pallas_call(
        functools.partial(_nsa_sample_select_kernel, ts=ts, past=past),
        grid=(bsz,),
        in_specs=[per_b(ts, N_HEADS * HEAD_DIM), seg_t, seg_t],
        out_specs=[per_b(N_HEADS * ts, HEAD_DIM), per_b(KV_HEADS * ts, n_pblk)],
        out_shape=[jax.ShapeDtypeStruct((bsz, N_HEADS * ts, HEAD_DIM), F32),
                   jax.ShapeDtypeStruct((bsz, KV_HEADS * ts, n_pblk), F32)],
        compiler_params=_cparams(("arbitrary",)),
        name="nsa_sample_select",
    )(q, ca, cb)


def _nsa_sample_attend_kernel(pt_ref, *refs, ts, past, pad):
    del pt_ref
    pages = refs[:PAGES_PER_STEP]
    (q_ref, ng_ref, oc_ref, sel_ref, tail_ref, wbuf_ref, wnew_ref, o_ref, m_s, l_s, acc_s) = refs[PAGES_PER_STEP:]
    j = pl.program_id(1)
    nj = pl.num_programs(1)
    rows = GROUP * ts
    kc = PAGES_PER_STEP * PAGE_SIZE
    bps = kc // SEL_BLOCK
    qrow = past + lax.broadcasted_iota(jnp.int32, (ts, 1), 0)
    qpos = _tile_rows(qrow)
    q = q_ref[...]

    @pl.when(j == 0)
    def _():
        m_s[...] = jnp.full(m_s.shape, NEG_INF, F32)
        l_s[...] = jnp.zeros(l_s.shape, F32)
        acc_s[...] = jnp.zeros(acc_s.shape, F32)

    k0 = j * kc
    d = qpos - (k0 + lax.broadcasted_iota(jnp.int32, (1, kc), 1))
    df = d.astype(F32)
    expand = _expand_matrix(bps, 0, 0, kc)
    for g in range(KV_HEADS):
        qg = _stack_heads(q, g)
        k_t = jnp.concatenate([p[0, g] for p in pages], axis=1).astype(BF16)
        v_t = jnp.concatenate([p[1, g] for p in pages], axis=1).astype(BF16)
        s = _dot(qg, k_t) * QK_SCALE - _row_slopes(g, ts) * df
        selk = _tile_rows(_dot(sel_ref[g * ts:(g + 1) * ts, :].astype(BF16), expand))
        sl = slice(g * rows, (g + 1) * rows)
        m, l, acc = _flash_step((m_s[sl, :], l_s[sl, :], acc_s[sl, :]), s, (d >= 0) & (selk > 0.5), v_t,
                                v_transposed=True)
        m_s[sl, :] = m
        l_s[sl, :] = l
        acc_s[sl, :] = acc

    @pl.when(j == nj - 1)
    def _():
        ng = ng_ref[...]
        tail = tail_ref[...]
        wnew = wnew_ref[...]
        wb = wbuf_ref.shape[-1]
        colp = lax.broadcasted_iota(jnp.int32, (1, pad), 1)
        d_new = qpos - (past + colp)
        ok_new = (d_new >= 0) & (colp < ts)
        d_old = qpos - (past - wb + lax.broadcasted_iota(jnp.int32, (1, wb), 1))
        ok_old = (d_old >= 0) & (d_old < WINDOW)
        branches = []
        for g in range(KV_HEADS):
            qg = _stack_heads(q, g)
            slope = _row_slopes(g, ts)
            ksl = slice(g * HEAD_DIM, (g + 1) * HEAD_DIM)
            vsl = slice(KV_W // 2 + g * HEAD_DIM, KV_W // 2 + (g + 1) * HEAD_DIM)
            sl = slice(g * rows, (g + 1) * rows)
            s = _dot_nt(qg, tail[:, ksl]) * QK_SCALE - slope * d_new.astype(F32)
            o_s = _flash_out(_flash_step((m_s[sl, :], l_s[sl, :], acc_s[sl, :]), s, ok_new, tail[:, vsl]))
            s = _dot(qg, wbuf_ref[0, g].astype(BF16)) * QK_SCALE - slope * d_old.astype(F32)
            carry = _flash_step(_flash_init(rows), s, ok_old, wbuf_ref[1, g].astype(BF16), v_transposed=True)
            s = _dot_nt(qg, wnew[:, ksl]) * QK_SCALE - slope * d_new.astype(F32)
            o_w = _flash_out(_flash_step(carry, s, ok_new & (d_new < WINDOW), wnew[:, vsl]))
            branches.append([oc_ref[sl, :], o_s, o_w])
        o_ref[...] = _gate_and_pack(ng, branches, ts).astype(BF16)


def _nsa_sample_attend(cache_t, pt_flat, layer, q, ng, oc, sel4, tail, wbuf_t, wnew, past):
    bsz, ts, _ = q.shape
    n_pages = past // PAGE_SIZE
    steps = n_pages // PAGES_PER_STEP
    pad = tail.shape[1]
    wb = wbuf_t.shape[-1]
    bps = PAGES_PER_STEP * PAGE_SIZE // SEL_BLOCK
    per_b = lambda r, w: pl.BlockSpec((None, r, w), lambda b, j, pt: (b, 0, 0))
    grid_spec = pltpu.PrefetchScalarGridSpec(
        num_scalar_prefetch=1,
        grid=(bsz, steps),
        in_specs=_page_specs(layer, n_pages, PAGES_PER_STEP) + [
            per_b(ts, N_HEADS * HEAD_DIM), per_b(ts, LANES), per_b(N_HEADS * ts, HEAD_DIM),
            pl.BlockSpec((None, None, KV_HEADS * ts, bps), lambda b, j, pt: (b, j, 0, 0)),
            per_b(pad, KV_W),
            pl.BlockSpec((None, None, 2, KV_HEADS, HEAD_DIM, wb), lambda b, j, pt: (layer, b, 0, 0, 0, 0)),
            per_b(pad, KV_W)],
        out_specs=per_b(ts, N_HEADS * HEAD_DIM),
        scratch_shapes=[pltpu.VMEM((N_HEADS * ts, 1), F32), pltpu.VMEM((N_HEADS * ts, 1), F32),
                        pltpu.VMEM((N_HEADS * ts, HEAD_DIM), F32)],
    )
    return pl.pallas_call(
        functools.partial(_nsa_sample_attend_kernel, ts=ts, past=past, pad=pad),
        grid_spec=grid_spec,
        out_shape=jax.ShapeDtypeStruct((bsz, ts, N_HEADS * HEAD_DIM), BF16),
        compiler_params=_cparams(("arbitrary", "arbitrary")),
        name="nsa_sample_attend",
    )(pt_flat, *([cache_t] * PAGES_PER_STEP), q, ng, oc, sel4, tail, wbuf_t, wnew)


def _layer_norm(x, g, b):
    mu = jnp.mean(x, axis=-1, keepdims=True)
    xc = x - mu
    var = jnp.mean(xc * xc, axis=-1, keepdims=True)
    return xc * lax.rsqrt(var + LN_EPS) * g + b


def _merge_kernel(x_ref, yr_ref, ya_ref, mg_ref, wa_ref, wb_ref, wo_ref, g_ref, b_ref, wr_ref, br_ref,
                  o_ref, ob_ref, comb_ref):
    a = _dot(yr_ref[...], wa_ref[...])
    b = _dot(ya_ref[...], wb_ref[...])
    merged = mg_ref[:, 0:D_MODEL] * a + mg_ref[:, D_MODEL:2 * D_MODEL] * b
    mix = _dot(merged.astype(BF16), wo_ref[...])
    x1 = _layer_norm(ALPHA * x_ref[...] + mix, g_ref[...], b_ref[...])
    o_ref[...] = x1
    x1b = x1.astype(BF16)
    ob_ref[...] = x1b
    comb_ref[...] = _route(_dot(x1b, wr_ref[...]) + br_ref[...])


def _route(logits):
    n = logits.shape[0]
    lane = lax.broadcasted_iota(jnp.int32, (n, LANES), 1)
    is_g = lane < N_GROUPS
    gl = jnp.where(is_g, logits, NEG_INF)
    gmax = jnp.max(gl, axis=-1, keepdims=True)
    grp = jnp.min(jnp.where(gl == gmax, lane, LANES), axis=-1, keepdims=True)
    gw = 1.0 / jnp.sum(jnp.exp(gl - gmax), axis=-1, keepdims=True)
    e_idx = lane - N_GROUPS
    in_grp = (e_idx >= grp * EXP_PER_GROUP) & (e_idx < (grp + 1) * EXP_PER_GROUP)
    el = jnp.where(in_grp, logits, NEG_INF)
    v1 = jnp.max(el, axis=-1, keepdims=True)
    i1 = jnp.min(jnp.where(el == v1, lane, LANES), axis=-1, keepdims=True)
    el2 = jnp.where(lane == i1, NEG_INF, el)
    v2 = jnp.max(el2, axis=-1, keepdims=True)
    i2 = jnp.min(jnp.where(el2 == v2, lane, LANES), axis=-1, keepdims=True)
    e2 = jnp.exp(v2 - v1)
    w1 = gw / (1.0 + e2)
    w2 = gw * e2 / (1.0 + e2)
    comb = jnp.where(lane == i1, w1, 0.0) + jnp.where(lane == i2, w2, 0.0)
    return pltpu.roll(comb, LANES - N_GROUPS, axis=1)


def _merge_out(x2d, yr, ya, mg, wa, wb, wo, g, b, wr, br):
    n = x2d.shape[0]
    tm = min(MERGE_TM, n)
    row = lambda w: pl.BlockSpec((tm, w), lambda i: (i, 0))
    sq = _full_spec((D_MODEL, D_MODEL))
    vec = _full_spec((1, D_MODEL))
    return pl.pallas_call(
        _merge_kernel,
        grid=(n // tm,),
        in_specs=[row(D_MODEL), row(D_RNN), row(N_HEADS * HEAD_DIM), row(2 * D_MODEL), sq, sq, sq, vec, vec,
                  _full_spec((D_MODEL, LANES)), _full_spec((1, LANES))],
        out_specs=[row(D_MODEL), row(D_MODEL), row(LANES)],
        out_shape=[jax.ShapeDtypeStruct((n, D_MODEL), F32), jax.ShapeDtypeStruct((n, D_MODEL), BF16),
                   jax.ShapeDtypeStruct((n, LANES), F32)],
        compiler_params=_cparams(("arbitrary",)),
        name="merge_out",
    )(x2d, yr, ya, mg, wa, wb, wo, g, b, wr, br)


def _moe_kernel(x_ref, xb_ref, comb_ref, w1_ref, w3_ref, w2_ref, g_ref, b_ref, o_ref, acc):
    e = pl.program_id(1)
    ne = pl.num_programs(1)

    @pl.when(e == 0)
    def _():
        acc[...] = jnp.zeros(acc.shape, F32)

    xb = xb_ref[...]
    lane = lax.broadcasted_iota(jnp.int32, comb_ref.shape, 1)
    cw = jnp.sum(jnp.where(lane == e, comb_ref[...], 0.0), axis=-1, keepdims=True)
    h1 = _dot(xb, w1_ref[...])
    h3 = _dot(xb, w3_ref[...])
    h = (h1 * _sigmoid(h1)) * h3 * cw
    acc[...] += _dot(h.astype(BF16), w2_ref[...])

    @pl.when(e == ne - 1)
    def _():
        o_ref[...] = _layer_norm(ALPHA * x_ref[...] + acc[...], g_ref[...], b_ref[...])


def _moe(x1, x1b, comb, w1, w3, w2, g, b):
    n = x1.shape[0]
    tm = min(MOE_TM, n)
    row = lambda w: pl.BlockSpec((tm, w), lambda i, e: (i, 0))
    vec = pl.BlockSpec((1, D_MODEL), lambda i, e: (0, 0))
    return pl.pallas_call(
        _moe_kernel,
        grid=(n // tm, N_EXPERTS),
        in_specs=[row(D_MODEL), row(D_MODEL), row(LANES),
                  pl.BlockSpec((None, D_MODEL, D_EXPERT), lambda i, e: (e, 0, 0)),
                  pl.BlockSpec((None, D_MODEL, D_EXPERT), lambda i, e: (e, 0, 0)),
                  pl.BlockSpec((None, D_EXPERT, D_MODEL), lambda i, e: (e, 0, 0)),
                  vec, vec],
        out_specs=row(D_MODEL),
        out_shape=jax.ShapeDtypeStruct((n, D_MODEL), F32),
        scratch_shapes=[pltpu.VMEM((tm, D_MODEL), F32)],
        compiler_params=_cparams(("arbitrary", "arbitrary")),
        name="moe",
    )(x1, x1b, comb, w1, w3, w2, g, b)


def _block_diag4(w):
    w4 = w.reshape(N_RNN_BLOCKS // 4, 4, RNN_BLOCK, RNN_BLOCK)
    eye = jnp.eye(4, dtype=w.dtype)
    return jnp.einsum("kaij,ab->kaibj", w4, eye).reshape(N_RNN_BLOCKS // 4, 256, 256).astype(BF16)


def _layer_params(l, w_in, conv_w, conv_b, lru_wa, lru_ba, lru_wx, lru_bx, lru_lambda, cmp_w,
                  w_br_a, w_br_b, w_out, ln1_g, ln1_b, ln2_g, ln2_b,
                  router_wg, router_bg, router_we, router_be, exp_w1, exp_w3, exp_w2):
    n_main = 3 * D_RNN + 3 * KV_W
    n_ng = 3 * N_HEADS
    w = w_in[l]
    vec = lambda a: a[l].reshape(1, -1)
    wr = jnp.concatenate([router_wg[l], router_we[l]], axis=1)
    br = jnp.concatenate([router_bg[l], router_be[l]])
    npad = LANES - wr.shape[1]
    return dict(
        wm=w[:, :n_main].astype(BF16),
        wng=jnp.pad(w[:, n_main:n_main + n_ng], ((0, 0), (0, LANES - n_ng))).astype(BF16),
        wmg=w[:, n_main + n_ng:].astype(BF16),
        cw=conv_w[l], cb=vec(conv_b),
        wa=_block_diag4(lru_wa[l]), ba=vec(lru_ba), wx=_block_diag4(lru_wx[l]), bx=vec(lru_bx),
        lam=vec(lru_lambda),
        cmpw=jnp.broadcast_to(cmp_w[l][:, :, None, :], (CMP_LEN, 2, KV_HEADS, HEAD_DIM)).reshape(CMP_LEN, KV_W),
        cmpwt=jnp.tile(cmp_w[l].reshape(2, CMP_STRIDE, 2, HEAD_DIM).transpose(0, 2, 3, 1), (1, 1, 1, LANES // CMP_STRIDE)),
        wbra=w_br_a[l].astype(BF16), wbrb=w_br_b[l].astype(BF16), wout=w_out[l].astype(BF16),
        ln1g=vec(ln1_g), ln1b=vec(ln1_b), ln2g=vec(ln2_g), ln2b=vec(ln2_b),
        wr=jnp.pad(wr, ((0, 0), (0, npad))).astype(BF16), br=jnp.pad(br, (0, npad)).reshape(1, LANES),
        w1=exp_w1[l].astype(BF16), w3=exp_w3[l].astype(BF16), w2=exp_w2[l].astype(BF16),
    )


def _mixer_tail(p, x2d, y_rnn, y_att, mg):
    n = x2d.shape[0]
    x1, x1b, comb = _merge_out(x2d, y_rnn.reshape(n, D_RNN), y_att.reshape(n, -1), mg,
                               p["wbra"], p["wbrb"], p["wout"], p["ln1g"], p["ln1b"], p["wr"], p["br"])
    return _moe(x1, x1b, comb, p["w1"], p["w3"], p["w2"], p["ln2g"], p["ln2b"])


def kernel(x_prompt, x_sample, cache_cmp, cache_sel, cache_win, state_conv, state_lru, page_table,
           w_in, conv_w, conv_b, lru_wa, lru_ba, lru_wx, lru_bx, lru_lambda, cmp_w,
           w_br_a, w_br_b, w_out, ln1_g, ln1_b, ln2_g, ln2_b,
           router_wg, router_bg, router_we, router_be, exp_w1, exp_w3, exp_w2):
    bp, tp, _ = x_prompt.shape
    bs, ts, _ = x_sample.shape
    n_pages = page_table.shape[1]
    past = n_pages * PAGE_SIZE
    assert ts <= SEL_BLOCK and n_pages % CMP_PAGES_PER_STEP == 0 and tp % SEL_BLOCK == 0
    depth = w_in.shape[0]
    rows_minor = lambda a: a.transpose(0, 1, 3, 4, 5, 2)
    ccmp = rows_minor(cache_cmp)
    csel = rows_minor(cache_sel)
    cwin = rows_minor(cache_win)
    pt_flat = page_table.reshape(-1).astype(jnp.int32)
    pad = LANES
    kv6 = lambda a, b, t: a.reshape(b, t, 2, KV_HEADS, HEAD_DIM)

    xp = x_prompt.reshape(bp * tp, D_MODEL)
    xs = x_sample.reshape(bs * ts, D_MODEL)
    outs = [[] for _ in range(10)]
    for l in range(depth):
        p = _layer_params(l, w_in, conv_w, conv_b, lru_wa, lru_ba, lru_wx, lru_bx, lru_lambda, cmp_w,
                          w_br_a, w_br_b, w_out, ln1_g, ln1_b, ln2_g, ln2_b,
                          router_wg, router_bg, router_we, router_be, exp_w1, exp_w3, exp_w2)
        lru = (p["cw"], p["cb"], p["wa"], p["ba"], p["wx"], p["bx"], p["lam"])

        xr, gr, q, kvc, kvs, kvw, kvsb, kvwb, ng, mg = _project_in(xp, p["wm"], p["wng"], p["wmg"])
        r3 = lambda a: a.reshape(bp, tp, -1)
        y_rnn, cbuf, ht = _rglru(r3(xr), r3(gr), jnp.zeros((bp, CONV_W - 1, D_RNN), F32),
                                 jnp.zeros((bp, 1, D_RNN), F32), *lru, pos0=0)
        ca, cb = _compress_rows(r3(kvc), p["cmpw"])
        y_att = _nsa_prompt(r3(q), r3(ng), ca, cb, r3(kvsb), r3(kvwb))
        xp = _mixer_tail(p, xp, y_rnn, y_att, mg)
        wlen = min(WINDOW, tp)
        outs[0].append(kv6(kvc, bp, tp))
        outs[2].append(kv6(kvs, bp, tp))
        outs[4].append(kv6(kvw, bp, tp)[:, tp - wlen:])
        outs[6].append(cbuf)
        outs[8].append(ht.reshape(bp, D_RNN))

        xr, gr, q, kvc, kvs, kvw, kvsb, kvwb, ng, mg = _project_in(xs, p["wm"], p["wng"], p["wmg"])
        r3 = lambda a: a.reshape(bs, ts, -1)
        y_rnn, cbuf, ht = _rglru(r3(xr), r3(gr), state_conv[l], state_lru[l].reshape(bs, 1, D_RNN),
                                 *lru, pos0=past)
        ca, cb = _compress_pages(ccmp, pt_flat, p["cmpwt"], l, bs, n_pages)
        oc, sel = _nsa_sample_select(r3(q), ca, cb, past)
        steps = n_pages // PAGES_PER_STEP
        sel4 = sel.reshape(bs, KV_HEADS * ts, steps, -1).transpose(0, 2, 1, 3)
        padrows = lambda a: jnp.pad(r3(a), ((0, 0), (0, pad - ts), (0, 0)))
        y_att = _nsa_sample_attend(csel, pt_flat, l, r3(q), r3(ng), oc, sel4, padrows(kvsb), cwin, padrows(kvwb),
                                   past)
        xs = _mixer_tail(p, xs, y_rnn, y_att, mg)
        outs[1].append(kv6(kvc, bs, ts))
        outs[3].append(kv6(kvs, bs, ts))
        kw_all = jnp.concatenate([cache_win[l], kv6(kvw, bs, ts)], axis=1)
        outs[5].append(kw_all[:, ts:])
        outs[7].append(cbuf)
        outs[9].append(ht.reshape(bs, D_RNN))

    st = [jnp.stack(o) for o in outs]
    return (xp.reshape(bp, tp, D_MODEL), xs.reshape(bs, ts, D_MODEL),
            st[0], st[1], st[2], st[3], st[4], st[5], st[6], st[7], st[8], st[9])
```

```python
import functools

import jax
import jax.numpy as jnp
import numpy as np
from jax import lax
from jax.experimental import pallas as pl
from jax.experimental.pallas import tpu as pltpu

F32 = jnp.float32
BF16 = jnp.bfloat16
NEG_INF = float("-inf")

D_MODEL = 1024
D_RNN = 1024
N_RNN_BLOCKS = 16
RNN_BLOCK = D_RNN // N_RNN_BLOCKS
CONV_W = 4
LRU_C = 8.0
N_HEADS = 16
HEAD_DIM = 64
KV_HEADS = 4
GROUP = N_HEADS // KV_HEADS
CMP_LEN = 32
CMP_STRIDE = 16
SEL_BLOCK = 64
N_SELECT = 16
WINDOW = 512
FORCE_BONUS = 1e4
PAGE_SIZE = 128
N_GROUPS = 4
EXP_PER_GROUP = 8
N_EXPERTS = N_GROUPS * EXP_PER_GROUP
D_EXPERT = 256
DEPTH = 2
ALPHA = (2.0 * DEPTH) ** 0.25
LN_EPS = 1e-5
KV_W = 2 * KV_HEADS * HEAD_DIM
QK_SCALE = HEAD_DIM ** -0.5
PER_SEL = SEL_BLOCK // CMP_STRIDE
SLOPES = [2.0 ** (-8.0 * (h + 1) / N_HEADS) for h in range(N_HEADS)]

LANES = 128
VMEM_LIMIT_BYTES = 56 * 1024 * 1024

PROJ_TM = 256
LRU_TC = 256
MERGE_TM = 512
MOE_TM = 1024
ATT_QB = 128
ATT_KC = 512
PAGES_PER_STEP = 32
CMP_PAGES_PER_STEP = 16


def _cparams(sem):
    return pltpu.CompilerParams(dimension_semantics=sem, vmem_limit_bytes=VMEM_LIMIT_BYTES)


def _full_spec(shape):
    nd = len(shape)
    return pl.BlockSpec(shape, lambda *_: (0,) * nd)


def _dot(a, b):
    return jnp.dot(a, b, preferred_element_type=F32)


def _dot_nt(a, b):
    return lax.dot_general(a, b, (((1,), (1,)), ((), ())), preferred_element_type=F32)


def _sigmoid(x):
    return 1.0 / (1.0 + jnp.exp(-x))


def _proj_kernel(x_ref, wm_ref, wng_ref, wmg_ref,
                 xr_ref, gr_ref, q_ref, kvc_ref, kvs_ref, kvw_ref, kvsb_ref, kvwb_ref, ng_ref, mg_ref):
    xb = x_ref[...].astype(BF16)
    xr_ref[...] = _dot(xb, wm_ref[:, 0:D_RNN])
    gr_ref[...] = _dot(xb, wm_ref[:, D_RNN:2 * D_RNN])
    q_ref[...] = _dot(xb, wm_ref[:, 2 * D_RNN:3 * D_RNN]).astype(BF16)
    o = 3 * D_RNN
    kvc_ref[...] = _dot(xb, wm_ref[:, o:o + KV_W])
    kvs = _dot(xb, wm_ref[:, o + KV_W:o + 2 * KV_W])
    kvs_ref[...] = kvs
    kvsb_ref[...] = kvs.astype(BF16)
    kvw = _dot(xb, wm_ref[:, o + 2 * KV_W:o + 3 * KV_W])
    kvw_ref[...] = kvw
    kvwb_ref[...] = kvw.astype(BF16)
    ng_ref[...] = _sigmoid(_dot(xb, wng_ref[...]))
    mg_ref[...] = _sigmoid(_dot(xb, wmg_ref[...]))


def _project_in(x2d, wm, wng, wmg):
    n = x2d.shape[0]
    tm = min(PROJ_TM, n)
    row = lambda w: pl.BlockSpec((tm, w), lambda i: (i, 0))
    outs = [(D_RNN, F32), (D_RNN, F32), (N_HEADS * HEAD_DIM, BF16), (KV_W, F32), (KV_W, F32), (KV_W, F32),
            (KV_W, BF16), (KV_W, BF16), (LANES, F32), (2 * D_MODEL, F32)]
    return pl.pallas_call(
        _proj_kernel,
        grid=(n // tm,),
        in_specs=[row(D_MODEL), _full_spec(wm.shape), _full_spec(wng.shape), _full_spec(wmg.shape)],
        out_specs=[row(w) for w, _ in outs],
        out_shape=[jax.ShapeDtypeStruct((n, w), dt) for w, dt in outs],
        compiler_params=_cparams(("arbitrary",)),
        name="project_in",
    )(x2d, wm, wng, wmg)


def _shift_rows(x, s, fill):
    row = lax.broadcasted_iota(jnp.int32, x.shape, 0)
    return jnp.where(row >= s, pltpu.roll(x, s, axis=0), fill)


def _gelu_tanh(x):
    c = np.float32(np.sqrt(2.0 / np.pi))
    return 0.5 * x * (1.0 + jnp.tanh(c * (x + np.float32(0.044715) * (x * x * x))))


def _rglru_kernel(xr_ref, gr_ref, cbuf_ref, h0_ref, cw_ref, cb_ref, wa_ref, ba_ref, wx_ref, bx_ref, lam_ref,
                  y_ref, nbuf_ref, ht_ref, xbuf, hcar, *, tc, pos0):
    c = pl.program_id(1)
    nc = pl.num_programs(1)

    @pl.when(c == 0)
    def _():
        xbuf[0:8, :] = jnp.zeros((8, D_RNN), F32)
        xbuf[8 - (CONV_W - 1):8, :] = cbuf_ref[...]
        hcar[...] = h0_ref[...]

    x = xr_ref[...]
    xbuf[8:8 + tc, :] = x
    conv = cb_ref[...] + cw_ref[CONV_W - 1:CONV_W, :] * x
    for j in range(1, CONV_W):
        conv = conv + cw_ref[CONV_W - 1 - j:CONV_W - j, :] * xbuf[8 - j:8 - j + tc, :]
    nbuf = xbuf[8 + tc - (CONV_W - 1):8 + tc, :]
    xbuf[0:8, :] = xbuf[tc:tc + 8, :]

    cvb = conv.astype(BF16)
    nb4 = D_RNN // 256
    za = jnp.concatenate([_dot(cvb[:, k * 256:(k + 1) * 256], wa_ref[k]) for k in range(nb4)], axis=1)
    zx = jnp.concatenate([_dot(cvb[:, k * 256:(k + 1) * 256], wx_ref[k]) for k in range(nb4)], axis=1)
    r = _sigmoid(za + ba_ref[...])
    i = _sigmoid(zx + bx_ref[...])
    nl = -lam_ref[...]
    softplus = jnp.maximum(nl, 0.0) + jnp.log(1.0 + jnp.exp(-jnp.abs(nl)))
    log_a = (-LRU_C) * r * softplus
    a = jnp.exp(log_a)
    pos = pos0 + c * tc + lax.broadcasted_iota(jnp.int32, (tc, 1), 0)
    mult = jnp.where(pos == 0, 1.0, jnp.sqrt(1.0 - jnp.exp(2.0 * log_a)))
    b = mult * i * conv

    s = 1
    while s < tc:
        a_sh = _shift_rows(a, s, 1.0)
        b_sh = _shift_rows(b, s, 0.0)
        b = a * b_sh + b
        a = a * a_sh
        s *= 2
    h = b + a * hcar[...]
    hcar[...] = h[tc - 1:tc, :]
    y_ref[...] = (_gelu_tanh(gr_ref[...]) * h).astype(BF16)

    @pl.when(c == nc - 1)
    def _():
        nbuf_ref[...] = nbuf
        ht_ref[...] = h[tc - 1:tc, :]


def _rglru(xr, gr, conv_buf, h0, cw, cb, wa_bd, ba, wx_bd, bx, lam, pos0):
    bsz, t, _ = xr.shape
    tc = min(LRU_TC, t)
    seq = pl.BlockSpec((None, tc, D_RNN), lambda b, c: (b, c, 0))
    per_b = lambda r: pl.BlockSpec((None, r, D_RNN), lambda b, c: (b, 0, 0))
    vec = pl.BlockSpec((1, D_RNN), lambda b, c: (0, 0))
    bd = pl.BlockSpec(wa_bd.shape, lambda b, c: (0, 0, 0))
    return pl.pallas_call(
        functools.partial(_rglru_kernel, tc=tc, pos0=pos0),
        grid=(bsz, t // tc),
        in_specs=[seq, seq, per_b(CONV_W - 1), per_b(1),
                  pl.BlockSpec((CONV_W, D_RNN), lambda b, c: (0, 0)), vec, bd, vec, bd, vec, vec],
        out_specs=[seq, per_b(CONV_W - 1), per_b(1)],
        out_shape=[jax.ShapeDtypeStruct((bsz, t, D_RNN), BF16),
                   jax.ShapeDtypeStruct((bsz, CONV_W - 1, D_RNN), F32),
                   jax.ShapeDtypeStruct((bsz, 1, D_RNN), F32)],
        scratch_shapes=[pltpu.VMEM((tc + 8, D_RNN), F32), pltpu.VMEM((1, D_RNN), F32)],
        compiler_params=_cparams(("arbitrary", "arbitrary")),
        name="rglru",
    )(xr, gr, conv_buf, h0, cw, cb, wa_bd, ba, wx_bd, bx, lam)


def _segment_sums(x, w1, w2):
    r = x.shape[0]
    x3 = x.reshape(r // CMP_STRIDE, CMP_STRIDE, KV_W)
    return jnp.sum(x3 * w1[None], axis=1), jnp.sum(x3 * w2[None], axis=1)


def _compress_kernel(kv_ref, w_ref, a_ref, b_ref):
    a, b = _segment_sums(kv_ref[...], w_ref[0:CMP_STRIDE, :], w_ref[CMP_STRIDE:CMP_LEN, :])
    a_ref[...] = a
    b_ref[...] = b


def _compress_rows(kv, w):
    bsz, t, _ = kv.shape
    tr = min(1024, t)
    ns = tr // CMP_STRIDE
    return pl.pallas_call(
        _compress_kernel,
        grid=(bsz, t // tr),
        in_specs=[pl.BlockSpec((None, tr, KV_W), lambda b, c: (b, c, 0)), _full_spec(w.shape)],
        out_specs=[pl.BlockSpec((None, ns, KV_W), lambda b, c: (b, c, 0))] * 2,
        out_shape=[jax.ShapeDtypeStruct((bsz, t // CMP_STRIDE, KV_W), F32)] * 2,
        compiler_params=_cparams(("arbitrary", "arbitrary")),
        name="compress_rows",
    )(kv, w)


def _split_bf16(y):
    hi = y.astype(BF16)
    return hi, (y - hi.astype(F32)).astype(BF16)


def _compress_pages_kernel(pt_ref, *refs):
    del pt_ref
    pages = refs[:CMP_PAGES_PER_STEP]
    w_ref, seg_ref, a_ref, b_ref = refs[CMP_PAGES_PER_STEP:]
    seg = seg_ref[...]
    for role, o_ref in enumerate((a_ref, b_ref)):
        rows = []
        for x in range(2):
            wt = w_ref[role, x]
            for g in range(KV_HEADS):
                rows.append(jnp.concatenate([p[x, g] * wt for p in pages], axis=1))
        hi, lo = _split_bf16(jnp.concatenate(rows, axis=0))
        z = _dot(hi, seg) + _dot(lo, seg)
        for x in range(2):
            for g in range(KV_HEADS):
                i = x * KV_HEADS + g
                o_ref[x, g] = z[i * HEAD_DIM:(i + 1) * HEAD_DIM, :]


def _page_specs(layer, n_pages, per_step):
    def spec(k):
        return pl.BlockSpec((None, None, 2, KV_HEADS, HEAD_DIM, PAGE_SIZE),
                            lambda b, j, pt: (layer, pt[b * n_pages + j * per_step + k], 0, 0, 0, 0))
    return [spec(k) for k in range(per_step)]


def _compress_pages(cache_t, pt_flat, wt, layer, bsz, n_pages):
    spp = PAGE_SIZE // CMP_STRIDE
    per_step = CMP_PAGES_PER_STEP
    assert per_step * spp == LANES
    steps = n_pages // per_step
    kdim = per_step * PAGE_SIZE
    lrow = lax.broadcasted_iota(jnp.int32, (kdim, LANES), 0)
    lcol = lax.broadcasted_iota(jnp.int32, (kdim, LANES), 1)
    seg = jnp.where(lrow // CMP_STRIDE == lcol, 1.0, 0.0).astype(BF16)
    out_block = pl.BlockSpec((None, 2, KV_HEADS, HEAD_DIM, LANES), lambda b, j, pt: (b, 0, 0, 0, j))
    grid_spec = pltpu.PrefetchScalarGridSpec(
        num_scalar_prefetch=1,
        grid=(bsz, steps),
        in_specs=_page_specs(layer, n_pages, per_step) + [
            pl.BlockSpec(wt.shape, lambda b, j, pt: (0, 0, 0, 0)), pl.BlockSpec(seg.shape, lambda b, j, pt: (0, 0))],
        out_specs=[out_block] * 2,
    )
    return pl.pallas_call(
        _compress_pages_kernel,
        grid_spec=grid_spec,
        out_shape=[jax.ShapeDtypeStruct((bsz, 2, KV_HEADS, HEAD_DIM, n_pages * spp), F32)] * 2,
        compiler_params=_cparams(("arbitrary", "arbitrary")),
        name="compress_pages",
    )(pt_flat, *([cache_t] * per_step), wt, seg)


def _row_slopes(g, n_q):
    r = lax.broadcasted_iota(jnp.int32, (GROUP * n_q, 1), 0) // n_q
    out = jnp.full((GROUP * n_q, 1), np.float32(SLOPES[g * GROUP + GROUP - 1]), F32)
    for k in range(GROUP - 2, -1, -1):
        out = jnp.where(r == k, np.float32(SLOPES[g * GROUP + k]), out)
    return out


def _stack_heads(q, g):
    return jnp.concatenate(
        [q[:, (g * GROUP + r) * HEAD_DIM:(g * GROUP + r + 1) * HEAD_DIM] for r in range(GROUP)], axis=0)


def _masked_softmax_rows(s, valid):
    s = jnp.where(valid, s, NEG_INF)
    m = jnp.max(s, axis=-1, keepdims=True)
    m = jnp.where(m == NEG_INF, 0.0, m)
    p = jnp.exp(s - m)
    return p / jnp.maximum(jnp.sum(p, axis=-1, keepdims=True), 1e-30)


def _flash_step(carry, s, valid, v, v_transposed=False):
    m, l, acc = carry
    s = jnp.where(valid, s, NEG_INF)
    m_new = jnp.maximum(m, jnp.max(s, axis=-1, keepdims=True))
    m_safe = jnp.where(m_new == NEG_INF, 0.0, m_new)
    alpha = jnp.exp(m - m_safe)
    p = jnp.exp(s - m_safe)
    l = alpha * l + jnp.sum(p, axis=-1, keepdims=True)
    pv = _dot_nt(p.astype(BF16), v) if v_transposed else _dot(p.astype(BF16), v)
    return m_new, l, alpha * acc + pv


def _flash_init(rows):
    return (jnp.full((rows, 1), NEG_INF, F32), jnp.zeros((rows, 1), F32), jnp.zeros((rows, HEAD_DIM), F32))


def _flash_out(carry):
    _, l, acc = carry
    return acc / jnp.maximum(l, 1e-30)


def _select_blocks(score, n_extra_above=None):
    n_blk = score.shape[1]
    lane = lax.broadcasted_iota(jnp.int32, score.shape, 1)
    cnt = jnp.zeros(score.shape, jnp.int32) if n_extra_above is None else n_extra_above
    for k in range(n_blk):
        col = score[:, k:k + 1]
        cnt = cnt + jnp.where(lane > k, jnp.where(col >= score, 1, 0), jnp.where(col > score, 1, 0))
    return cnt < N_SELECT


def _block_sum_matrix(n_cmp, n_blk):
    c = lax.broadcasted_iota(jnp.int32, (n_cmp, n_blk), 0)
    j = lax.broadcasted_iota(jnp.int32, (n_cmp, n_blk), 1)
    return jnp.where(c // PER_SEL == j, 1.0, 0.0).astype(F32)


def _expand_matrix(n_blk, blk0, k0, kc):
    j = lax.broadcasted_iota(jnp.int32, (n_blk, kc), 0) + blk0
    c = lax.broadcasted_iota(jnp.int32, (n_blk, kc), 1) + k0
    return jnp.where(c // SEL_BLOCK == j, 1.0, 0.0).astype(BF16)


def _sum_heads(p, n_q):
    out = p[0:n_q]
    for r in range(1, GROUP):
        out = out + p[r * n_q:(r + 1) * n_q]
    return out


def _tile_rows(x):
    return jnp.concatenate([x] * GROUP, axis=0)


def _gate_and_pack(ng, branches, n_q):
    pieces = []
    for g in range(KV_HEADS):
        for r in range(GROUP):
            idx = g * GROUP + r
            acc = None
            for br, o in enumerate(branches[g]):
                term = ng[:, br * N_HEADS + idx:br * N_HEADS + idx + 1] * o[r * n_q:(r + 1) * n_q]
                acc = term if acc is None else acc + term
            pieces.append(acc)
    return jnp.concatenate(pieces, axis=1)


def _select_blocks_t(score_t):
    n_blk = score_t.shape[0]
    slab = 8
    slabs = [score_t[i:i + slab] for i in range(0, n_blk, slab)]
    cnts = [jnp.zeros(s.shape, jnp.int32) for s in slabs]
    row = lax.broadcasted_iota(jnp.int32, slabs[0].shape, 0)
    for k in range(n_blk):
        pivot = score_t[k:k + 1, :]
        for i, s in enumerate(slabs):
            lo = i * slab
            if lo > k:
                beats = jnp.where(pivot >= s, 1, 0)
            elif lo + slab - 1 < k:
                beats = jnp.where(pivot > s, 1, 0)
            else:
                beats = jnp.where(row + lo > k, jnp.where(pivot >= s, 1, 0), jnp.where(pivot > s, 1, 0))
            cnts[i] = cnts[i] + beats
    cnt = jnp.concatenate(cnts, axis=0)
    return jnp.where(cnt < N_SELECT, 1.0, 0.0)


def _project_t_kernel(x_ref, wt_ref, qt_ref, vst_ref, vwt_ref, ngt_ref):
    xb = x_ref[...].astype(BF16)
    nq = N_HEADS * HEAD_DIM
    nv = KV_W // 2
    qt_ref[...] = (_dot_nt(wt_ref[0:nq, :], xb) * QK_SCALE).astype(BF16)
    vst_ref[...] = _dot_nt(wt_ref[nq:nq + nv, :], xb).astype(BF16)
    vwt_ref[...] = _dot_nt(wt_ref[nq + nv:nq + 2 * nv, :], xb).astype(BF16)
    ngt_ref[...] = _sigmoid(_dot_nt(wt_ref[nq + 2 * nv:nq + 2 * nv + LANES, :], xb))


def _project_t(x2d, wt):
    n = x2d.shape[0]
    tm = min(PROJ_TM, n)
    outs = [(N_HEADS * HEAD_DIM, BF16), (KV_W // 2, BF16), (KV_W // 2, BF16), (LANES, F32)]
    return pl.pallas_call(
        _project_t_kernel,
        grid=(n // tm,),
        in_specs=[pl.BlockSpec((tm, D_MODEL), lambda i: (i, 0)), _full_spec(wt.shape)],
        out_specs=[pl.BlockSpec((r, tm), lambda i: (0, i)) for r, _ in outs],
        out_shape=[jax.ShapeDtypeStruct((r, n), dt) for r, dt in outs],
        compiler_params=_cparams(("arbitrary",)),
        name="project_t",
    )(x2d, wt)


def _exp_cols(s):
    m = jnp.max(s, axis=0, keepdims=True)
    m = jnp.where(m == NEG_INF, 0.0, m)
    e = jnp.exp(s - m)
    return e, 1.0 / jnp.maximum(jnp.sum(e, axis=0, keepdims=True), 1e-30)


def _add_head_bias_t(s, table_ref, g, mask, n_q):
    return jnp.concatenate(
        [s[:, r * n_q:(r + 1) * n_q] + (table_ref[g * GROUP + r] + mask) for r in range(GROUP)], axis=1)


def _lane_slopes(g, n_q):
    r = lax.broadcasted_iota(jnp.int32, (1, GROUP * n_q), 1) // n_q
    out = jnp.full((1, GROUP * n_q), np.float32(SLOPES[g * GROUP + GROUP - 1]), F32)
    for k in range(GROUP - 2, -1, -1):
        out = jnp.where(r == k, np.float32(SLOPES[g * GROUP + k]), out)
    return out


def _nsa_prompt_t_kernel(qt_ref, ngt_ref, ca_ref, cb_ref, ks_ref, kw_ref, vst_ref, vwt_ref, o_ref,
                         bsel, bwin, bcmp, efull, sel_s, oc_s, ow_s, m_s, l_s, acc_s, *, t, qb, kc, wk):
    n_seg = ca_ref.shape[0]
    n_cmp_valid = (t - CMP_LEN) // CMP_STRIDE + 1
    n_blk = t // SEL_BLOCK
    nv = KV_W // 2
    ql = lax.broadcasted_iota(jnp.int32, (1, qb), 1)

    def krow(n):
        return lax.broadcasted_iota(jnp.int32, (n, 1), 0)

    @pl.when((pl.program_id(0) == 0) & (pl.program_id(1) == 0))
    def _():
        d_sel = (ql - krow(kc)).astype(F32)
        d_win = (ql - krow(wk)).astype(F32)
        d_cmp = (ql - (krow(n_seg) * CMP_STRIDE + (CMP_LEN - 1))).astype(F32)
        for h in range(N_HEADS):
            slope = np.float32(SLOPES[h])
            bsel[h] = -(slope * d_sel)
            bwin[h] = -(slope * d_win)
            bcmp[h] = -(slope * d_cmp)
        c = lax.broadcasted_iota(jnp.int32, (t, n_blk), 0)
        j = lax.broadcasted_iota(jnp.int32, (t, n_blk), 1)
        efull[...] = jnp.where(c // SEL_BLOCK == j, 1.0, 0.0).astype(BF16)

    s0 = pl.program_id(1) * qb

    def q_t(g):
        return jnp.concatenate(
            [qt_ref[(g * GROUP + r) * HEAD_DIM:(g * GROUP + r + 1) * HEAD_DIM, :] for r in range(GROUP)], axis=1)

    ckv = ca_ref[...] + pltpu.roll(cb_ref[...], n_seg - 1, axis=0)
    ck = ckv[:, 0:nv].astype(BF16)
    cv_t = ckv[:, nv:2 * nv].T.astype(BF16)

    cc = krow(n_seg)
    cmask = jnp.where((s0 + ql - (cc * CMP_STRIDE + (CMP_LEN - 1)) >= 0) & (cc < n_cmp_valid), 0.0, NEG_INF)
    kstart = pl.multiple_of(jnp.clip(s0 + qb - wk, 0, t - wk), qb)
    wd = (s0 - kstart) + ql - krow(wk)
    wmask = jnp.where((wd >= 0) & (wd < WINDOW), 0.0, NEG_INF)

    gj = lax.broadcasted_iota(jnp.int32, (n_blk, n_seg), 0)
    gc = lax.broadcasted_iota(jnp.int32, (n_blk, n_seg), 1)
    gsum_t = jnp.where(gc // PER_SEL == gj, 1.0, 0.0).astype(BF16)
    blk_t = krow(n_blk)
    cur_t = (s0 + ql) // SEL_BLOCK
    forced_t = (blk_t == 0) | (blk_t == cur_t) | (blk_t == cur_t - 1)

    for g in range(KV_HEADS):
        ksl = slice(g * HEAD_DIM, (g + 1) * HEAD_DIM)
        qg = q_t(g)
        e_c, r_c = _exp_cols(_add_head_bias_t(_dot(ck[:, ksl], qg), bcmp, g, cmask, qb))
        e_hi, e_lo = _split_bf16(e_c)
        oc_s[g] = _dot(cv_t[ksl, :], e_hi) * r_c
        imp4 = (_dot(gsum_t, e_hi) + _dot(gsum_t, e_lo)) * r_c
        imp_t = imp4[:, 0:qb]
        for r in range(1, GROUP):
            imp_t = imp_t + imp4[:, r * qb:(r + 1) * qb]
        score_t = jnp.where(forced_t, FORCE_BONUS, imp_t)
        score_t = jnp.where(blk_t <= cur_t, score_t, NEG_INF)
        sel_s[g] = _select_blocks_t(score_t).astype(BF16)
        e_w, r_w = _exp_cols(_add_head_bias_t(_dot(kw_ref[pl.ds(kstart, wk), ksl], qg), bwin, g, wmask, qb))
        ow_s[g] = _dot(vwt_ref[ksl, pl.ds(kstart, wk)], e_w.astype(BF16)) * r_w

    m_s[...] = jnp.full(m_s.shape, NEG_INF, F32)
    l_s[...] = jnp.zeros(l_s.shape, F32)
    acc_s[...] = jnp.zeros(acc_s.shape, F32)

    def body(c, carry):
        k0 = pl.multiple_of(c * kc, kc)
        expand = efull[pl.ds(k0, kc), :]
        delta = jnp.zeros((1, 1), jnp.int32) + (s0 - k0)
        causal = (delta + ql - krow(kc)) >= 0
        delta_f = delta.astype(F32)
        for g in range(KV_HEADS):
            ksl = slice(g * HEAD_DIM, (g + 1) * HEAD_DIM)
            mask = jnp.where((_dot(expand, sel_s[g]) > 0.5) & causal, 0.0, NEG_INF)
            s = _add_head_bias_t(_dot(ks_ref[pl.ds(k0, kc), ksl], q_t(g)), bsel, g, mask, qb)
            off = _lane_slopes(g, qb) * delta_f
            m_prev = m_s[g]
            m_new = jnp.maximum(m_prev, jnp.max(s, axis=0, keepdims=True) - off)
            m_safe = jnp.where(m_new == NEG_INF, 0.0, m_new)
            alpha = jnp.exp(m_prev - m_safe)
            p = jnp.exp(s - (m_safe + off))
            l_s[g] = alpha * l_s[g] + jnp.sum(p, axis=0, keepdims=True)
            acc_s[g] = alpha * acc_s[g] + _dot(vst_ref[ksl, pl.ds(k0, kc)], p.astype(BF16))
            m_s[g] = m_new
        return carry

    lax.fori_loop(0, (s0 + qb + kc - 1) // kc, body, 0)

    ngt = ngt_ref[...]
    pieces = []
    for g in range(KV_HEADS):
        o_s = acc_s[g] / jnp.maximum(l_s[g], 1e-30)
        o_c = oc_s[g]
        o_w = ow_s[g]
        for r in range(GROUP):
            h = g * GROUP + r
            sl = slice(r * qb, (r + 1) * qb)
            pieces.append(ngt[h:h + 1, :] * o_c[:, sl] + ngt[N_HEADS + h:N_HEADS + h + 1, :] * o_s[:, sl]
                          + ngt[2 * N_HEADS + h:2 * N_HEADS + h + 1, :] * o_w[:, sl])
    o_ref[...] = jnp.concatenate(pieces, axis=0).T.astype(BF16)


def _nsa_prompt_t(qt, ngt, ca, cb, ksb, kwb, vst, vwt, bsz, t):
    qb = min(ATT_QB, t)
    kc = min(ATT_KC, t)
    wk = min(WINDOW + qb, t)
    n_seg = ca.shape[1]
    n_blk = t // SEL_BLOCK
    nq = t // qb
    cols = GROUP * qb
    per_b = lambda r, w: pl.BlockSpec((None, r, w), lambda b, i: (b, 0, 0))
    tok = lambda r: pl.BlockSpec((r, qb), lambda b, i: (0, b * nq + i))
    seq = lambda r: pl.BlockSpec((r, t), lambda b, i: (0, b))
    return pl.pallas_call(
        functools.partial(_nsa_prompt_t_kernel, t=t, qb=qb, kc=kc, wk=wk),
        grid=(bsz, nq),
        in_specs=[tok(N_HEADS * HEAD_DIM), tok(LANES), per_b(n_seg, KV_W), per_b(n_seg, KV_W),
                  per_b(t, KV_W), per_b(t, KV_W), seq(KV_W // 2), seq(KV_W // 2)],
        out_specs=pl.BlockSpec((None, qb, N_HEADS * HEAD_DIM), lambda b, i: (b, i, 0)),
        out_shape=jax.ShapeDtypeStruct((bsz, t, N_HEADS * HEAD_DIM), BF16),
        scratch_shapes=[pltpu.VMEM((N_HEADS, kc, qb), F32), pltpu.VMEM((N_HEADS, wk, qb), F32),
                        pltpu.VMEM((N_HEADS, n_seg, qb), F32), pltpu.VMEM((t, n_blk), BF16),
                        pltpu.VMEM((KV_HEADS, n_blk, qb), BF16),
                        pltpu.VMEM((KV_HEADS, HEAD_DIM, cols), F32), pltpu.VMEM((KV_HEADS, HEAD_DIM, cols), F32),
                        pltpu.VMEM((KV_HEADS, 1, cols), F32), pltpu.VMEM((KV_HEADS, 1, cols), F32),
                        pltpu.VMEM((KV_HEADS, HEAD_DIM, cols), F32)],
        compiler_params=_cparams(("arbitrary", "arbitrary")),
        name="nsa_prompt",
    )(qt, ngt, ca, cb, ksb, kwb, vst, vwt)


def _nsa_sample_select_kernel(q_ref, ca_ref, cb_ref, oc_ref, sel_ref, *, ts, past):
    n_seg = ca_ref.shape[-1]
    n_cmp_valid = n_seg - 1
    n_pblk = past // SEL_BLOCK
    qrow = past + lax.broadcasted_iota(jnp.int32, (ts, 1), 0)
    qpos = _tile_rows(qrow)
    q = q_ref[...]
    gsum = _block_sum_matrix(n_seg, n_pblk)
    blk = lax.broadcasted_iota(jnp.int32, (1, n_pblk), 1)
    cur = qrow // SEL_BLOCK
    forced = (blk == 0) | (blk == cur) | (blk == cur - 1)
    col = lax.broadcasted_iota(jnp.int32, (1, n_seg), 1)
    d = qpos - (col * CMP_STRIDE + (CMP_LEN - 1))
    valid = (d >= 0) & (col < n_cmp_valid)
    df = d.astype(F32)
    outs = []
    for g in range(KV_HEADS):
        qg = _stack_heads(q, g)
        ck_t = (ca_ref[0, g] + pltpu.roll(cb_ref[0, g], n_seg - 1, axis=1)).astype(BF16)
        cv_t = (ca_ref[1, g] + pltpu.roll(cb_ref[1, g], n_seg - 1, axis=1)).astype(BF16)
        s = _dot(qg, ck_t) * QK_SCALE - _row_slopes(g, ts) * df
        p_c = _masked_softmax_rows(s, valid)
        outs.append(_dot_nt(p_c.astype(BF16), cv_t))
        imp = jnp.dot(_sum_heads(p_c, ts), gsum, preferred_element_type=F32, precision=lax.Precision.HIGHEST)
        score = jnp.where(forced, FORCE_BONUS, imp)
        above = jnp.where(FORCE_BONUS > score, 1, 0)
        sel_ref[g * ts:(g + 1) * ts, :] = jnp.where(_select_blocks(score, above), 1.0, 0.0)
    for g in range(KV_HEADS):
        oc_ref[g * GROUP * ts:(g + 1) * GROUP * ts, :] = outs[g]


def _nsa_sample_select(q, ca, cb, past):
    bsz, ts, _ = q.shape
    n_seg = ca.shape[-1]
    n_pblk = past // SEL_BLOCK
    per_b = lambda r, w: pl.BlockSpec((None, r, w), lambda b: (b, 0, 0))
    seg_t = pl.BlockSpec((None, 2, KV_HEADS, HEAD_DIM, n_seg), lambda b: (b, 0, 0, 0, 0))
    return pl.pallas_call(
        functools.partial(_nsa_sample_select_kernel, ts=ts, past=past),
        grid=(bsz,),
        in_specs=[per_b(ts, N_HEADS * HEAD_DIM), seg_t, seg_t],
        out_specs=[per_b(N_HEADS * ts, HEAD_DIM), per_b(KV_HEADS * ts, n_pblk)],
        out_shape=[jax.ShapeDtypeStruct((bsz, N_HEADS * ts, HEAD_DIM), F32),
                   jax.ShapeDtypeStruct((bsz, KV_HEADS * ts, n_pblk), F32)],
        compiler_params=_cparams(("arbitrary",)),
        name="nsa_sample_select",
    )(q, ca, cb)


def _nsa_sample_attend_kernel(pt_ref, *refs, ts, past, pad, n_pg):
    del pt_ref
    pages = refs[:n_pg]
    (q_ref, ng_ref, oc_ref, sel_ref, tail_ref, wbuf_ref, wnew_ref, o_ref, m_s, l_s, acc_s) = refs[n_pg:]
    j = pl.program_id(1)
    nj = pl.num_programs(1)
    rows = GROUP * ts
    kc = n_pg * PAGE_SIZE
    bps = kc // SEL_BLOCK
    qrow = past + lax.broadcasted_iota(jnp.int32, (ts, 1), 0)
    qpos = _tile_rows(qrow)
    q = q_ref[...]

    @pl.when(j == 0)
    def _():
        m_s[...] = jnp.full(m_s.shape, NEG_INF, F32)
        l_s[...] = jnp.zeros(l_s.shape, F32)
        acc_s[...] = jnp.zeros(acc_s.shape, F32)

    k0 = j * kc
    d = qpos - (k0 + lax.broadcasted_iota(jnp.int32, (1, kc), 1))
    df = d.astype(F32)
    expand = _expand_matrix(bps, 0, 0, kc)
    state = [(m_s[g], l_s[g], acc_s[g]) for g in range(KV_HEADS)]
    new_state = []
    for g in range(KV_HEADS):
        qg = _stack_heads(q, g)
        k_t = jnp.concatenate([p[0, g] for p in pages], axis=1).astype(BF16)
        v_t = jnp.concatenate([p[1, g] for p in pages], axis=1).astype(BF16)
        s = _dot(qg, k_t) * QK_SCALE - _row_slopes(g, ts) * df
        selk = _tile_rows(_dot(sel_ref[g * ts:(g + 1) * ts, :].astype(BF16), expand))
        new_state.append(_flash_step(state[g], s, (d >= 0) & (selk > 0.5), v_t, v_transposed=True))
    for g in range(KV_HEADS):
        m_s[g], l_s[g], acc_s[g] = new_state[g]

    @pl.when(j == nj - 1)
    def _():
        ng = ng_ref[...]
        tail = tail_ref[...]
        wnew = wnew_ref[...]
        wb = wbuf_ref.shape[-1]
        colp = lax.broadcasted_iota(jnp.int32, (1, pad), 1)
        d_new = qpos - (past + colp)
        ok_new = (d_new >= 0) & (colp < ts)
        d_old = qpos - (past - wb + lax.broadcasted_iota(jnp.int32, (1, wb), 1))
        ok_old = (d_old >= 0) & (d_old < WINDOW)
        branches = []
        for g in range(KV_HEADS):
            qg = _stack_heads(q, g)
            slope = _row_slopes(g, ts)
            ksl = slice(g * HEAD_DIM, (g + 1) * HEAD_DIM)
            vsl = slice(KV_W // 2 + g * HEAD_DIM, KV_W // 2 + (g + 1) * HEAD_DIM)
            sl = slice(g * rows, (g + 1) * rows)
            s = _dot_nt(qg, tail[:, ksl]) * QK_SCALE - slope * d_new.astype(F32)
            o_s = _flash_out(_flash_step((m_s[g], l_s[g], acc_s[g]), s, ok_new, tail[:, vsl]))
            s = _dot(qg, wbuf_ref[0, g].astype(BF16)) * QK_SCALE - slope * d_old.astype(F32)
            carry = _flash_step(_flash_init(rows), s, ok_old, wbuf_ref[1, g].astype(BF16), v_transposed=True)
            s = _dot_nt(qg, wnew[:, ksl]) * QK_SCALE - slope * d_new.astype(F32)
            o_w = _flash_out(_flash_step(carry, s, ok_new & (d_new < WINDOW), wnew[:, vsl]))
            branches.append([oc_ref[sl, :], o_s, o_w])
        o_ref[...] = _gate_and_pack(ng, branches, ts).astype(BF16)


def _nsa_sample_attend(cache_t, pt_flat, layer, q, ng, oc, sel, tail, wbuf_t, wnew, past):
    bsz, ts, _ = q.shape
    n_pages = past // PAGE_SIZE
    n_pg = min(PAGES_PER_STEP, n_pages)
    steps = n_pages // n_pg
    pad = tail.shape[1]
    wb = wbuf_t.shape[-1]
    bps = n_pg * PAGE_SIZE // SEL_BLOCK
    sel4 = sel.reshape(bsz, KV_HEADS * ts, steps, bps).transpose(0, 2, 1, 3)
    per_b = lambda r, w: pl.BlockSpec((None, r, w), lambda b, j, pt: (b, 0, 0))
    grid_spec = pltpu.PrefetchScalarGridSpec(
        num_scalar_prefetch=1,
        grid=(bsz, steps),
        in_specs=_page_specs(layer, n_pages, n_pg) + [
            per_b(ts, N_HEADS * HEAD_DIM), per_b(ts, LANES), per_b(N_HEADS * ts, HEAD_DIM),
            pl.BlockSpec((None, None, KV_HEADS * ts, bps), lambda b, j, pt: (b, j, 0, 0)),
            per_b(pad, KV_W),
            pl.BlockSpec((None, None, 2, KV_HEADS, HEAD_DIM, wb), lambda b, j, pt: (layer, b, 0, 0, 0, 0)),
            per_b(pad, KV_W)],
        out_specs=per_b(ts, N_HEADS * HEAD_DIM),
        scratch_shapes=[pltpu.VMEM((KV_HEADS, GROUP * ts, 1), F32), pltpu.VMEM((KV_HEADS, GROUP * ts, 1), F32),
                        pltpu.VMEM((KV_HEADS, GROUP * ts, HEAD_DIM), F32)],
    )
    return pl.pallas_call(
        functools.partial(_nsa_sample_attend_kernel, ts=ts, past=past, pad=pad, n_pg=n_pg),
        grid_spec=grid_spec,
        out_shape=jax.ShapeDtypeStruct((bsz, ts, N_HEADS * HEAD_DIM), BF16),
        compiler_params=_cparams(("arbitrary", "arbitrary")),
        name="nsa_sample_attend",
    )(pt_flat, *([cache_t] * n_pg), q, ng, oc, sel4, tail, wbuf_t, wnew)


def _layer_norm(x, g, b):
    mu = jnp.mean(x, axis=-1, keepdims=True)
    xc = x - mu
    var = jnp.mean(xc * xc, axis=-1, keepdims=True)
    return xc * lax.rsqrt(var + LN_EPS) * g + b


def _merge_kernel(x_ref, yr_ref, ya_ref, mg_ref, wa_ref, wb_ref, wo_ref, g_ref, b_ref, wr_ref, br_ref,
                  o_ref, ob_ref, comb_ref):
    a = _dot(yr_ref[...], wa_ref[...])
    b = _dot(ya_ref[...], wb_ref[...])
    merged = mg_ref[:, 0:D_MODEL] * a + mg_ref[:, D_MODEL:2 * D_MODEL] * b
    mix = _dot(merged.astype(BF16), wo_ref[...])
    x1 = _layer_norm(ALPHA * x_ref[...] + mix, g_ref[...], b_ref[...])
    o_ref[...] = x1
    x1b = x1.astype(BF16)
    ob_ref[...] = x1b
    comb_ref[...] = _route(_dot(x1b, wr_ref[...]) + br_ref[...])


def _route(logits):
    n = logits.shape[0]
    lane = lax.broadcasted_iota(jnp.int32, (n, LANES), 1)
    is_g = lane < N_GROUPS
    gl = jnp.where(is_g, logits, NEG_INF)
    gmax = jnp.max(gl, axis=-1, keepdims=True)
    grp = jnp.min(jnp.where(gl == gmax, lane, LANES), axis=-1, keepdims=True)
    gw = 1.0 / jnp.sum(jnp.exp(gl - gmax), axis=-1, keepdims=True)
    e_idx = lane - N_GROUPS
    in_grp = (e_idx >= grp * EXP_PER_GROUP) & (e_idx < (grp + 1) * EXP_PER_GROUP)
    el = jnp.where(in_grp, logits, NEG_INF)
    v1 = jnp.max(el, axis=-1, keepdims=True)
    i1 = jnp.min(jnp.where(el == v1, lane, LANES), axis=-1, keepdims=True)
    el2 = jnp.where(lane == i1, NEG_INF, el)
    v2 = jnp.max(el2, axis=-1, keepdims=True)
    i2 = jnp.min(jnp.where(el2 == v2, lane, LANES), axis=-1, keepdims=True)
    e2 = jnp.exp(v2 - v1)
    w1 = gw / (1.0 + e2)
    w2 = gw * e2 / (1.0 + e2)
    comb = jnp.where(lane == i1, w1, 0.0) + jnp.where(lane == i2, w2, 0.0)
    return pltpu.roll(comb, LANES - N_GROUPS, axis=1)


def _merge_out(x2d, yr, ya, mg, wa, wb, wo, g, b, wr, br):
    n = x2d.shape[0]
    tm = min(MERGE_TM, n)
    row = lambda w: pl.BlockSpec((tm, w), lambda i: (i, 0))
    sq = _full_spec((D_MODEL, D_MODEL))
    vec = _full_spec((1, D_MODEL))
    return pl.pallas_call(
        _merge_kernel,
        grid=(n // tm,),
        in_specs=[row(D_MODEL), row(D_RNN), row(N_HEADS * HEAD_DIM), row(2 * D_MODEL), sq, sq, sq, vec, vec,
                  _full_spec((D_MODEL, LANES)), _full_spec((1, LANES))],
        out_specs=[row(D_MODEL), row(D_MODEL), row(LANES)],
        out_shape=[jax.ShapeDtypeStruct((n, D_MODEL), F32), jax.ShapeDtypeStruct((n, D_MODEL), BF16),
                   jax.ShapeDtypeStruct((n, LANES), F32)],
        compiler_params=_cparams(("arbitrary",)),
        name="merge_out",
    )(x2d, yr, ya, mg, wa, wb, wo, g, b, wr, br)


def _moe_kernel(x_ref, xb_ref, comb_ref, w1_ref, w3_ref, w2_ref, g_ref, b_ref, o_ref, acc):
    e = pl.program_id(1)
    ne = pl.num_programs(1)

    @pl.when(e == 0)
    def _():
        acc[...] = jnp.zeros(acc.shape, F32)

    xb = xb_ref[...]
    lane = lax.broadcasted_iota(jnp.int32, comb_ref.shape, 1)
    cw = jnp.sum(jnp.where(lane == e, comb_ref[...], 0.0), axis=-1, keepdims=True)
    h1 = _dot(xb, w1_ref[...])
    h3 = _dot(xb, w3_ref[...])
    h = (h1 * _sigmoid(h1)) * h3 * cw
    acc[...] += _dot(h.astype(BF16), w2_ref[...])

    @pl.when(e == ne - 1)
    def _():
        o_ref[...] = _layer_norm(ALPHA * x_ref[...] + acc[...], g_ref[...], b_ref[...])


def _moe(x1, x1b, comb, w1, w3, w2, g, b):
    n = x1.shape[0]
    tm = min(MOE_TM, n)
    row = lambda w: pl.BlockSpec((tm, w), lambda i, e: (i, 0))
    vec = pl.BlockSpec((1, D_MODEL), lambda i, e: (0, 0))
    return pl.pallas_call(
        _moe_kernel,
        grid=(n // tm, N_EXPERTS),
        in_specs=[row(D_MODEL), row(D_MODEL), row(LANES),
                  pl.BlockSpec((None, D_MODEL, D_EXPERT), lambda i, e: (e, 0, 0)),
                  pl.BlockSpec((None, D_MODEL, D_EXPERT), lambda i, e: (e, 0, 0)),
                  pl.BlockSpec((None, D_EXPERT, D_MODEL), lambda i, e: (e, 0, 0)),
                  vec, vec],
        out_specs=row(D_MODEL),
        out_shape=jax.ShapeDtypeStruct((n, D_MODEL), F32),
        scratch_shapes=[pltpu.VMEM((tm, D_MODEL), F32)],
        compiler_params=_cparams(("arbitrary", "arbitrary")),
        name="moe",
    )(x1, x1b, comb, w1, w3, w2, g, b)


def _block_diag4(w):
    w4 = w.reshape(N_RNN_BLOCKS // 4, 4, RNN_BLOCK, RNN_BLOCK)
    eye = jnp.eye(4, dtype=w.dtype)
    return jnp.einsum("kaij,ab->kaibj", w4, eye).reshape(N_RNN_BLOCKS // 4, 256, 256).astype(BF16)


def _layer_params(l, w_in, conv_w, conv_b, lru_wa, lru_ba, lru_wx, lru_bx, lru_lambda, cmp_w,
                  w_br_a, w_br_b, w_out, ln1_g, ln1_b, ln2_g, ln2_b,
                  router_wg, router_bg, router_we, router_be, exp_w1, exp_w3, exp_w2):
    n_main = 3 * D_RNN + 3 * KV_W
    n_ng = 3 * N_HEADS
    w = w_in[l]
    vec = lambda a: a[l].reshape(1, -1)
    wr = jnp.concatenate([router_wg[l], router_we[l]], axis=1)
    br = jnp.concatenate([router_bg[l], router_be[l]])
    npad = LANES - wr.shape[1]
    return dict(
        wm=w[:, :n_main].astype(BF16),
        wng=jnp.pad(w[:, n_main:n_main + n_ng], ((0, 0), (0, LANES - n_ng))).astype(BF16),
        wmg=w[:, n_main + n_ng:].astype(BF16),
        wt=jnp.concatenate([w[:, 2 * D_RNN:3 * D_RNN], w[:, 3 * D_RNN + KV_W + KV_W // 2:3 * D_RNN + 2 * KV_W],
                            w[:, 3 * D_RNN + 2 * KV_W + KV_W // 2:n_main],
                            jnp.pad(w[:, n_main:n_main + n_ng], ((0, 0), (0, LANES - n_ng)))], axis=1).T.astype(BF16),
        cw=conv_w[l], cb=vec(conv_b),
        wa=_block_diag4(lru_wa[l]), ba=vec(lru_ba), wx=_block_diag4(lru_wx[l]), bx=vec(lru_bx),
        lam=vec(lru_lambda),
        cmpw=jnp.broadcast_to(cmp_w[l][:, :, None, :], (CMP_LEN, 2, KV_HEADS, HEAD_DIM)).reshape(CMP_LEN, KV_W),
        cmpwt=jnp.tile(cmp_w[l].reshape(2, CMP_STRIDE, 2, HEAD_DIM).transpose(0, 2, 3, 1), (1, 1, 1, LANES // CMP_STRIDE)),
        wbra=w_br_a[l].astype(BF16), wbrb=w_br_b[l].astype(BF16), wout=w_out[l].astype(BF16),
        ln1g=vec(ln1_g), ln1b=vec(ln1_b), ln2g=vec(ln2_g), ln2b=vec(ln2_b),
        wr=jnp.pad(wr, ((0, 0), (0, npad))).astype(BF16), br=jnp.pad(br, (0, npad)).reshape(1, LANES),
        w1=exp_w1[l].astype(BF16), w3=exp_w3[l].astype(BF16), w2=exp_w2[l].astype(BF16),
    )


def _mixer_tail(p, x2d, y_rnn, y_att, mg):
    n = x2d.shape[0]
    x1, x1b, comb = _merge_out(x2d, y_rnn.reshape(n, D_RNN), y_att.reshape(n, -1), mg,
                               p["wbra"], p["wbrb"], p["wout"], p["ln1g"], p["ln1b"], p["wr"], p["br"])
    return _moe(x1, x1b, comb, p["w1"], p["w3"], p["w2"], p["ln2g"], p["ln2b"])


def kernel(x_prompt, x_sample, cache_cmp, cache_sel, cache_win, state_conv, state_lru, page_table,
           w_in, conv_w, conv_b, lru_wa, lru_ba, lru_wx, lru_bx, lru_lambda, cmp_w,
           w_br_a, w_br_b, w_out, ln1_g, ln1_b, ln2_g, ln2_b,
           router_wg, router_bg, router_we, router_be, exp_w1, exp_w3, exp_w2):
    bp, tp, _ = x_prompt.shape
    bs, ts, _ = x_sample.shape
    n_pages = page_table.shape[1]
    past = n_pages * PAGE_SIZE
    assert ts <= SEL_BLOCK and n_pages % CMP_PAGES_PER_STEP == 0 and tp % SEL_BLOCK == 0
    depth = w_in.shape[0]
    rows_minor = lambda a: a.transpose(0, 1, 3, 4, 5, 2)
    ccmp = rows_minor(cache_cmp)
    csel = rows_minor(cache_sel)
    cwin = rows_minor(cache_win)
    pt_flat = page_table.reshape(-1).astype(jnp.int32)
    pad = LANES
    kv6 = lambda a, b, t: a.reshape(b, t, 2, KV_HEADS, HEAD_DIM)

    xp = x_prompt.reshape(bp * tp, D_MODEL)
    xs = x_sample.reshape(bs * ts, D_MODEL)
    outs = [[] for _ in range(10)]
    for l in range(depth):
        p = _layer_params(l, w_in, conv_w, conv_b, lru_wa, lru_ba, lru_wx, lru_bx, lru_lambda, cmp_w,
                          w_br_a, w_br_b, w_out, ln1_g, ln1_b, ln2_g, ln2_b,
                          router_wg, router_bg, router_we, router_be, exp_w1, exp_w3, exp_w2)
        lru = (p["cw"], p["cb"], p["wa"], p["ba"], p["wx"], p["bx"], p["lam"])

        xr, gr, q, kvc, kvs, kvw, kvsb, kvwb, ng, mg = _project_in(xp, p["wm"], p["wng"], p["wmg"])
        r3 = lambda a: a.reshape(bp, tp, -1)
        y_rnn, cbuf, ht = _rglru(r3(xr), r3(gr), jnp.zeros((bp, CONV_W - 1, D_RNN), F32),
                                 jnp.zeros((bp, 1, D_RNN), F32), *lru, pos0=0)
        ca, cb = _compress_rows(r3(kvc), p["cmpw"])
        qt, vst, vwt, ngt = _project_t(xp, p["wt"])
        y_att = _nsa_prompt_t(qt, ngt, ca, cb, r3(kvsb), r3(kvwb), vst, vwt, bp, tp)
        xp = _mixer_tail(p, xp, y_rnn, y_att, mg)
        wlen = min(WINDOW, tp)
        outs[0].append(kv6(kvc, bp, tp))
        outs[2].append(kv6(kvs, bp, tp))
        outs[4].append(kv6(kvw, bp, tp)[:, tp - wlen:])
        outs[6].append(cbuf)
        outs[8].append(ht.reshape(bp, D_RNN))

        xr, gr, q, kvc, kvs, kvw, kvsb, kvwb, ng, mg = _project_in(xs, p["wm"], p["wng"], p["wmg"])
        r3 = lambda a: a.reshape(bs, ts, -1)
        y_rnn, cbuf, ht = _rglru(r3(xr), r3(gr), state_conv[l], state_lru[l].reshape(bs, 1, D_RNN),
                                 *lru, pos0=past)
        ca, cb = _compress_pages(ccmp, pt_flat, p["cmpwt"], l, bs, n_pages)
        oc, sel = _nsa_sample_select(r3(q), ca, cb, past)
        padrows = lambda a: jnp.pad(r3(a), ((0, 0), (0, pad - ts), (0, 0)))
        y_att = _nsa_sample_attend(csel, pt_flat, l, r3(q), r3(ng), oc, sel, padrows(kvsb), cwin, padrows(kvwb),
                                   past)
        xs = _mixer_tail(p, xs, y_rnn, y_att, mg)
        outs[1].append(kv6(kvc, bs, ts))
        outs[3].append(kv6(kvs, bs, ts))
        kw_all = jnp.concatenate([cache_win[l], kv6(kvw, bs, ts)], axis=1)
        outs[5].append(kw_all[:, ts:])
        outs[7].append(cbuf)
        outs[9].append(ht.reshape(bs, D_RNN))

    st = [jnp.stack(o) for o in outs]
    return (xp.reshape(bp, tp, D_MODEL), xs.reshape(bs, ts, D_MODEL),
            st[0], st[1], st[2], st[3], st[4], st[5], st[6], st[7], st[8], st[9])
```

```python
import functools

import jax
import jax.numpy as jnp
import numpy as np
from jax import lax
from jax.experimental import pallas as pl
from jax.experimental.pallas import tpu as pltpu

F32 = jnp.float32
BF16 = jnp.bfloat16
NEG_INF = float("-inf")
MASK_NEG = -1e30

D_MODEL = 1024
D_RNN = 1024
N_RNN_BLOCKS = 16
RNN_BLOCK = D_RNN // N_RNN_BLOCKS
CONV_W = 4
LRU_C = 8.0
N_HEADS = 16
HEAD_DIM = 64
KV_HEADS = 4
GROUP = N_HEADS // KV_HEADS
CMP_LEN = 32
CMP_STRIDE = 16
SEL_BLOCK = 64
N_SELECT = 16
WINDOW = 512
FORCE_BONUS = 1e4
PAGE_SIZE = 128
N_GROUPS = 4
EXP_PER_GROUP = 8
N_EXPERTS = N_GROUPS * EXP_PER_GROUP
D_EXPERT = 256
DEPTH = 2
ALPHA = (2.0 * DEPTH) ** 0.25
LN_EPS = 1e-5
KV_W = 2 * KV_HEADS * HEAD_DIM
QK_SCALE = HEAD_DIM ** -0.5
PER_SEL = SEL_BLOCK // CMP_STRIDE
SLOPES = [2.0 ** (-8.0 * (h + 1) / N_HEADS) for h in range(N_HEADS)]

LANES = 128
VMEM_LIMIT_BYTES = 56 * 1024 * 1024

PROJ_TM = 256
LRU_TC = 256
MERGE_TM = 512
MOE_TM = 1024
ATT_QB = 128
ATT_KC = 512
AUX_BLOCK_ROWS = 16
PAGES_PER_STEP = 32
CMP_PAGES_PER_STEP = 16


def _cparams(sem):
    return pltpu.CompilerParams(dimension_semantics=sem, vmem_limit_bytes=VMEM_LIMIT_BYTES)


def _full_spec(shape):
    nd = len(shape)
    return pl.BlockSpec(shape, lambda *_: (0,) * nd)


def _dot(a, b):
    return jnp.dot(a, b, preferred_element_type=F32)


def _dot_nt(a, b):
    return lax.dot_general(a, b, (((1,), (1,)), ((), ())), preferred_element_type=F32)


def _sigmoid(x):
    return 1.0 / (1.0 + jnp.exp(-x))


def _proj_kernel(x_ref, wm_ref, wng_ref, wmg_ref,
                 xr_ref, gr_ref, q_ref, kvc_ref, kvs_ref, kvw_ref, kvsb_ref, kvwb_ref, ng_ref, mg_ref):
    xb = x_ref[...].astype(BF16)
    xr_ref[...] = _dot(xb, wm_ref[:, 0:D_RNN])
    gr_ref[...] = _dot(xb, wm_ref[:, D_RNN:2 * D_RNN])
    q_ref[...] = _dot(xb, wm_ref[:, 2 * D_RNN:3 * D_RNN]).astype(BF16)
    o = 3 * D_RNN
    kvc_ref[...] = _dot(xb, wm_ref[:, o:o + KV_W])
    kvs = _dot(xb, wm_ref[:, o + KV_W:o + 2 * KV_W])
    kvs_ref[...] = kvs
    kvsb_ref[...] = kvs.astype(BF16)
    kvw = _dot(xb, wm_ref[:, o + 2 * KV_W:o + 3 * KV_W])
    kvw_ref[...] = kvw
    kvwb_ref[...] = kvw.astype(BF16)
    ng_ref[...] = _sigmoid(_dot(xb, wng_ref[...]))
    mg_ref[...] = _sigmoid(_dot(xb, wmg_ref[...]))


def _project_in(x2d, wm, wng, wmg):
    n = x2d.shape[0]
    tm = min(PROJ_TM, n)
    row = lambda w: pl.BlockSpec((tm, w), lambda i: (i, 0))
    outs = [(D_RNN, F32), (D_RNN, F32), (N_HEADS * HEAD_DIM, BF16), (KV_W, F32), (KV_W, F32), (KV_W, F32),
            (KV_W, BF16), (KV_W, BF16), (LANES, F32), (2 * D_MODEL, F32)]
    return pl.pallas_call(
        _proj_kernel,
        grid=(n // tm,),
        in_specs=[row(D_MODEL), _full_spec(wm.shape), _full_spec(wng.shape), _full_spec(wmg.shape)],
        out_specs=[row(w) for w, _ in outs],
        out_shape=[jax.ShapeDtypeStruct((n, w), dt) for w, dt in outs],
        compiler_params=_cparams(("arbitrary",)),
        name="project_in",
    )(x2d, wm, wng, wmg)


def _shift_rows(x, s, fill):
    row = lax.broadcasted_iota(jnp.int32, x.shape, 0)
    return jnp.where(row >= s, pltpu.roll(x, s, axis=0), fill)


def _gelu_tanh(x):
    c = np.float32(np.sqrt(2.0 / np.pi))
    return 0.5 * x * (1.0 + jnp.tanh(c * (x + np.float32(0.044715) * (x * x * x))))


def _rglru_kernel(xr_ref, gr_ref, cbuf_ref, h0_ref, cw_ref, cb_ref, wa_ref, ba_ref, wx_ref, bx_ref, lam_ref,
                  y_ref, nbuf_ref, ht_ref, xbuf, hcar, *, tc, pos0):
    c = pl.program_id(1)
    nc = pl.num_programs(1)

    @pl.when(c == 0)
    def _():
        xbuf[0:8, :] = jnp.zeros((8, D_RNN), F32)
        xbuf[8 - (CONV_W - 1):8, :] = cbuf_ref[...]
        hcar[...] = h0_ref[...]

    x = xr_ref[...]
    xbuf[8:8 + tc, :] = x
    conv = cb_ref[...] + cw_ref[CONV_W - 1:CONV_W, :] * x
    for j in range(1, CONV_W):
        conv = conv + cw_ref[CONV_W - 1 - j:CONV_W - j, :] * xbuf[8 - j:8 - j + tc, :]
    nbuf = xbuf[8 + tc - (CONV_W - 1):8 + tc, :]
    xbuf[0:8, :] = xbuf[tc:tc + 8, :]

    cvb = conv.astype(BF16)
    nb4 = D_RNN // 256
    za = jnp.concatenate([_dot(cvb[:, k * 256:(k + 1) * 256], wa_ref[k]) for k in range(nb4)], axis=1)
    zx = jnp.concatenate([_dot(cvb[:, k * 256:(k + 1) * 256], wx_ref[k]) for k in range(nb4)], axis=1)
    r = _sigmoid(za + ba_ref[...])
    i = _sigmoid(zx + bx_ref[...])
    nl = -lam_ref[...]
    softplus = jnp.maximum(nl, 0.0) + jnp.log(1.0 + jnp.exp(-jnp.abs(nl)))
    log_a = (-LRU_C) * r * softplus
    a = jnp.exp(log_a)
    pos = pos0 + c * tc + lax.broadcasted_iota(jnp.int32, (tc, 1), 0)
    mult = jnp.where(pos == 0, 1.0, jnp.sqrt(1.0 - jnp.exp(2.0 * log_a)))
    b = mult * i * conv

    s = 1
    while s < tc:
        a_sh = _shift_rows(a, s, 1.0)
        b_sh = _shift_rows(b, s, 0.0)
        b = a * b_sh + b
        a = a * a_sh
        s *= 2
    h = b + a * hcar[...]
    hcar[...] = h[tc - 1:tc, :]
    y_ref[...] = (_gelu_tanh(gr_ref[...]) * h).astype(BF16)

    @pl.when(c == nc - 1)
    def _():
        nbuf_ref[...] = nbuf
        ht_ref[...] = h[tc - 1:tc, :]


def _rglru(xr, gr, conv_buf, h0, cw, cb, wa_bd, ba, wx_bd, bx, lam, pos0):
    bsz, t, _ = xr.shape
    tc = min(LRU_TC, t)
    seq = pl.BlockSpec((None, tc, D_RNN), lambda b, c: (b, c, 0))
    per_b = lambda r: pl.BlockSpec((None, r, D_RNN), lambda b, c: (b, 0, 0))
    vec = pl.BlockSpec((1, D_RNN), lambda b, c: (0, 0))
    bd = pl.BlockSpec(wa_bd.shape, lambda b, c: (0, 0, 0))
    return pl.pallas_call(
        functools.partial(_rglru_kernel, tc=tc, pos0=pos0),
        grid=(bsz, t // tc),
        in_specs=[seq, seq, per_b(CONV_W - 1), per_b(1),
                  pl.BlockSpec((CONV_W, D_RNN), lambda b, c: (0, 0)), vec, bd, vec, bd, vec, vec],
        out_specs=[seq, per_b(CONV_W - 1), per_b(1)],
        out_shape=[jax.ShapeDtypeStruct((bsz, t, D_RNN), BF16),
                   jax.ShapeDtypeStruct((bsz, CONV_W - 1, D_RNN), F32),
                   jax.ShapeDtypeStruct((bsz, 1, D_RNN), F32)],
        scratch_shapes=[pltpu.VMEM((tc + 8, D_RNN), F32), pltpu.VMEM((1, D_RNN), F32)],
        compiler_params=_cparams(("arbitrary", "arbitrary")),
        name="rglru",
    )(xr, gr, conv_buf, h0, cw, cb, wa_bd, ba, wx_bd, bx, lam)


def _segment_sums(x, w1, w2):
    r = x.shape[0]
    x3 = x.reshape(r // CMP_STRIDE, CMP_STRIDE, KV_W)
    return jnp.sum(x3 * w1[None], axis=1), jnp.sum(x3 * w2[None], axis=1)


def _compress_kernel(kv_ref, w_ref, a_ref, b_ref):
    a, b = _segment_sums(kv_ref[...], w_ref[0:CMP_STRIDE, :], w_ref[CMP_STRIDE:CMP_LEN, :])
    a_ref[...] = a
    b_ref[...] = b


def _compress_rows(kv, w):
    bsz, t, _ = kv.shape
    tr = min(1024, t)
    ns = tr // CMP_STRIDE
    return pl.pallas_call(
        _compress_kernel,
        grid=(bsz, t // tr),
        in_specs=[pl.BlockSpec((None, tr, KV_W), lambda b, c: (b, c, 0)), _full_spec(w.shape)],
        out_specs=[pl.BlockSpec((None, ns, KV_W), lambda b, c: (b, c, 0))] * 2,
        out_shape=[jax.ShapeDtypeStruct((bsz, t // CMP_STRIDE, KV_W), F32)] * 2,
        compiler_params=_cparams(("arbitrary", "arbitrary")),
        name="compress_rows",
    )(kv, w)


def _split_bf16_f32(y):
    hi = y.astype(BF16)
    return hi, y - hi.astype(F32)


def _split_bf16(y):
    hi, rest = _split_bf16_f32(y)
    return hi, rest.astype(BF16)


def _compress_pages_kernel(pt_ref, *refs):
    del pt_ref
    pages = refs[:CMP_PAGES_PER_STEP]
    w_ref, seg_ref, a_ref, b_ref = refs[CMP_PAGES_PER_STEP:]
    seg = seg_ref[...]
    for role, o_ref in enumerate((a_ref, b_ref)):
        rows = []
        for x in range(2):
            wt = w_ref[role, x]
            for g in range(KV_HEADS):
                rows.append(jnp.concatenate([p[x, g] * wt for p in pages], axis=1))
        hi, lo = _split_bf16(jnp.concatenate(rows, axis=0))
        z = _dot(hi, seg) + _dot(lo, seg)
        for x in range(2):
            for g in range(KV_HEADS):
                i = x * KV_HEADS + g
                o_ref[x, g] = z[i * HEAD_DIM:(i + 1) * HEAD_DIM, :]


def _page_specs(layer, n_pages, per_step):
    def spec(k):
        return pl.BlockSpec((None, None, 2, KV_HEADS, HEAD_DIM, PAGE_SIZE),
                            lambda b, j, pt: (layer, pt[b * n_pages + j * per_step + k], 0, 0, 0, 0))
    return [spec(k) for k in range(per_step)]


def _compress_pages(cache_t, pt_flat, wt, layer, bsz, n_pages):
    spp = PAGE_SIZE // CMP_STRIDE
    per_step = CMP_PAGES_PER_STEP
    assert per_step * spp == LANES
    steps = n_pages // per_step
    kdim = per_step * PAGE_SIZE
    lrow = lax.broadcasted_iota(jnp.int32, (kdim, LANES), 0)
    lcol = lax.broadcasted_iota(jnp.int32, (kdim, LANES), 1)
    seg = jnp.where(lrow // CMP_STRIDE == lcol, 1.0, 0.0).astype(BF16)
    out_block = pl.BlockSpec((None, 2, KV_HEADS, HEAD_DIM, LANES), lambda b, j, pt: (b, 0, 0, 0, j))
    grid_spec = pltpu.PrefetchScalarGridSpec(
        num_scalar_prefetch=1,
        grid=(bsz, steps),
        in_specs=_page_specs(layer, n_pages, per_step) + [
            pl.BlockSpec(wt.shape, lambda b, j, pt: (0, 0, 0, 0)), pl.BlockSpec(seg.shape, lambda b, j, pt: (0, 0))],
        out_specs=[out_block] * 2,
    )
    return pl.pallas_call(
        _compress_pages_kernel,
        grid_spec=grid_spec,
        out_shape=[jax.ShapeDtypeStruct((bsz, 2, KV_HEADS, HEAD_DIM, n_pages * spp), F32)] * 2,
        compiler_params=_cparams(("arbitrary", "arbitrary")),
        name="compress_pages",
    )(pt_flat, *([cache_t] * per_step), wt, seg)


def _row_slopes(g, n_q):
    r = lax.broadcasted_iota(jnp.int32, (GROUP * n_q, 1), 0) // n_q
    out = jnp.full((GROUP * n_q, 1), np.float32(SLOPES[g * GROUP + GROUP - 1]), F32)
    for k in range(GROUP - 2, -1, -1):
        out = jnp.where(r == k, np.float32(SLOPES[g * GROUP + k]), out)
    return out


def _stack_heads(q, g):
    return jnp.concatenate(
        [q[:, (g * GROUP + r) * HEAD_DIM:(g * GROUP + r + 1) * HEAD_DIM] for r in range(GROUP)], axis=0)


def _masked_softmax_rows(s, valid):
    s = jnp.where(valid, s, NEG_INF)
    m = jnp.max(s, axis=-1, keepdims=True)
    m = jnp.where(m == NEG_INF, 0.0, m)
    p = jnp.exp(s - m)
    return p / jnp.maximum(jnp.sum(p, axis=-1, keepdims=True), 1e-30)


def _flash_step(carry, s, valid, v, v_transposed=False):
    m, l, acc = carry
    s = jnp.where(valid, s, NEG_INF)
    m_new = jnp.maximum(m, jnp.max(s, axis=-1, keepdims=True))
    m_safe = jnp.where(m_new == NEG_INF, 0.0, m_new)
    alpha = jnp.exp(m - m_safe)
    p = jnp.exp(s - m_safe)
    l = alpha * l + jnp.sum(p, axis=-1, keepdims=True)
    pv = _dot_nt(p.astype(BF16), v) if v_transposed else _dot(p.astype(BF16), v)
    return m_new, l, alpha * acc + pv


def _flash_init(rows):
    return (jnp.full((rows, 1), NEG_INF, F32), jnp.zeros((rows, 1), F32), jnp.zeros((rows, HEAD_DIM), F32))


def _flash_out(carry):
    _, l, acc = carry
    return acc / jnp.maximum(l, 1e-30)


def _select_blocks(score, n_extra_above=None):
    n_blk = score.shape[1]
    lane = lax.broadcasted_iota(jnp.int32, score.shape, 1)
    cnt = jnp.zeros(score.shape, jnp.int32) if n_extra_above is None else n_extra_above
    for k in range(n_blk):
        col = score[:, k:k + 1]
        cnt = cnt + jnp.where(lane > k, jnp.where(col >= score, 1, 0), jnp.where(col > score, 1, 0))
    return cnt < N_SELECT


def _block_sum_matrix(n_cmp, n_blk):
    c = lax.broadcasted_iota(jnp.int32, (n_cmp, n_blk), 0)
    j = lax.broadcasted_iota(jnp.int32, (n_cmp, n_blk), 1)
    return jnp.where(c // PER_SEL == j, 1.0, 0.0).astype(F32)


def _expand_matrix(n_blk, blk0, k0, kc):
    j = lax.broadcasted_iota(jnp.int32, (n_blk, kc), 0) + blk0
    c = lax.broadcasted_iota(jnp.int32, (n_blk, kc), 1) + k0
    return jnp.where(c // SEL_BLOCK == j, 1.0, 0.0).astype(BF16)


def _sum_heads(p, n_q):
    out = p[0:n_q]
    for r in range(1, GROUP):
        out = out + p[r * n_q:(r + 1) * n_q]
    return out


def _tile_rows(x):
    return jnp.concatenate([x] * GROUP, axis=0)


def _gate_and_pack(ng, branches, n_q):
    pieces = []
    for g in range(KV_HEADS):
        for r in range(GROUP):
            idx = g * GROUP + r
            acc = None
            for br, o in enumerate(branches[g]):
                term = ng[:, br * N_HEADS + idx:br * N_HEADS + idx + 1] * o[r * n_q:(r + 1) * n_q]
                acc = term if acc is None else acc + term
            pieces.append(acc)
    return jnp.concatenate(pieces, axis=1)


def _select_blocks_t(score_t):
    n_blk = score_t.shape[0]
    slab = 8
    slabs = [score_t[i:i + slab] for i in range(0, n_blk, slab)]
    cnts = [jnp.zeros(s.shape, jnp.int32) for s in slabs]
    row = lax.broadcasted_iota(jnp.int32, slabs[0].shape, 0)
    for k in range(n_blk):
        pivot = score_t[k:k + 1, :]
        for i, s in enumerate(slabs):
            lo = i * slab
            if lo > k:
                beats = jnp.where(pivot >= s, 1, 0)
            elif lo + slab - 1 < k:
                beats = jnp.where(pivot > s, 1, 0)
            else:
                beats = jnp.where(row + lo > k, jnp.where(pivot >= s, 1, 0), jnp.where(pivot > s, 1, 0))
            cnts[i] = cnts[i] + beats
    cnt = jnp.concatenate(cnts, axis=0)
    return jnp.where(cnt < N_SELECT, 1.0, 0.0)


def _project_t_kernel(x_ref, wt_ref, qt_ref, vst_ref, vwt_ref, ngt_ref):
    xb = x_ref[...].astype(BF16)
    nq = N_HEADS * HEAD_DIM
    nv = KV_W // 2
    qt_ref[...] = (_dot_nt(wt_ref[0:nq, :], xb) * QK_SCALE).astype(BF16)
    vst_ref[...] = _dot_nt(wt_ref[nq:nq + nv, :], xb).astype(BF16)
    vwt_ref[...] = _dot_nt(wt_ref[nq + nv:nq + 2 * nv, :], xb).astype(BF16)
    ngt_ref[...] = _sigmoid(_dot_nt(wt_ref[nq + 2 * nv:nq + 2 * nv + LANES, :], xb))


def _project_t(x2d, wt):
    n = x2d.shape[0]
    tm = min(PROJ_TM, n)
    outs = [(N_HEADS * HEAD_DIM, BF16), (KV_W // 2, BF16), (KV_W // 2, BF16), (LANES, F32)]
    return pl.pallas_call(
        _project_t_kernel,
        grid=(n // tm,),
        in_specs=[pl.BlockSpec((tm, D_MODEL), lambda i: (i, 0)), _full_spec(wt.shape)],
        out_specs=[pl.BlockSpec((r, tm), lambda i: (0, i)) for r, _ in outs],
        out_shape=[jax.ShapeDtypeStruct((r, n), dt) for r, dt in outs],
        compiler_params=_cparams(("arbitrary",)),
        name="project_t",
    )(x2d, wt)


def _exp_cols(s):
    m = jnp.max(s, axis=0, keepdims=True)
    m = jnp.where(m == NEG_INF, 0.0, m)
    e = jnp.exp(s - m)
    return e, 1.0 / jnp.maximum(jnp.sum(e, axis=0, keepdims=True), 1e-30)


def _add_head_bias_t(s, table_ref, g, mask, n_q):
    return jnp.concatenate(
        [s[:, r * n_q:(r + 1) * n_q] + (table_ref[g * GROUP + r] + mask) for r in range(GROUP)], axis=1)


def _lane_slopes(g, n_q):
    r = lax.broadcasted_iota(jnp.int32, (1, GROUP * n_q), 1) // n_q
    out = jnp.full((1, GROUP * n_q), np.float32(SLOPES[g * GROUP + GROUP - 1]), F32)
    for k in range(GROUP - 2, -1, -1):
        out = jnp.where(r == k, np.float32(SLOPES[g * GROUP + k]), out)
    return out


def _nsa_prompt_t_kernel(qt_ref, ngt_ref, ca_ref, cb_ref, ks_ref, kw_ref, vst_ref, vwt_ref, o_ref,
                         aux_s, bwin, bcmp, rhs_s, seln_s, oc_s, ow_s, m_s, l_s, acc_s, *, t, qb, kc, wk):
    n_seg = ca_ref.shape[0]
    n_cmp_valid = (t - CMP_LEN) // CMP_STRIDE + 1
    n_blk = t // SEL_BLOCK
    nv = KV_W // 2
    ql = lax.broadcasted_iota(jnp.int32, (1, qb), 1)

    def krow(n):
        return lax.broadcasted_iota(jnp.int32, (n, 1), 0)

    @pl.when((pl.program_id(0) == 0) & (pl.program_id(1) == 0))
    def _():
        d_win = (ql - krow(wk)).astype(F32)
        d_cmp = (ql - (krow(n_seg) * CMP_STRIDE + (CMP_LEN - 1))).astype(F32)
        for h in range(N_HEADS):
            slope = np.float32(SLOPES[h])
            bwin[h] = -(slope * d_win)
            bcmp[h] = -(slope * d_cmp)
        c = lax.broadcasted_iota(jnp.int32, (kc, LANES), 0)
        ln = lax.broadcasted_iota(jnp.int32, (kc, LANES), 1)
        feat = jnp.where(ln < AUX_BLOCK_ROWS, jnp.where(c // SEL_BLOCK == ln, 1, 0),
                         jnp.where((ln >= AUX_BLOCK_ROWS) & (ln < AUX_BLOCK_ROWS + 3), (c // 16) * 16,
                                   jnp.where((ln >= AUX_BLOCK_ROWS + 3) & (ln < AUX_BLOCK_ROWS + 6), c % 16, 0)))
        aux_s[...] = feat.astype(F32).astype(BF16)
        for g in range(KV_HEADS):
            slope = _lane_slopes(g, qb)
            s1, r1 = _split_bf16_f32(slope)
            s2, r2 = _split_bf16_f32(r1)
            s3 = r2.astype(BF16)
            rhs_s[g, 0:AUX_BLOCK_ROWS, :] = jnp.zeros((AUX_BLOCK_ROWS, GROUP * qb), BF16)
            rhs_s[g, AUX_BLOCK_ROWS:2 * AUX_BLOCK_ROWS, :] = jnp.concatenate(
                [s1, s2, s3, s1, s2, s3, jnp.zeros((AUX_BLOCK_ROWS - 6, GROUP * qb), BF16)], axis=0)
            rhs_s[g, 2 * AUX_BLOCK_ROWS:LANES, :] = jnp.zeros((LANES - 2 * AUX_BLOCK_ROWS, GROUP * qb), BF16)

    s0 = pl.program_id(1) * qb

    def q_t(g):
        return jnp.concatenate(
            [qt_ref[(g * GROUP + r) * HEAD_DIM:(g * GROUP + r + 1) * HEAD_DIM, :] for r in range(GROUP)], axis=1)

    ckv = ca_ref[...] + pltpu.roll(cb_ref[...], n_seg - 1, axis=0)
    ck = ckv[:, 0:nv].astype(BF16)
    cv_t = ckv[:, nv:2 * nv].T.astype(BF16)

    cc = krow(n_seg)
    cmask = jnp.where((s0 + ql - (cc * CMP_STRIDE + (CMP_LEN - 1)) >= 0) & (cc < n_cmp_valid), 0.0, NEG_INF)
    kstart = pl.multiple_of(jnp.clip(s0 + qb - wk, 0, t - wk), qb)
    wd = (s0 - kstart) + ql - krow(wk)
    wmask = jnp.where((wd >= 0) & (wd < WINDOW), 0.0, NEG_INF)

    gj = lax.broadcasted_iota(jnp.int32, (n_blk, n_seg), 0)
    gc = lax.broadcasted_iota(jnp.int32, (n_blk, n_seg), 1)
    gsum_t = jnp.where(gc // PER_SEL == gj, 1.0, 0.0).astype(BF16)
    blk_t = krow(n_blk)
    cur_t = (s0 + ql) // SEL_BLOCK
    forced_t = (blk_t == 0) | (blk_t == cur_t) | (blk_t == cur_t - 1)

    heads = range(KV_HEADS)
    hsl = [slice(g * HEAD_DIM, (g + 1) * HEAD_DIM) for g in heads]
    qgs = [q_t(g) for g in heads]
    for g in heads:
        rhs_s[g, LANES:LANES + HEAD_DIM, :] = qgs[g]
    s_c = [_add_head_bias_t(_dot(ck[:, hsl[g]], qgs[g]), bcmp, g, cmask, qb) for g in heads]
    s_w = [_add_head_bias_t(_dot(kw_ref[pl.ds(kstart, wk), hsl[g]], qgs[g]), bwin, g, wmask, qb) for g in heads]
    ex_c = [_exp_cols(s) for s in s_c]
    ex_w = [_exp_cols(s) for s in s_w]
    imps = []
    for g in heads:
        e_hi, e_lo = _split_bf16(ex_c[g][0])
        oc_s[g] = _dot(cv_t[hsl[g], :], e_hi) * ex_c[g][1]
        imps.append((_dot(gsum_t, e_hi) + _dot(gsum_t, e_lo)) * ex_c[g][1])
        ow_s[g] = _dot(vwt_ref[hsl[g], pl.ds(kstart, wk)], ex_w[g][0].astype(BF16)) * ex_w[g][1]
    for g in heads:
        imp_t = imps[g][:, 0:qb]
        for r in range(1, GROUP):
            imp_t = imp_t + imps[g][:, r * qb:(r + 1) * qb]
        score_t = jnp.where(forced_t, FORCE_BONUS, imp_t)
        score_t = jnp.where(blk_t <= cur_t, score_t, NEG_INF)
        sel_t = jnp.where(blk_t <= cur_t, _select_blocks_t(score_t), 0.0)
        seln_s[g] = jnp.concatenate([jnp.where(sel_t > 0.5, 0.0, MASK_NEG)] * GROUP, axis=1)

    m_s[...] = jnp.full(m_s.shape, NEG_INF, F32)
    l_s[...] = jnp.zeros(l_s.shape, F32)
    acc_s[...] = jnp.zeros(acc_s.shape, F32)

    bpc = kc // SEL_BLOCK
    ql4 = jnp.concatenate([ql] * GROUP, axis=1)

    def chunk(c, diagonal):
        k0 = pl.multiple_of(c * kc, kc)
        delta = jnp.zeros((1, 1), jnp.int32) + (s0 - k0)
        qoff = (delta + ql4).astype(F32)
        if diagonal:
            causal = (delta + ql4 - krow(kc)) >= 0
        aux = aux_s[...]
        state = [(m_s[g], l_s[g], acc_s[g]) for g in range(KV_HEADS)]
        scores = []
        for g in range(KV_HEADS):
            blocks = seln_s[g, pl.ds(pl.multiple_of(c * bpc, bpc), bpc), :]
            rhs_s[g, 0:AUX_BLOCK_ROWS, :] = jnp.concatenate(
                [blocks, jnp.zeros((AUX_BLOCK_ROWS - bpc, GROUP * qb), F32)], axis=0).astype(BF16)
        for g in range(KV_HEADS):
            lhs = jnp.concatenate([aux, ks_ref[pl.ds(k0, kc), g * HEAD_DIM:(g + 1) * HEAD_DIM]], axis=1)
            s = _dot(lhs, rhs_s[g])
            scores.append(jnp.where(causal, s, MASK_NEG) if diagonal else s)
        probs, stats = [], []
        for g in range(KV_HEADS):
            m_prev, l_prev, _ = state[g]
            off = _lane_slopes(g, qb) * qoff
            m_new = jnp.maximum(m_prev, jnp.max(scores[g], axis=0, keepdims=True) - off)
            m_safe = jnp.where(m_new == NEG_INF, 0.0, m_new)
            alpha = jnp.exp(m_prev - m_safe)
            p = jnp.exp(scores[g] - (m_safe + off))
            stats.append((m_new, alpha * l_prev + jnp.sum(p, axis=0, keepdims=True), alpha))
            probs.append(p.astype(BF16))
        for g in range(KV_HEADS):
            m_new, l_new, alpha = stats[g]
            pv = _dot(vst_ref[g * HEAD_DIM:(g + 1) * HEAD_DIM, pl.ds(k0, kc)], probs[g])
            m_s[g] = m_new
            l_s[g] = l_new
            acc_s[g] = alpha * state[g][2] + pv

    def body(c, carry):
        chunk(c, False)
        return carry

    last = (s0 + qb + kc - 1) // kc - 1
    lax.fori_loop(0, last, body, 0)
    chunk(last, True)

    ngt = ngt_ref[...]
    pieces = []
    for g in range(KV_HEADS):
        o_s = acc_s[g] / jnp.maximum(l_s[g], 1e-30)
        o_c = oc_s[g]
        o_w = ow_s[g]
        for r in range(GROUP):
            h = g * GROUP + r
            sl = slice(r * qb, (r + 1) * qb)
            pieces.append(ngt[h:h + 1, :] * o_c[:, sl] + ngt[N_HEADS + h:N_HEADS + h + 1, :] * o_s[:, sl]
                          + ngt[2 * N_HEADS + h:2 * N_HEADS + h + 1, :] * o_w[:, sl])
    o_ref[...] = jnp.concatenate(pieces, axis=0).T.astype(BF16)


def _nsa_prompt_t(qt, ngt, ca, cb, ksb, kwb, vst, vwt, bsz, t):
    qb = min(ATT_QB, t)
    kc = min(ATT_KC, t)
    wk = min(WINDOW + qb, t)
    n_seg = ca.shape[1]
    n_blk = t // SEL_BLOCK
    nq = t // qb
    cols = GROUP * qb
    per_b = lambda r, w: pl.BlockSpec((None, r, w), lambda b, i: (b, 0, 0))
    tok = lambda r: pl.BlockSpec((r, qb), lambda b, i: (0, b * nq + i))
    seq = lambda r: pl.BlockSpec((r, t), lambda b, i: (0, b))
    return pl.pallas_call(
        functools.partial(_nsa_prompt_t_kernel, t=t, qb=qb, kc=kc, wk=wk),
        grid=(bsz, nq),
        in_specs=[tok(N_HEADS * HEAD_DIM), tok(LANES), per_b(n_seg, KV_W), per_b(n_seg, KV_W),
                  per_b(t, KV_W), per_b(t, KV_W), seq(KV_W // 2), seq(KV_W // 2)],
        out_specs=pl.BlockSpec((None, qb, N_HEADS * HEAD_DIM), lambda b, i: (b, i, 0)),
        out_shape=jax.ShapeDtypeStruct((bsz, t, N_HEADS * HEAD_DIM), BF16),
        scratch_shapes=[pltpu.VMEM((kc, LANES), BF16), pltpu.VMEM((N_HEADS, wk, qb), F32),
                        pltpu.VMEM((N_HEADS, n_seg, qb), F32), pltpu.VMEM((KV_HEADS, LANES + HEAD_DIM, cols), BF16),
                        pltpu.VMEM((KV_HEADS, n_blk, cols), F32),
                        pltpu.VMEM((KV_HEADS, HEAD_DIM, cols), F32), pltpu.VMEM((KV_HEADS, HEAD_DIM, cols), F32),
                        pltpu.VMEM((KV_HEADS, 1, cols), F32), pltpu.VMEM((KV_HEADS, 1, cols), F32),
                        pltpu.VMEM((KV_HEADS, HEAD_DIM, cols), F32)],
        compiler_params=_cparams(("arbitrary", "arbitrary")),
        name="nsa_prompt",
    )(qt, ngt, ca, cb, ksb, kwb, vst, vwt)


def _nsa_sample_select_kernel(q_ref, ca_ref, cb_ref, oc_ref, sel_ref, *, ts, past):
    n_seg = ca_ref.shape[-1]
    n_cmp_valid = n_seg - 1
    n_pblk = past // SEL_BLOCK
    qrow = past + lax.broadcasted_iota(jnp.int32, (ts, 1), 0)
    qpos = _tile_rows(qrow)
    q = q_ref[...]
    gsum = _block_sum_matrix(n_seg, n_pblk)
    blk = lax.broadcasted_iota(jnp.int32, (1, n_pblk), 1)
    cur = qrow // SEL_BLOCK
    forced = (blk == 0) | (blk == cur) | (blk == cur - 1)
    col = lax.broadcasted_iota(jnp.int32, (1, n_seg), 1)
    d = qpos - (col * CMP_STRIDE + (CMP_LEN - 1))
    valid = (d >= 0) & (col < n_cmp_valid)
    df = d.astype(F32)
    outs = []
    for g in range(KV_HEADS):
        qg = _stack_heads(q, g)
        ck_t = (ca_ref[0, g] + pltpu.roll(cb_ref[0, g], n_seg - 1, axis=1)).astype(BF16)
        cv_t = (ca_ref[1, g] + pltpu.roll(cb_ref[1, g], n_seg - 1, axis=1)).astype(BF16)
        s = _dot(qg, ck_t) * QK_SCALE - _row_slopes(g, ts) * df
        p_c = _masked_softmax_rows(s, valid)
        outs.append(_dot_nt(p_c.astype(BF16), cv_t))
        imp = jnp.dot(_sum_heads(p_c, ts), gsum, preferred_element_type=F32, precision=lax.Precision.HIGHEST)
        score = jnp.where(forced, FORCE_BONUS, imp)
        above = jnp.where(FORCE_BONUS > score, 1, 0)
        sel_ref[g * ts:(g + 1) * ts, :] = jnp.where(_select_blocks(score, above), 1.0, 0.0)
    for g in range(KV_HEADS):
        oc_ref[g * GROUP * ts:(g + 1) * GROUP * ts, :] = outs[g]


def _nsa_sample_select(q, ca, cb, past):
    bsz, ts, _ = q.shape
    n_seg = ca.shape[-1]
    n_pblk = past // SEL_BLOCK
    per_b = lambda r, w: pl.BlockSpec((None, r, w), lambda b: (b, 0, 0))
    seg_t = pl.BlockSpec((None, 2, KV_HEADS, HEAD_DIM, n_seg), lambda b: (b, 0, 0, 0, 0))
    return pl.pallas_call(
        functools.partial(_nsa_sample_select_kernel, ts=ts, past=past),
        grid=(bsz,),
        in_specs=[per_b(ts, N_HEADS * HEAD_DIM), seg_t, seg_t],
        out_specs=[per_b(N_HEADS * ts, HEAD_DIM), per_b(KV_HEADS * ts, n_pblk)],
        out_shape=[jax.ShapeDtypeStruct((bsz, N_HEADS * ts, HEAD_DIM), F32),
                   jax.ShapeDtypeStruct((bsz, KV_HEADS * ts, n_pblk), F32)],
        compiler_params=_cparams(("arbitrary",)),
        name="nsa_sample_select",
    )(q, ca, cb)


def _nsa_sample_attend_kernel(pt_ref, *refs, ts, past, pad, n_pg):
    del pt_ref
    pages = refs[:n_pg]
    (q_ref, ng_ref, oc_ref, sel_ref, tail_ref, wbuf_ref, wnew_ref, o_ref, m_s, l_s, acc_s) = refs[n_pg:]
    j = pl.program_id(1)
    nj = pl.num_programs(1)
    rows = GROUP * ts
    kc = n_pg * PAGE_SIZE
    bps = kc // SEL_BLOCK
    qrow = past + lax.broadcasted_iota(jnp.int32, (ts, 1), 0)
    qpos = _tile_rows(qrow)
    q = q_ref[...]

    @pl.when(j == 0)
    def _():
        m_s[...] = jnp.full(m_s.shape, NEG_INF, F32)
        l_s[...] = jnp.zeros(l_s.shape, F32)
        acc_s[...] = jnp.zeros(acc_s.shape, F32)

    k0 = j * kc
    d = qpos - (k0 + lax.broadcasted_iota(jnp.int32, (1, kc), 1))
    df = d.astype(F32)
    expand = _expand_matrix(bps, 0, 0, kc)
    state = [(m_s[g], l_s[g], acc_s[g]) for g in range(KV_HEADS)]
    new_state = []
    for g in range(KV_HEADS):
        qg = _stack_heads(q, g)
        k_t = jnp.concatenate([p[0, g] for p in pages], axis=1).astype(BF16)
        v_t = jnp.concatenate([p[1, g] for p in pages], axis=1).astype(BF16)
        s = _dot(qg, k_t) * QK_SCALE - _row_slopes(g, ts) * df
        selk = _tile_rows(_dot(sel_ref[g * ts:(g + 1) * ts, :].astype(BF16), expand))
        new_state.append(_flash_step(state[g], s, (d >= 0) & (selk > 0.5), v_t, v_transposed=True))
    for g in range(KV_HEADS):
        m_s[g], l_s[g], acc_s[g] = new_state[g]

    @pl.when(j == nj - 1)
    def _():
        ng = ng_ref[...]
        tail = tail_ref[...]
        wnew = wnew_ref[...]
        wb = wbuf_ref.shape[-1]
        colp = lax.broadcasted_iota(jnp.int32, (1, pad), 1)
        d_new = qpos - (past + colp)
        ok_new = (d_new >= 0) & (colp < ts)
        d_old = qpos - (past - wb + lax.broadcasted_iota(jnp.int32, (1, wb), 1))
        ok_old = (d_old >= 0) & (d_old < WINDOW)
        branches = []
        for g in range(KV_HEADS):
            qg = _stack_heads(q, g)
            slope = _row_slopes(g, ts)
            ksl = slice(g * HEAD_DIM, (g + 1) * HEAD_DIM)
            vsl = slice(KV_W // 2 + g * HEAD_DIM, KV_W // 2 + (g + 1) * HEAD_DIM)
            sl = slice(g * rows, (g + 1) * rows)
            s = _dot_nt(qg, tail[:, ksl]) * QK_SCALE - slope * d_new.astype(F32)
            o_s = _flash_out(_flash_step((m_s[g], l_s[g], acc_s[g]), s, ok_new, tail[:, vsl]))
            s = _dot(qg, wbuf_ref[0, g].astype(BF16)) * QK_SCALE - slope * d_old.astype(F32)
            carry = _flash_step(_flash_init(rows), s, ok_old, wbuf_ref[1, g].astype(BF16), v_transposed=True)
            s = _dot_nt(qg, wnew[:, ksl]) * QK_SCALE - slope * d_new.astype(F32)
            o_w = _flash_out(_flash_step(carry, s, ok_new & (d_new < WINDOW), wnew[:, vsl]))
            branches.append([oc_ref[sl, :], o_s, o_w])
        o_ref[...] = _gate_and_pack(ng, branches, ts).astype(BF16)


def _nsa_sample_attend(cache_t, pt_flat, layer, q, ng, oc, sel, tail, wbuf_t, wnew, past):
    bsz, ts, _ = q.shape
    n_pages = past // PAGE_SIZE
    n_pg = min(PAGES_PER_STEP, n_pages)
    steps = n_pages // n_pg
    pad = tail.shape[1]
    wb = wbuf_t.shape[-1]
    bps = n_pg * PAGE_SIZE // SEL_BLOCK
    sel4 = sel.reshape(bsz, KV_HEADS * ts, steps, bps).transpose(0, 2, 1, 3)
    per_b = lambda r, w: pl.BlockSpec((None, r, w), lambda b, j, pt: (b, 0, 0))
    grid_spec = pltpu.PrefetchScalarGridSpec(
        num_scalar_prefetch=1,
        grid=(bsz, steps),
        in_specs=_page_specs(layer, n_pages, n_pg) + [
            per_b(ts, N_HEADS * HEAD_DIM), per_b(ts, LANES), per_b(N_HEADS * ts, HEAD_DIM),
            pl.BlockSpec((None, None, KV_HEADS * ts, bps), lambda b, j, pt: (b, j, 0, 0)),
            per_b(pad, KV_W),
            pl.BlockSpec((None, None, 2, KV_HEADS, HEAD_DIM, wb), lambda b, j, pt: (layer, b, 0, 0, 0, 0)),
            per_b(pad, KV_W)],
        out_specs=per_b(ts, N_HEADS * HEAD_DIM),
        scratch_shapes=[pltpu.VMEM((KV_HEADS, GROUP * ts, 1), F32), pltpu.VMEM((KV_HEADS, GROUP * ts, 1), F32),
                        pltpu.VMEM((KV_HEADS, GROUP * ts, HEAD_DIM), F32)],
    )
    return pl.pallas_call(
        functools.partial(_nsa_sample_attend_kernel, ts=ts, past=past, pad=pad, n_pg=n_pg),
        grid_spec=grid_spec,
        out_shape=jax.ShapeDtypeStruct((bsz, ts, N_HEADS * HEAD_DIM), BF16),
        compiler_params=_cparams(("arbitrary", "arbitrary")),
        name="nsa_sample_attend",
    )(pt_flat, *([cache_t] * n_pg), q, ng, oc, sel4, tail, wbuf_t, wnew)


def _layer_norm(x, g, b):
    mu = jnp.mean(x, axis=-1, keepdims=True)
    xc = x - mu
    var = jnp.mean(xc * xc, axis=-1, keepdims=True)
    return xc * lax.rsqrt(var + LN_EPS) * g + b


def _merge_kernel(x_ref, yr_ref, ya_ref, mg_ref, wa_ref, wb_ref, wo_ref, g_ref, b_ref, wr_ref, br_ref,
                  o_ref, ob_ref, comb_ref):
    a = _dot(yr_ref[...], wa_ref[...])
    b = _dot(ya_ref[...], wb_ref[...])
    merged = mg_ref[:, 0:D_MODEL] * a + mg_ref[:, D_MODEL:2 * D_MODEL] * b
    mix = _dot(merged.astype(BF16), wo_ref[...])
    x1 = _layer_norm(ALPHA * x_ref[...] + mix, g_ref[...], b_ref[...])
    o_ref[...] = x1
    x1b = x1.astype(BF16)
    ob_ref[...] = x1b
    comb_ref[...] = _route(_dot(x1b, wr_ref[...]) + br_ref[...])


def _route(logits):
    n = logits.shape[0]
    lane = lax.broadcasted_iota(jnp.int32, (n, LANES), 1)
    is_g = lane < N_GROUPS
    gl = jnp.where(is_g, logits, NEG_INF)
    gmax = jnp.max(gl, axis=-1, keepdims=True)
    grp = jnp.min(jnp.where(gl == gmax, lane, LANES), axis=-1, keepdims=True)
    gw = 1.0 / jnp.sum(jnp.exp(gl - gmax), axis=-1, keepdims=True)
    e_idx = lane - N_GROUPS
    in_grp = (e_idx >= grp * EXP_PER_GROUP) & (e_idx < (grp + 1) * EXP_PER_GROUP)
    el = jnp.where(in_grp, logits, NEG_INF)
    v1 = jnp.max(el, axis=-1, keepdims=True)
    i1 = jnp.min(jnp.where(el == v1, lane, LANES), axis=-1, keepdims=True)
    el2 = jnp.where(lane == i1, NEG_INF, el)
    v2 = jnp.max(el2, axis=-1, keepdims=True)
    i2 = jnp.min(jnp.where(el2 == v2, lane, LANES), axis=-1, keepdims=True)
    e2 = jnp.exp(v2 - v1)
    w1 = gw / (1.0 + e2)
    w2 = gw * e2 / (1.0 + e2)
    comb = jnp.where(lane == i1, w1, 0.0) + jnp.where(lane == i2, w2, 0.0)
    return pltpu.roll(comb, LANES - N_GROUPS, axis=1)


def _merge_out(x2d, yr, ya, mg, wa, wb, wo, g, b, wr, br):
    n = x2d.shape[0]
    tm = min(MERGE_TM, n)
    row = lambda w: pl.BlockSpec((tm, w), lambda i: (i, 0))
    sq = _full_spec((D_MODEL, D_MODEL))
    vec = _full_spec((1, D_MODEL))
    return pl.pallas_call(
        _merge_kernel,
        grid=(n // tm,),
        in_specs=[row(D_MODEL), row(D_RNN), row(N_HEADS * HEAD_DIM), row(2 * D_MODEL), sq, sq, sq, vec, vec,
                  _full_spec((D_MODEL, LANES)), _full_spec((1, LANES))],
        out_specs=[row(D_MODEL), row(D_MODEL), row(LANES)],
        out_shape=[jax.ShapeDtypeStruct((n, D_MODEL), F32), jax.ShapeDtypeStruct((n, D_MODEL), BF16),
                   jax.ShapeDtypeStruct((n, LANES), F32)],
        compiler_params=_cparams(("arbitrary",)),
        name="merge_out",
    )(x2d, yr, ya, mg, wa, wb, wo, g, b, wr, br)


def _moe_kernel(x_ref, xb_ref, comb_ref, w1_ref, w3_ref, w2_ref, g_ref, b_ref, o_ref, acc):
    e = pl.program_id(1)
    ne = pl.num_programs(1)

    @pl.when(e == 0)
    def _():
        acc[...] = jnp.zeros(acc.shape, F32)

    xb = xb_ref[...]
    lane = lax.broadcasted_iota(jnp.int32, comb_ref.shape, 1)
    cw = jnp.sum(jnp.where(lane == e, comb_ref[...], 0.0), axis=-1, keepdims=True)
    h1 = _dot(xb, w1_ref[...])
    h3 = _dot(xb, w3_ref[...])
    h = (h1 * _sigmoid(h1)) * h3 * cw
    acc[...] += _dot(h.astype(BF16), w2_ref[...])

    @pl.when(e == ne - 1)
    def _():
        o_ref[...] = _layer_norm(ALPHA * x_ref[...] + acc[...], g_ref[...], b_ref[...])


def _moe(x1, x1b, comb, w1, w3, w2, g, b):
    n = x1.shape[0]
    tm = min(MOE_TM, n)
    row = lambda w: pl.BlockSpec((tm, w), lambda i, e: (i, 0))
    vec = pl.BlockSpec((1, D_MODEL), lambda i, e: (0, 0))
    return pl.pallas_call(
        _moe_kernel,
        grid=(n // tm, N_EXPERTS),
        in_specs=[row(D_MODEL), row(D_MODEL), row(LANES),
                  pl.BlockSpec((None, D_MODEL, D_EXPERT), lambda i, e: (e, 0, 0)),
                  pl.BlockSpec((None, D_MODEL, D_EXPERT), lambda i, e: (e, 0, 0)),
                  pl.BlockSpec((None, D_EXPERT, D_MODEL), lambda i, e: (e, 0, 0)),
                  vec, vec],
        out_specs=row(D_MODEL),
        out_shape=jax.ShapeDtypeStruct((n, D_MODEL), F32),
        scratch_shapes=[pltpu.VMEM((tm, D_MODEL), F32)],
        compiler_params=_cparams(("arbitrary", "arbitrary")),
        name="moe",
    )(x1, x1b, comb, w1, w3, w2, g, b)


def _block_diag4(w):
    w4 = w.reshape(N_RNN_BLOCKS // 4, 4, RNN_BLOCK, RNN_BLOCK)
    eye = jnp.eye(4, dtype=w.dtype)
    return jnp.einsum("kaij,ab->kaibj", w4, eye).reshape(N_RNN_BLOCKS // 4, 256, 256).astype(BF16)


def _layer_params(l, w_in, conv_w, conv_b, lru_wa, lru_ba, lru_wx, lru_bx, lru_lambda, cmp_w,
                  w_br_a, w_br_b, w_out, ln1_g, ln1_b, ln2_g, ln2_b,
                  router_wg, router_bg, router_we, router_be, exp_w1, exp_w3, exp_w2):
    n_main = 3 * D_RNN + 3 * KV_W
    n_ng = 3 * N_HEADS
    w = w_in[l]
    vec = lambda a: a[l].reshape(1, -1)
    wr = jnp.concatenate([router_wg[l], router_we[l]], axis=1)
    br = jnp.concatenate([router_bg[l], router_be[l]])
    npad = LANES - wr.shape[1]
    return dict(
        wm=w[:, :n_main].astype(BF16),
        wng=jnp.pad(w[:, n_main:n_main + n_ng], ((0, 0), (0, LANES - n_ng))).astype(BF16),
        wmg=w[:, n_main + n_ng:].astype(BF16),
        wt=jnp.concatenate([w[:, 2 * D_RNN:3 * D_RNN], w[:, 3 * D_RNN + KV_W + KV_W // 2:3 * D_RNN + 2 * KV_W],
                            w[:, 3 * D_RNN + 2 * KV_W + KV_W // 2:n_main],
                            jnp.pad(w[:, n_main:n_main + n_ng], ((0, 0), (0, LANES - n_ng)))], axis=1).T.astype(BF16),
        cw=conv_w[l], cb=vec(conv_b),
        wa=_block_diag4(lru_wa[l]), ba=vec(lru_ba), wx=_block_diag4(lru_wx[l]), bx=vec(lru_bx),
        lam=vec(lru_lambda),
        cmpw=jnp.broadcast_to(cmp_w[l][:, :, None, :], (CMP_LEN, 2, KV_HEADS, HEAD_DIM)).reshape(CMP_LEN, KV_W),
        cmpwt=jnp.tile(cmp_w[l].reshape(2, CMP_STRIDE, 2, HEAD_DIM).transpose(0, 2, 3, 1), (1, 1, 1, LANES // CMP_STRIDE)),
        wbra=w_br_a[l].astype(BF16), wbrb=w_br_b[l].astype(BF16), wout=w_out[l].astype(BF16),
        ln1g=vec(ln1_g), ln1b=vec(ln1_b), ln2g=vec(ln2_g), ln2b=vec(ln2_b),
        wr=jnp.pad(wr, ((0, 0), (0, npad))).astype(BF16), br=jnp.pad(br, (0, npad)).reshape(1, LANES),
        w1=exp_w1[l].astype(BF16), w3=exp_w3[l].astype(BF16), w2=exp_w2[l].astype(BF16),
    )


def _mixer_tail(p, x2d, y_rnn, y_att, mg):
    n = x2d.shape[0]
    x1, x1b, comb = _merge_out(x2d, y_rnn.reshape(n, D_RNN), y_att.reshape(n, -1), mg,
                               p["wbra"], p["wbrb"], p["wout"], p["ln1g"], p["ln1b"], p["wr"], p["br"])
    return _moe(x1, x1b, comb, p["w1"], p["w3"], p["w2"], p["ln2g"], p["ln2b"])


def kernel(x_prompt, x_sample, cache_cmp, cache_sel, cache_win, state_conv, state_lru, page_table,
           w_in, conv_w, conv_b, lru_wa, lru_ba, lru_wx, lru_bx, lru_lambda, cmp_w,
           w_br_a, w_br_b, w_out, ln1_g, ln1_b, ln2_g, ln2_b,
           router_wg, router_bg, router_we, router_be, exp_w1, exp_w3, exp_w2):
    bp, tp, _ = x_prompt.shape
    bs, ts, _ = x_sample.shape
    n_pages = page_table.shape[1]
    past = n_pages * PAGE_SIZE
    assert ts <= SEL_BLOCK and n_pages % CMP_PAGES_PER_STEP == 0 and tp % SEL_BLOCK == 0
    depth = w_in.shape[0]
    rows_minor = lambda a: a.transpose(0, 1, 3, 4, 5, 2)
    ccmp = rows_minor(cache_cmp)
    csel = rows_minor(cache_sel)
    cwin = rows_minor(cache_win)
    pt_flat = page_table.reshape(-1).astype(jnp.int32)
    pad = LANES
    kv6 = lambda a, b, t: a.reshape(b, t, 2, KV_HEADS, HEAD_DIM)

    xp = x_prompt.reshape(bp * tp, D_MODEL)
    xs = x_sample.reshape(bs * ts, D_MODEL)
    outs = [[] for _ in range(10)]
    for l in range(depth):
        p = _layer_params(l, w_in, conv_w, conv_b, lru_wa, lru_ba, lru_wx, lru_bx, lru_lambda, cmp_w,
                          w_br_a, w_br_b, w_out, ln1_g, ln1_b, ln2_g, ln2_b,
                          router_wg, router_bg, router_we, router_be, exp_w1, exp_w3, exp_w2)
        lru = (p["cw"], p["cb"], p["wa"], p["ba"], p["wx"], p["bx"], p["lam"])

        xr, gr, q, kvc, kvs, kvw, kvsb, kvwb, ng, mg = _project_in(xp, p["wm"], p["wng"], p["wmg"])
        r3 = lambda a: a.reshape(bp, tp, -1)
        y_rnn, cbuf, ht = _rglru(r3(xr), r3(gr), jnp.zeros((bp, CONV_W - 1, D_RNN), F32),
                                 jnp.zeros((bp, 1, D_RNN), F32), *lru, pos0=0)
        ca, cb = _compress_rows(r3(kvc), p["cmpw"])
        qt, vst, vwt, ngt = _project_t(xp, p["wt"])
        y_att = _nsa_prompt_t(qt, ngt, ca, cb, r3(kvsb), r3(kvwb), vst, vwt, bp, tp)
        xp = _mixer_tail(p, xp, y_rnn, y_att, mg)
        wlen = min(WINDOW, tp)
        outs[0].append(kv6(kvc, bp, tp))
        outs[2].append(kv6(kvs, bp, tp))
        outs[4].append(kv6(kvw, bp, tp)[:, tp - wlen:])
        outs[6].append(cbuf)
        outs[8].append(ht.reshape(bp, D_RNN))

        xr, gr, q, kvc, kvs, kvw, kvsb, kvwb, ng, mg = _project_in(xs, p["wm"], p["wng"], p["wmg"])
        r3 = lambda a: a.reshape(bs, ts, -1)
        y_rnn, cbuf, ht = _rglru(r3(xr), r3(gr), state_conv[l], state_lru[l].reshape(bs, 1, D_RNN),
                                 *lru, pos0=past)
        ca, cb = _compress_pages(ccmp, pt_flat, p["cmpwt"], l, bs, n_pages)
        oc, sel = _nsa_sample_select(r3(q), ca, cb, past)
        padrows = lambda a: jnp.pad(r3(a), ((0, 0), (0, pad - ts), (0, 0)))
        y_att = _nsa_sample_attend(csel, pt_flat, l, r3(q), r3(ng), oc, sel, padrows(kvsb), cwin, padrows(kvwb),
                                   past)
        xs = _mixer_tail(p, xs, y_rnn, y_att, mg)
        outs[1].append(kv6(kvc, bs, ts))
        outs[3].append(kv6(kvs, bs, ts))
        kw_all = jnp.concatenate([cache_win[l], kv6(kvw, bs, ts)], axis=1)
        outs[5].append(kw_all[:, ts:])
        outs[7].append(cbuf)
        outs[9].append(ht.reshape(bs, D_RNN))

    st = [jnp.stack(o) for o in outs]
    return (xp.reshape(bp, tp, D_MODEL), xs.reshape(bs, ts, D_MODEL),
            st[0], st[1], st[2], st[3], st[4], st[5], st[6], st[7], st[8], st[9])
```

```python
import functools

import jax
import jax.numpy as jnp
import numpy as np
from jax import lax
from jax.experimental import pallas as pl
from jax.experimental.pallas import tpu as pltpu

F32 = jnp.float32
BF16 = jnp.bfloat16
NEG_INF = float("-inf")
MASK_NEG = -1e30

D_MODEL = 1024
D_RNN = 1024
N_RNN_BLOCKS = 16
RNN_BLOCK = D_RNN // N_RNN_BLOCKS
CONV_W = 4
LRU_C = 8.0
N_HEADS = 16
HEAD_DIM = 64
KV_HEADS = 4
GROUP = N_HEADS // KV_HEADS
CMP_LEN = 32
CMP_STRIDE = 16
SEL_BLOCK = 64
N_SELECT = 16
WINDOW = 512
FORCE_BONUS = 1e4
PAGE_SIZE = 128
N_GROUPS = 4
EXP_PER_GROUP = 8
N_EXPERTS = N_GROUPS * EXP_PER_GROUP
D_EXPERT = 256
DEPTH = 2
ALPHA = (2.0 * DEPTH) ** 0.25
LN_EPS = 1e-5
KV_W = 2 * KV_HEADS * HEAD_DIM
QK_SCALE = HEAD_DIM ** -0.5
PER_SEL = SEL_BLOCK // CMP_STRIDE
SLOPES = [2.0 ** (-8.0 * (h + 1) / N_HEADS) for h in range(N_HEADS)]

LANES = 128
VMEM_LIMIT_BYTES = 56 * 1024 * 1024

PROJ_TM = 256
LRU_TC = 256
MERGE_TM = 512
MOE_TM = 1024
MOE_CHUNK = 320
GID_LANE = 64
ATT_QB = 128
ATT_KC = 512
AUX_BLOCK_ROWS = 16
PAGES_PER_STEP = 32
CMP_PAGES_PER_STEP = 16


def _cparams(sem):
    return pltpu.CompilerParams(dimension_semantics=sem, vmem_limit_bytes=VMEM_LIMIT_BYTES)


def _full_spec(shape):
    nd = len(shape)
    return pl.BlockSpec(shape, lambda *_: (0,) * nd)


def _dot(a, b):
    return jnp.dot(a, b, preferred_element_type=F32)


def _dot_nt(a, b):
    return lax.dot_general(a, b, (((1,), (1,)), ((), ())), preferred_element_type=F32)


def _sigmoid(x):
    return 1.0 / (1.0 + jnp.exp(-x))


def _proj_kernel(x_ref, wm_ref, wng_ref, wmg_ref,
                 xr_ref, gr_ref, q_ref, kvc_ref, kvs_ref, kvw_ref, kvsb_ref, kvwb_ref, ng_ref, mg_ref):
    xb = x_ref[...].astype(BF16)
    xr_ref[...] = _dot(xb, wm_ref[:, 0:D_RNN])
    gr_ref[...] = _dot(xb, wm_ref[:, D_RNN:2 * D_RNN])
    q_ref[...] = _dot(xb, wm_ref[:, 2 * D_RNN:3 * D_RNN]).astype(BF16)
    o = 3 * D_RNN
    kvc_ref[...] = _dot(xb, wm_ref[:, o:o + KV_W])
    kvs = _dot(xb, wm_ref[:, o + KV_W:o + 2 * KV_W])
    kvs_ref[...] = kvs
    kvsb_ref[...] = kvs.astype(BF16)
    kvw = _dot(xb, wm_ref[:, o + 2 * KV_W:o + 3 * KV_W])
    kvw_ref[...] = kvw
    kvwb_ref[...] = kvw.astype(BF16)
    ng_ref[...] = _sigmoid(_dot(xb, wng_ref[...]))
    mg_ref[...] = _sigmoid(_dot(xb, wmg_ref[...]))


def _project_in(x2d, wm, wng, wmg):
    n = x2d.shape[0]
    tm = min(PROJ_TM, n)
    row = lambda w: pl.BlockSpec((tm, w), lambda i: (i, 0))
    outs = [(D_RNN, F32), (D_RNN, F32), (N_HEADS * HEAD_DIM, BF16), (KV_W, F32), (KV_W, F32), (KV_W, F32),
            (KV_W, BF16), (KV_W, BF16), (LANES, F32), (2 * D_MODEL, F32)]
    return pl.pallas_call(
        _proj_kernel,
        grid=(n // tm,),
        in_specs=[row(D_MODEL), _full_spec(wm.shape), _full_spec(wng.shape), _full_spec(wmg.shape)],
        out_specs=[row(w) for w, _ in outs],
        out_shape=[jax.ShapeDtypeStruct((n, w), dt) for w, dt in outs],
        compiler_params=_cparams(("arbitrary",)),
        name="project_in",
    )(x2d, wm, wng, wmg)


def _shift_rows(x, s, fill):
    row = lax.broadcasted_iota(jnp.int32, x.shape, 0)
    return jnp.where(row >= s, pltpu.roll(x, s, axis=0), fill)


def _gelu_tanh(x):
    c = np.float32(np.sqrt(2.0 / np.pi))
    return 0.5 * x * (1.0 + jnp.tanh(c * (x + np.float32(0.044715) * (x * x * x))))


def _rglru_kernel(xr_ref, gr_ref, cbuf_ref, h0_ref, cw_ref, cb_ref, wa_ref, ba_ref, wx_ref, bx_ref, lam_ref,
                  y_ref, nbuf_ref, ht_ref, xbuf, hcar, *, tc, pos0):
    c = pl.program_id(1)
    nc = pl.num_programs(1)

    @pl.when(c == 0)
    def _():
        xbuf[0:8, :] = jnp.zeros((8, D_RNN), F32)
        xbuf[8 - (CONV_W - 1):8, :] = cbuf_ref[...]
        hcar[...] = h0_ref[...]

    x = xr_ref[...]
    xbuf[8:8 + tc, :] = x
    conv = cb_ref[...] + cw_ref[CONV_W - 1:CONV_W, :] * x
    for j in range(1, CONV_W):
        conv = conv + cw_ref[CONV_W - 1 - j:CONV_W - j, :] * xbuf[8 - j:8 - j + tc, :]
    nbuf = xbuf[8 + tc - (CONV_W - 1):8 + tc, :]
    xbuf[0:8, :] = xbuf[tc:tc + 8, :]

    cvb = conv.astype(BF16)
    nb4 = D_RNN // 256
    za = jnp.concatenate([_dot(cvb[:, k * 256:(k + 1) * 256], wa_ref[k]) for k in range(nb4)], axis=1)
    zx = jnp.concatenate([_dot(cvb[:, k * 256:(k + 1) * 256], wx_ref[k]) for k in range(nb4)], axis=1)
    r = _sigmoid(za + ba_ref[...])
    i = _sigmoid(zx + bx_ref[...])
    nl = -lam_ref[...]
    softplus = jnp.maximum(nl, 0.0) + jnp.log(1.0 + jnp.exp(-jnp.abs(nl)))
    log_a = (-LRU_C) * r * softplus
    a = jnp.exp(log_a)
    pos = pos0 + c * tc + lax.broadcasted_iota(jnp.int32, (tc, 1), 0)
    mult = jnp.where(pos == 0, 1.0, jnp.sqrt(1.0 - jnp.exp(2.0 * log_a)))
    b = mult * i * conv

    s = 1
    while s < tc:
        a_sh = _shift_rows(a, s, 1.0)
        b_sh = _shift_rows(b, s, 0.0)
        b = a * b_sh + b
        a = a * a_sh
        s *= 2
    h = b + a * hcar[...]
    hcar[...] = h[tc - 1:tc, :]
    y_ref[...] = (_gelu_tanh(gr_ref[...]) * h).astype(BF16)

    @pl.when(c == nc - 1)
    def _():
        nbuf_ref[...] = nbuf
        ht_ref[...] = h[tc - 1:tc, :]


def _rglru(xr, gr, conv_buf, h0, cw, cb, wa_bd, ba, wx_bd, bx, lam, pos0):
    bsz, t, _ = xr.shape
    tc = min(LRU_TC, t)
    seq = pl.BlockSpec((None, tc, D_RNN), lambda b, c: (b, c, 0))
    per_b = lambda r: pl.BlockSpec((None, r, D_RNN), lambda b, c: (b, 0, 0))
    vec = pl.BlockSpec((1, D_RNN), lambda b, c: (0, 0))
    bd = pl.BlockSpec(wa_bd.shape, lambda b, c: (0, 0, 0))
    return pl.pallas_call(
        functools.partial(_rglru_kernel, tc=tc, pos0=pos0),
        grid=(bsz, t // tc),
        in_specs=[seq, seq, per_b(CONV_W - 1), per_b(1),
                  pl.BlockSpec((CONV_W, D_RNN), lambda b, c: (0, 0)), vec, bd, vec, bd, vec, vec],
        out_specs=[seq, per_b(CONV_W - 1), per_b(1)],
        out_shape=[jax.ShapeDtypeStruct((bsz, t, D_RNN), BF16),
                   jax.ShapeDtypeStruct((bsz, CONV_W - 1, D_RNN), F32),
                   jax.ShapeDtypeStruct((bsz, 1, D_RNN), F32)],
        scratch_shapes=[pltpu.VMEM((tc + 8, D_RNN), F32), pltpu.VMEM((1, D_RNN), F32)],
        compiler_params=_cparams(("arbitrary", "arbitrary")),
        name="rglru",
    )(xr, gr, conv_buf, h0, cw, cb, wa_bd, ba, wx_bd, bx, lam)


def _segment_sums(x, w1, w2):
    r = x.shape[0]
    x3 = x.reshape(r // CMP_STRIDE, CMP_STRIDE, KV_W)
    return jnp.sum(x3 * w1[None], axis=1), jnp.sum(x3 * w2[None], axis=1)


def _compress_kernel(kv_ref, w_ref, a_ref, b_ref):
    a, b = _segment_sums(kv_ref[...], w_ref[0:CMP_STRIDE, :], w_ref[CMP_STRIDE:CMP_LEN, :])
    a_ref[...] = a
    b_ref[...] = b


def _compress_rows(kv, w):
    bsz, t, _ = kv.shape
    tr = min(1024, t)
    ns = tr // CMP_STRIDE
    return pl.pallas_call(
        _compress_kernel,
        grid=(bsz, t // tr),
        in_specs=[pl.BlockSpec((None, tr, KV_W), lambda b, c: (b, c, 0)), _full_spec(w.shape)],
        out_specs=[pl.BlockSpec((None, ns, KV_W), lambda b, c: (b, c, 0))] * 2,
        out_shape=[jax.ShapeDtypeStruct((bsz, t // CMP_STRIDE, KV_W), F32)] * 2,
        compiler_params=_cparams(("arbitrary", "arbitrary")),
        name="compress_rows",
    )(kv, w)


def _split_bf16_f32(y):
    hi = y.astype(BF16)
    return hi, y - hi.astype(F32)


def _split_bf16(y):
    hi, rest = _split_bf16_f32(y)
    return hi, rest.astype(BF16)


def _compress_pages_kernel(pt_ref, *refs):
    del pt_ref
    pages = refs[:CMP_PAGES_PER_STEP]
    w_ref, seg_ref, a_ref, b_ref = refs[CMP_PAGES_PER_STEP:]
    seg = seg_ref[...]
    splits = []
    for role in range(2):
        rows = []
        for x in range(2):
            wt = w_ref[role, x]
            for g in range(KV_HEADS):
                rows.append(jnp.concatenate([p[x, g] * wt for p in pages], axis=1))
        splits.append(_split_bf16(jnp.concatenate(rows, axis=0)))
    zs = [_dot(hi, seg) + _dot(lo, seg) for hi, lo in splits]
    for z, o_ref in zip(zs, (a_ref, b_ref)):
        for x in range(2):
            for g in range(KV_HEADS):
                i = x * KV_HEADS + g
                o_ref[x, g] = z[i * HEAD_DIM:(i + 1) * HEAD_DIM, :]


def _page_specs(layer, n_pages, per_step):
    def spec(k):
        return pl.BlockSpec((None, None, 2, KV_HEADS, HEAD_DIM, PAGE_SIZE),
                            lambda b, j, pt: (layer, pt[b * n_pages + j * per_step + k], 0, 0, 0, 0))
    return [spec(k) for k in range(per_step)]


def _compress_pages(cache_t, pt_flat, wt, layer, bsz, n_pages):
    spp = PAGE_SIZE // CMP_STRIDE
    per_step = CMP_PAGES_PER_STEP
    assert per_step * spp == LANES
    steps = n_pages // per_step
    kdim = per_step * PAGE_SIZE
    lrow = lax.broadcasted_iota(jnp.int32, (kdim, LANES), 0)
    lcol = lax.broadcasted_iota(jnp.int32, (kdim, LANES), 1)
    seg = jnp.where(lrow // CMP_STRIDE == lcol, 1.0, 0.0).astype(BF16)
    out_block = pl.BlockSpec((None, 2, KV_HEADS, HEAD_DIM, LANES), lambda b, j, pt: (b, 0, 0, 0, j))
    grid_spec = pltpu.PrefetchScalarGridSpec(
        num_scalar_prefetch=1,
        grid=(bsz, steps),
        in_specs=_page_specs(layer, n_pages, per_step) + [
            pl.BlockSpec(wt.shape, lambda b, j, pt: (0, 0, 0, 0)), pl.BlockSpec(seg.shape, lambda b, j, pt: (0, 0))],
        out_specs=[out_block] * 2,
    )
    return pl.pallas_call(
        _compress_pages_kernel,
        grid_spec=grid_spec,
        out_shape=[jax.ShapeDtypeStruct((bsz, 2, KV_HEADS, HEAD_DIM, n_pages * spp), F32)] * 2,
        compiler_params=_cparams(("arbitrary", "arbitrary")),
        name="compress_pages",
    )(pt_flat, *([cache_t] * per_step), wt, seg)


def _row_slopes(g, n_q):
    r = lax.broadcasted_iota(jnp.int32, (GROUP * n_q, 1), 0) // n_q
    out = jnp.full((GROUP * n_q, 1), np.float32(SLOPES[g * GROUP + GROUP - 1]), F32)
    for k in range(GROUP - 2, -1, -1):
        out = jnp.where(r == k, np.float32(SLOPES[g * GROUP + k]), out)
    return out


def _stack_heads(q, g):
    return jnp.concatenate(
        [q[:, (g * GROUP + r) * HEAD_DIM:(g * GROUP + r + 1) * HEAD_DIM] for r in range(GROUP)], axis=0)


def _masked_softmax_rows(s, valid):
    s = jnp.where(valid, s, NEG_INF)
    m = jnp.max(s, axis=-1, keepdims=True)
    m = jnp.where(m == NEG_INF, 0.0, m)
    p = jnp.exp(s - m)
    return p / jnp.maximum(jnp.sum(p, axis=-1, keepdims=True), 1e-30)


def _flash_step(carry, s, valid, v, v_transposed=False):
    m, l, acc = carry
    s = jnp.where(valid, s, NEG_INF)
    m_new = jnp.maximum(m, jnp.max(s, axis=-1, keepdims=True))
    m_safe = jnp.where(m_new == NEG_INF, 0.0, m_new)
    alpha = jnp.exp(m - m_safe)
    p = jnp.exp(s - m_safe)
    l = alpha * l + jnp.sum(p, axis=-1, keepdims=True)
    pv = _dot_nt(p.astype(BF16), v) if v_transposed else _dot(p.astype(BF16), v)
    return m_new, l, alpha * acc + pv


def _flash_init(rows):
    return (jnp.full((rows, 1), NEG_INF, F32), jnp.zeros((rows, 1), F32), jnp.zeros((rows, HEAD_DIM), F32))


def _flash_out(carry):
    _, l, acc = carry
    return acc / jnp.maximum(l, 1e-30)


def _select_blocks(score, n_extra_above=None):
    n_blk = score.shape[1]
    lane = lax.broadcasted_iota(jnp.int32, score.shape, 1)
    cnt = jnp.zeros(score.shape, jnp.int32) if n_extra_above is None else n_extra_above
    for k in range(n_blk):
        col = score[:, k:k + 1]
        cnt = cnt + jnp.where(lane > k, jnp.where(col >= score, 1, 0), jnp.where(col > score, 1, 0))
    return cnt < N_SELECT


def _block_sum_matrix(n_cmp, n_blk):
    c = lax.broadcasted_iota(jnp.int32, (n_cmp, n_blk), 0)
    j = lax.broadcasted_iota(jnp.int32, (n_cmp, n_blk), 1)
    return jnp.where(c // PER_SEL == j, 1.0, 0.0).astype(F32)


def _expand_matrix(n_blk, blk0, k0, kc):
    j = lax.broadcasted_iota(jnp.int32, (n_blk, kc), 0) + blk0
    c = lax.broadcasted_iota(jnp.int32, (n_blk, kc), 1) + k0
    return jnp.where(c // SEL_BLOCK == j, 1.0, 0.0).astype(BF16)


def _sum_heads(p, n_q):
    out = p[0:n_q]
    for r in range(1, GROUP):
        out = out + p[r * n_q:(r + 1) * n_q]
    return out


def _tile_rows(x):
    return jnp.concatenate([x] * GROUP, axis=0)


def _gate_and_pack(ng, branches, n_q):
    pieces = []
    for g in range(KV_HEADS):
        for r in range(GROUP):
            idx = g * GROUP + r
            acc = None
            for br, o in enumerate(branches[g]):
                term = ng[:, br * N_HEADS + idx:br * N_HEADS + idx + 1] * o[r * n_q:(r + 1) * n_q]
                acc = term if acc is None else acc + term
            pieces.append(acc)
    return jnp.concatenate(pieces, axis=1)


def _select_blocks_t(score_t):
    n_blk = score_t.shape[0]
    slab = 8
    slabs = [score_t[i:i + slab] for i in range(0, n_blk, slab)]
    cnts = [jnp.zeros(s.shape, jnp.int32) for s in slabs]
    row = lax.broadcasted_iota(jnp.int32, slabs[0].shape, 0)
    for k in range(n_blk):
        pivot = score_t[k:k + 1, :]
        for i, s in enumerate(slabs):
            lo = i * slab
            if lo > k:
                beats = jnp.where(pivot >= s, 1, 0)
            elif lo + slab - 1 < k:
                beats = jnp.where(pivot > s, 1, 0)
            else:
                beats = jnp.where(row + lo > k, jnp.where(pivot >= s, 1, 0), jnp.where(pivot > s, 1, 0))
            cnts[i] = cnts[i] + beats
    cnt = jnp.concatenate(cnts, axis=0)
    return jnp.where(cnt < N_SELECT, 1.0, 0.0)


def _project_t_kernel(x_ref, wt_ref, qt_ref, vst_ref, vwt_ref, ngt_ref):
    xb = x_ref[...].astype(BF16)
    nq = N_HEADS * HEAD_DIM
    nv = KV_W // 2
    qt_ref[...] = (_dot_nt(wt_ref[0:nq, :], xb) * QK_SCALE).astype(BF16)
    vst_ref[...] = _dot_nt(wt_ref[nq:nq + nv, :], xb).astype(BF16)
    vwt_ref[...] = _dot_nt(wt_ref[nq + nv:nq + 2 * nv, :], xb).astype(BF16)
    ngt_ref[...] = _sigmoid(_dot_nt(wt_ref[nq + 2 * nv:nq + 2 * nv + LANES, :], xb))


def _project_t(x2d, wt):
    n = x2d.shape[0]
    tm = min(PROJ_TM, n)
    outs = [(N_HEADS * HEAD_DIM, BF16), (KV_W // 2, BF16), (KV_W // 2, BF16), (LANES, F32)]
    return pl.pallas_call(
        _project_t_kernel,
        grid=(n // tm,),
        in_specs=[pl.BlockSpec((tm, D_MODEL), lambda i: (i, 0)), _full_spec(wt.shape)],
        out_specs=[pl.BlockSpec((r, tm), lambda i: (0, i)) for r, _ in outs],
        out_shape=[jax.ShapeDtypeStruct((r, n), dt) for r, dt in outs],
        compiler_params=_cparams(("arbitrary",)),
        name="project_t",
    )(x2d, wt)


def _exp_cols(s):
    m = jnp.max(s, axis=0, keepdims=True)
    m = jnp.where(m == NEG_INF, 0.0, m)
    e = jnp.exp(s - m)
    return e, 1.0 / jnp.maximum(jnp.sum(e, axis=0, keepdims=True), 1e-30)


def _add_head_bias_t(s, table_ref, g, mask, n_q):
    return jnp.concatenate(
        [s[:, r * n_q:(r + 1) * n_q] + (table_ref[g * GROUP + r] + mask) for r in range(GROUP)], axis=1)


def _lane_slopes(g, n_q):
    r = lax.broadcasted_iota(jnp.int32, (1, GROUP * n_q), 1) // n_q
    out = jnp.full((1, GROUP * n_q), np.float32(SLOPES[g * GROUP + GROUP - 1]), F32)
    for k in range(GROUP - 2, -1, -1):
        out = jnp.where(r == k, np.float32(SLOPES[g * GROUP + k]), out)
    return out


def _nsa_prompt_t_kernel(qt_ref, ngt_ref, ca_ref, cb_ref, ks_ref, kw_ref, vst_ref, vwt_ref, o_ref,
                         aux_s, bwin, bcmp, rhs_s, seln_s, oc_s, ow_s, m_s, l_s, acc_s, *, t, qb, kc, wk):
    n_seg = ca_ref.shape[0]
    n_cmp_valid = (t - CMP_LEN) // CMP_STRIDE + 1
    n_blk = t // SEL_BLOCK
    nv = KV_W // 2
    ql = lax.broadcasted_iota(jnp.int32, (1, qb), 1)

    def krow(n):
        return lax.broadcasted_iota(jnp.int32, (n, 1), 0)

    @pl.when((pl.program_id(0) == 0) & (pl.program_id(1) == 0))
    def _():
        d_win = (ql - krow(wk)).astype(F32)
        d_cmp = (ql - (krow(n_seg) * CMP_STRIDE + (CMP_LEN - 1))).astype(F32)
        for h in range(N_HEADS):
            slope = np.float32(SLOPES[h])
            bwin[h] = -(slope * d_win)
            bcmp[h] = -(slope * d_cmp)
        c = lax.broadcasted_iota(jnp.int32, (kc, LANES), 0)
        ln = lax.broadcasted_iota(jnp.int32, (kc, LANES), 1)
        feat = jnp.where(ln < AUX_BLOCK_ROWS, jnp.where(c // SEL_BLOCK == ln, 1, 0),
                         jnp.where((ln >= AUX_BLOCK_ROWS) & (ln < AUX_BLOCK_ROWS + 3), (c // 16) * 16,
                                   jnp.where((ln >= AUX_BLOCK_ROWS + 3) & (ln < AUX_BLOCK_ROWS + 6), c % 16, 0)))
        aux_s[...] = feat.astype(F32).astype(BF16)
        for g in range(KV_HEADS):
            slope = _lane_slopes(g, qb)
            s1, r1 = _split_bf16_f32(slope)
            s2, r2 = _split_bf16_f32(r1)
            s3 = r2.astype(BF16)
            rhs_s[g, 0:AUX_BLOCK_ROWS, :] = jnp.zeros((AUX_BLOCK_ROWS, GROUP * qb), BF16)
            rhs_s[g, AUX_BLOCK_ROWS:2 * AUX_BLOCK_ROWS, :] = jnp.concatenate(
                [s1, s2, s3, s1, s2, s3, jnp.zeros((AUX_BLOCK_ROWS - 6, GROUP * qb), BF16)], axis=0)
            rhs_s[g, 2 * AUX_BLOCK_ROWS:LANES, :] = jnp.zeros((LANES - 2 * AUX_BLOCK_ROWS, GROUP * qb), BF16)

    s0 = pl.program_id(1) * qb

    def q_t(g):
        return jnp.concatenate(
            [qt_ref[(g * GROUP + r) * HEAD_DIM:(g * GROUP + r + 1) * HEAD_DIM, :] for r in range(GROUP)], axis=1)

    ckv = ca_ref[...] + pltpu.roll(cb_ref[...], n_seg - 1, axis=0)
    ck = ckv[:, 0:nv].astype(BF16)
    cv_t = ckv[:, nv:2 * nv].T.astype(BF16)

    cc = krow(n_seg)
    cmask = jnp.where((s0 + ql - (cc * CMP_STRIDE + (CMP_LEN - 1)) >= 0) & (cc < n_cmp_valid), 0.0, NEG_INF)
    kstart = pl.multiple_of(jnp.clip(s0 + qb - wk, 0, t - wk), qb)
    wd = (s0 - kstart) + ql - krow(wk)
    wmask = jnp.where((wd >= 0) & (wd < WINDOW), 0.0, NEG_INF)

    gj = lax.broadcasted_iota(jnp.int32, (n_blk, n_seg), 0)
    gc = lax.broadcasted_iota(jnp.int32, (n_blk, n_seg), 1)
    gsum_t = jnp.where(gc // PER_SEL == gj, 1.0, 0.0).astype(BF16)
    blk_t = krow(n_blk)
    cur_t = (s0 + ql) // SEL_BLOCK
    forced_t = (blk_t == 0) | (blk_t == cur_t) | (blk_t == cur_t - 1)

    heads = range(KV_HEADS)
    hsl = [slice(g * HEAD_DIM, (g + 1) * HEAD_DIM) for g in heads]
    qgs = [q_t(g) for g in heads]
    for g in heads:
        rhs_s[g, LANES:LANES + HEAD_DIM, :] = qgs[g]
    s_c = [_add_head_bias_t(_dot(ck[:, hsl[g]], qgs[g]), bcmp, g, cmask, qb) for g in heads]
    s_w = [_add_head_bias_t(_dot(kw_ref[pl.ds(kstart, wk), hsl[g]], qgs[g]), bwin, g, wmask, qb) for g in heads]
    ex_c = [_exp_cols(s) for s in s_c]
    ex_w = [_exp_cols(s) for s in s_w]
    imps = []
    for g in heads:
        e_hi, e_lo = _split_bf16(ex_c[g][0])
        oc_s[g] = _dot(cv_t[hsl[g], :], e_hi) * ex_c[g][1]
        imps.append((_dot(gsum_t, e_hi) + _dot(gsum_t, e_lo)) * ex_c[g][1])
        ow_s[g] = _dot(vwt_ref[hsl[g], pl.ds(kstart, wk)], ex_w[g][0].astype(BF16)) * ex_w[g][1]
    for g in heads:
        imp_t = imps[g][:, 0:qb]
        for r in range(1, GROUP):
            imp_t = imp_t + imps[g][:, r * qb:(r + 1) * qb]
        score_t = jnp.where(forced_t, FORCE_BONUS, imp_t)
        score_t = jnp.where(blk_t <= cur_t, score_t, NEG_INF)
        sel_t = jnp.where(blk_t <= cur_t, _select_blocks_t(score_t), 0.0)
        seln_s[g] = jnp.concatenate([jnp.where(sel_t > 0.5, 0.0, MASK_NEG)] * GROUP, axis=1)

    m_s[...] = jnp.full(m_s.shape, NEG_INF, F32)
    l_s[...] = jnp.zeros(l_s.shape, F32)
    acc_s[...] = jnp.zeros(acc_s.shape, F32)

    bpc = kc // SEL_BLOCK
    ql4 = jnp.concatenate([ql] * GROUP, axis=1)

    def chunk(c, diagonal):
        k0 = pl.multiple_of(c * kc, kc)
        delta = jnp.zeros((1, 1), jnp.int32) + (s0 - k0)
        qoff = (delta + ql4).astype(F32)
        if diagonal:
            causal = (delta + ql4 - krow(kc)) >= 0
        aux = aux_s[...]
        state = [(m_s[g], l_s[g], acc_s[g]) for g in range(KV_HEADS)]
        scores = []
        for g in range(KV_HEADS):
            blocks = seln_s[g, pl.ds(pl.multiple_of(c * bpc, bpc), bpc), :]
            rhs_s[g, 0:AUX_BLOCK_ROWS, :] = jnp.concatenate(
                [blocks, jnp.zeros((AUX_BLOCK_ROWS - bpc, GROUP * qb), F32)], axis=0).astype(BF16)
        for g in range(KV_HEADS):
            lhs = jnp.concatenate([aux, ks_ref[pl.ds(k0, kc), g * HEAD_DIM:(g + 1) * HEAD_DIM]], axis=1)
            s = _dot(lhs, rhs_s[g])
            scores.append(jnp.where(causal, s, MASK_NEG) if diagonal else s)
        probs, stats = [], []
        for g in range(KV_HEADS):
            m_prev, l_prev, _ = state[g]
            off = _lane_slopes(g, qb) * qoff
            m_new = jnp.maximum(m_prev, jnp.max(scores[g], axis=0, keepdims=True) - off)
            m_safe = jnp.where(m_new == NEG_INF, 0.0, m_new)
            alpha = jnp.exp(m_prev - m_safe)
            p = jnp.exp(scores[g] - (m_safe + off))
            stats.append((m_new, alpha * l_prev + jnp.sum(p, axis=0, keepdims=True), alpha))
            probs.append(p.astype(BF16))
        for g in range(KV_HEADS):
            m_new, l_new, alpha = stats[g]
            pv = _dot(vst_ref[g * HEAD_DIM:(g + 1) * HEAD_DIM, pl.ds(k0, kc)], probs[g])
            m_s[g] = m_new
            l_s[g] = l_new
            acc_s[g] = alpha * state[g][2] + pv

    def body(c, carry):
        chunk(c, False)
        return carry

    last = (s0 + qb + kc - 1) // kc - 1
    lax.fori_loop(0, last, body, 0)
    chunk(last, True)

    ngt = ngt_ref[...]
    pieces = []
    for g in range(KV_HEADS):
        o_s = acc_s[g] / jnp.maximum(l_s[g], 1e-30)
        o_c = oc_s[g]
        o_w = ow_s[g]
        for r in range(GROUP):
            h = g * GROUP + r
            sl = slice(r * qb, (r + 1) * qb)
            pieces.append(ngt[h:h + 1, :] * o_c[:, sl] + ngt[N_HEADS + h:N_HEADS + h + 1, :] * o_s[:, sl]
                          + ngt[2 * N_HEADS + h:2 * N_HEADS + h + 1, :] * o_w[:, sl])
    o_ref[...] = jnp.concatenate(pieces, axis=0).T.astype(BF16)


def _nsa_prompt_t(qt, ngt, ca, cb, ksb, kwb, vst, vwt, bsz, t):
    qb = min(ATT_QB, t)
    kc = min(ATT_KC, t)
    wk = min(WINDOW + qb, t)
    n_seg = ca.shape[1]
    n_blk = t // SEL_BLOCK
    nq = t // qb
    cols = GROUP * qb
    per_b = lambda r, w: pl.BlockSpec((None, r, w), lambda b, i: (b, 0, 0))
    tok = lambda r: pl.BlockSpec((r, qb), lambda b, i: (0, b * nq + i))
    seq = lambda r: pl.BlockSpec((r, t), lambda b, i: (0, b))
    return pl.pallas_call(
        functools.partial(_nsa_prompt_t_kernel, t=t, qb=qb, kc=kc, wk=wk),
        grid=(bsz, nq),
        in_specs=[tok(N_HEADS * HEAD_DIM), tok(LANES), per_b(n_seg, KV_W), per_b(n_seg, KV_W),
                  per_b(t, KV_W), per_b(t, KV_W), seq(KV_W // 2), seq(KV_W // 2)],
        out_specs=pl.BlockSpec((None, qb, N_HEADS * HEAD_DIM), lambda b, i: (b, i, 0)),
        out_shape=jax.ShapeDtypeStruct((bsz, t, N_HEADS * HEAD_DIM), BF16),
        scratch_shapes=[pltpu.VMEM((kc, LANES), BF16), pltpu.VMEM((N_HEADS, wk, qb), F32),
                        pltpu.VMEM((N_HEADS, n_seg, qb), F32), pltpu.VMEM((KV_HEADS, LANES + HEAD_DIM, cols), BF16),
                        pltpu.VMEM((KV_HEADS, n_blk, cols), F32),
                        pltpu.VMEM((KV_HEADS, HEAD_DIM, cols), F32), pltpu.VMEM((KV_HEADS, HEAD_DIM, cols), F32),
                        pltpu.VMEM((KV_HEADS, 1, cols), F32), pltpu.VMEM((KV_HEADS, 1, cols), F32),
                        pltpu.VMEM((KV_HEADS, HEAD_DIM, cols), F32)],
        compiler_params=_cparams(("arbitrary", "arbitrary")),
        name="nsa_prompt",
    )(qt, ngt, ca, cb, ksb, kwb, vst, vwt)


def _nsa_sample_select_kernel(q_ref, ca_ref, cb_ref, oc_ref, sel_ref, *, ts, past):
    n_seg = ca_ref.shape[-1]
    n_cmp_valid = n_seg - 1
    n_pblk = past // SEL_BLOCK
    qrow = past + lax.broadcasted_iota(jnp.int32, (ts, 1), 0)
    qpos = _tile_rows(qrow)
    q = q_ref[...]
    gsum = _block_sum_matrix(n_seg, n_pblk)
    blk = lax.broadcasted_iota(jnp.int32, (1, n_pblk), 1)
    cur = qrow // SEL_BLOCK
    forced = (blk == 0) | (blk == cur) | (blk == cur - 1)
    col = lax.broadcasted_iota(jnp.int32, (1, n_seg), 1)
    d = qpos - (col * CMP_STRIDE + (CMP_LEN - 1))
    valid = (d >= 0) & (col < n_cmp_valid)
    df = d.astype(F32)
    outs = []
    for g in range(KV_HEADS):
        qg = _stack_heads(q, g)
        ck_t = (ca_ref[0, g] + pltpu.roll(cb_ref[0, g], n_seg - 1, axis=1)).astype(BF16)
        cv_t = (ca_ref[1, g] + pltpu.roll(cb_ref[1, g], n_seg - 1, axis=1)).astype(BF16)
        s = _dot(qg, ck_t) * QK_SCALE - _row_slopes(g, ts) * df
        p_c = _masked_softmax_rows(s, valid)
        outs.append(_dot_nt(p_c.astype(BF16), cv_t))
        imp = jnp.dot(_sum_heads(p_c, ts), gsum, preferred_element_type=F32, precision=lax.Precision.HIGHEST)
        score = jnp.where(forced, FORCE_BONUS, imp)
        above = jnp.where(FORCE_BONUS > score, 1, 0)
        sel_ref[g * ts:(g + 1) * ts, :] = jnp.where(_select_blocks(score, above), 1.0, 0.0)
    for g in range(KV_HEADS):
        oc_ref[g * GROUP * ts:(g + 1) * GROUP * ts, :] = outs[g]


def _nsa_sample_select(q, ca, cb, past):
    bsz, ts, _ = q.shape
    n_seg = ca.shape[-1]
    n_pblk = past // SEL_BLOCK
    per_b = lambda r, w: pl.BlockSpec((None, r, w), lambda b: (b, 0, 0))
    seg_t = pl.BlockSpec((None, 2, KV_HEADS, HEAD_DIM, n_seg), lambda b: (b, 0, 0, 0, 0))
    return pl.pallas_call(
        functools.partial(_nsa_sample_select_kernel, ts=ts, past=past),
        grid=(bsz,),
        in_specs=[per_b(ts, N_HEADS * HEAD_DIM), seg_t, seg_t],
        out_specs=[per_b(N_HEADS * ts, HEAD_DIM), per_b(KV_HEADS * ts, n_pblk)],
        out_shape=[jax.ShapeDtypeStruct((bsz, N_HEADS * ts, HEAD_DIM), F32),
                   jax.ShapeDtypeStruct((bsz, KV_HEADS * ts, n_pblk), F32)],
        compiler_params=_cparams(("arbitrary",)),
        name="nsa_sample_select",
    )(q, ca, cb)


def _nsa_sample_attend_kernel(pt_ref, *refs, ts, past, pad, n_pg):
    del pt_ref
    pages = refs[:n_pg]
    (q_ref, ng_ref, oc_ref, sel_ref, tail_ref, wbuf_ref, wnew_ref, o_ref, m_s, l_s, acc_s) = refs[n_pg:]
    j = pl.program_id(1)
    nj = pl.num_programs(1)
    rows = GROUP * ts
    kc = n_pg * PAGE_SIZE
    bps = kc // SEL_BLOCK
    qrow = past + lax.broadcasted_iota(jnp.int32, (ts, 1), 0)
    qpos = _tile_rows(qrow)
    q = q_ref[...]

    @pl.when(j == 0)
    def _():
        m_s[...] = jnp.full(m_s.shape, NEG_INF, F32)
        l_s[...] = jnp.zeros(l_s.shape, F32)
        acc_s[...] = jnp.zeros(acc_s.shape, F32)

    k0 = j * kc
    d = qpos - (k0 + lax.broadcasted_iota(jnp.int32, (1, kc), 1))
    df = d.astype(F32)
    expand = _expand_matrix(bps, 0, 0, kc)
    heads = range(KV_HEADS)
    state = [(m_s[g], l_s[g], acc_s[g]) for g in heads]
    scores = []
    for g in heads:
        k_t = jnp.concatenate([p[0, g] for p in pages], axis=1).astype(BF16)
        s = _dot(_stack_heads(q, g), k_t) * QK_SCALE - _row_slopes(g, ts) * df
        selk = _tile_rows(_dot(sel_ref[g * ts:(g + 1) * ts, :].astype(BF16), expand))
        scores.append(jnp.where((d >= 0) & (selk > 0.5), s, NEG_INF))
    m_new = [jnp.maximum(state[g][0], jnp.max(scores[g], axis=-1, keepdims=True)) for g in heads]
    m_safe = [jnp.where(m == NEG_INF, 0.0, m) for m in m_new]
    probs = [jnp.exp(scores[g] - m_safe[g]) for g in heads]
    alpha = [jnp.exp(state[g][0] - m_safe[g]) for g in heads]
    l_new = [alpha[g] * state[g][1] + jnp.sum(probs[g], axis=-1, keepdims=True) for g in heads]
    for g in heads:
        v_t = jnp.concatenate([p[1, g] for p in pages], axis=1).astype(BF16)
        acc_s[g] = alpha[g] * state[g][2] + _dot_nt(probs[g].astype(BF16), v_t)
        m_s[g] = m_new[g]
        l_s[g] = l_new[g]

    @pl.when(j == nj - 1)
    def _():
        ng = ng_ref[...]
        tail = tail_ref[...]
        wnew = wnew_ref[...]
        wb = wbuf_ref.shape[-1]
        colp = lax.broadcasted_iota(jnp.int32, (1, pad), 1)
        d_new = qpos - (past + colp)
        ok_new = (d_new >= 0) & (colp < ts)
        d_old = qpos - (past - wb + lax.broadcasted_iota(jnp.int32, (1, wb), 1))
        ok_old = (d_old >= 0) & (d_old < WINDOW)
        branches = []
        for g in range(KV_HEADS):
            qg = _stack_heads(q, g)
            slope = _row_slopes(g, ts)
            ksl = slice(g * HEAD_DIM, (g + 1) * HEAD_DIM)
            vsl = slice(KV_W // 2 + g * HEAD_DIM, KV_W // 2 + (g + 1) * HEAD_DIM)
            sl = slice(g * rows, (g + 1) * rows)
            s = _dot_nt(qg, tail[:, ksl]) * QK_SCALE - slope * d_new.astype(F32)
            o_s = _flash_out(_flash_step((m_s[g], l_s[g], acc_s[g]), s, ok_new, tail[:, vsl]))
            s = _dot(qg, wbuf_ref[0, g].astype(BF16)) * QK_SCALE - slope * d_old.astype(F32)
            carry = _flash_step(_flash_init(rows), s, ok_old, wbuf_ref[1, g].astype(BF16), v_transposed=True)
            s = _dot_nt(qg, wnew[:, ksl]) * QK_SCALE - slope * d_new.astype(F32)
            o_w = _flash_out(_flash_step(carry, s, ok_new & (d_new < WINDOW), wnew[:, vsl]))
            branches.append([oc_ref[sl, :], o_s, o_w])
        o_ref[...] = _gate_and_pack(ng, branches, ts).astype(BF16)


def _nsa_sample_attend(cache_t, pt_flat, layer, q, ng, oc, sel, tail, wbuf_t, wnew, past):
    bsz, ts, _ = q.shape
    n_pages = past // PAGE_SIZE
    n_pg = min(PAGES_PER_STEP, n_pages)
    steps = n_pages // n_pg
    pad = tail.shape[1]
    wb = wbuf_t.shape[-1]
    bps = n_pg * PAGE_SIZE // SEL_BLOCK
    sel4 = sel.reshape(bsz, KV_HEADS * ts, steps, bps).transpose(0, 2, 1, 3)
    per_b = lambda r, w: pl.BlockSpec((None, r, w), lambda b, j, pt: (b, 0, 0))
    grid_spec = pltpu.PrefetchScalarGridSpec(
        num_scalar_prefetch=1,
        grid=(bsz, steps),
        in_specs=_page_specs(layer, n_pages, n_pg) + [
            per_b(ts, N_HEADS * HEAD_DIM), per_b(ts, LANES), per_b(N_HEADS * ts, HEAD_DIM),
            pl.BlockSpec((None, None, KV_HEADS * ts, bps), lambda b, j, pt: (b, j, 0, 0)),
            per_b(pad, KV_W),
            pl.BlockSpec((None, None, 2, KV_HEADS, HEAD_DIM, wb), lambda b, j, pt: (layer, b, 0, 0, 0, 0)),
            per_b(pad, KV_W)],
        out_specs=per_b(ts, N_HEADS * HEAD_DIM),
        scratch_shapes=[pltpu.VMEM((KV_HEADS, GROUP * ts, 1), F32), pltpu.VMEM((KV_HEADS, GROUP * ts, 1), F32),
                        pltpu.VMEM((KV_HEADS, GROUP * ts, HEAD_DIM), F32)],
    )
    return pl.pallas_call(
        functools.partial(_nsa_sample_attend_kernel, ts=ts, past=past, pad=pad, n_pg=n_pg),
        grid_spec=grid_spec,
        out_shape=jax.ShapeDtypeStruct((bsz, ts, N_HEADS * HEAD_DIM), BF16),
        compiler_params=_cparams(("arbitrary", "arbitrary")),
        name="nsa_sample_attend",
    )(pt_flat, *([cache_t] * n_pg), q, ng, oc, sel4, tail, wbuf_t, wnew)


def _layer_norm(x, g, b):
    mu = jnp.mean(x, axis=-1, keepdims=True)
    xc = x - mu
    var = jnp.mean(xc * xc, axis=-1, keepdims=True)
    return xc * lax.rsqrt(var + LN_EPS) * g + b


def _merge_kernel(x_ref, yr_ref, ya_ref, mg_ref, wa_ref, wb_ref, wo_ref, g_ref, b_ref, wr_ref, br_ref,
                  o_ref, ob_ref, comb_ref):
    a = _dot(yr_ref[...], wa_ref[...])
    b = _dot(ya_ref[...], wb_ref[...])
    merged = mg_ref[:, 0:D_MODEL] * a + mg_ref[:, D_MODEL:2 * D_MODEL] * b
    mix = _dot(merged.astype(BF16), wo_ref[...])
    x1 = _layer_norm(ALPHA * x_ref[...] + mix, g_ref[...], b_ref[...])
    o_ref[...] = x1
    x1b = x1.astype(BF16)
    ob_ref[...] = x1b
    comb_ref[...] = _route(_dot(x1b, wr_ref[...]) + br_ref[...])


def _route(logits):
    n = logits.shape[0]
    lane = lax.broadcasted_iota(jnp.int32, (n, LANES), 1)
    is_g = lane < N_GROUPS
    gl = jnp.where(is_g, logits, NEG_INF)
    gmax = jnp.max(gl, axis=-1, keepdims=True)
    grp = jnp.min(jnp.where(gl == gmax, lane, LANES), axis=-1, keepdims=True)
    gw = 1.0 / jnp.sum(jnp.exp(gl - gmax), axis=-1, keepdims=True)
    e_idx = lane - N_GROUPS
    in_grp = (e_idx >= grp * EXP_PER_GROUP) & (e_idx < (grp + 1) * EXP_PER_GROUP)
    el = jnp.where(in_grp, logits, NEG_INF)
    v1 = jnp.max(el, axis=-1, keepdims=True)
    i1 = jnp.min(jnp.where(el == v1, lane, LANES), axis=-1, keepdims=True)
    el2 = jnp.where(lane == i1, NEG_INF, el)
    v2 = jnp.max(el2, axis=-1, keepdims=True)
    i2 = jnp.min(jnp.where(el2 == v2, lane, LANES), axis=-1, keepdims=True)
    e2 = jnp.exp(v2 - v1)
    w1 = gw / (1.0 + e2)
    w2 = gw * e2 / (1.0 + e2)
    comb = jnp.where(lane == i1, w1, 0.0) + jnp.where(lane == i2, w2, 0.0)
    comb = pltpu.roll(comb, LANES - N_GROUPS, axis=1)
    return comb + jnp.where(lane == GID_LANE, grp.astype(F32), 0.0)


def _merge_out(x2d, yr, ya, mg, wa, wb, wo, g, b, wr, br):
    n = x2d.shape[0]
    tm = min(MERGE_TM, n)
    row = lambda w: pl.BlockSpec((tm, w), lambda i: (i, 0))
    sq = _full_spec((D_MODEL, D_MODEL))
    vec = _full_spec((1, D_MODEL))
    return pl.pallas_call(
        _merge_kernel,
        grid=(n // tm,),
        in_specs=[row(D_MODEL), row(D_RNN), row(N_HEADS * HEAD_DIM), row(2 * D_MODEL), sq, sq, sq, vec, vec,
                  _full_spec((D_MODEL, LANES)), _full_spec((1, LANES))],
        out_specs=[row(D_MODEL), row(D_MODEL), row(LANES)],
        out_shape=[jax.ShapeDtypeStruct((n, D_MODEL), F32), jax.ShapeDtypeStruct((n, D_MODEL), BF16),
                   jax.ShapeDtypeStruct((n, LANES), F32)],
        compiler_params=_cparams(("arbitrary",)),
        name="merge_out",
    )(x2d, yr, ya, mg, wa, wb, wo, g, b, wr, br)


def _moe_grouped_kernel(x_ref, xb_ref, comb_ref, w1_ref, w3_ref, w2_ref, g_ref, b_ref, o_ref, acc, ltri, *, tm, ch):
    grp = pl.program_id(1)
    n_grp = pl.num_programs(1)

    @pl.when((pl.program_id(0) == 0) & (grp == 0))
    def _():
        r = lax.broadcasted_iota(jnp.int32, (tm, tm), 0)
        c = lax.broadcasted_iota(jnp.int32, (tm, tm), 1)
        ltri[...] = jnp.where(c < r, 1.0, 0.0).astype(BF16)

    @pl.when(grp == 0)
    def _():
        acc[...] = jnp.zeros(acc.shape, F32)

    comb = comb_ref[...]
    lane = lax.broadcasted_iota(jnp.int32, (tm, LANES), 1)
    gid = jnp.sum(jnp.where(lane == GID_LANE, comb, 0.0), axis=-1, keepdims=True).astype(jnp.int32)
    member = gid == grp
    memf = jnp.where(member, 1.0, 0.0)
    rank = _dot(ltri[...], jnp.broadcast_to(memf, (tm, LANES)).astype(BF16))[:, 0:1]
    n_members = jnp.sum(memf).astype(jnp.int32)
    rows_t = jnp.where(lane == 0, rank, jnp.where(lane == 1, memf, 0.0)).T
    rank_row = rows_t[0:1, :]
    mem_row = rows_t[1:2, :] > 0.5
    xb = xb_ref[...]
    c_hi, c_rest = _split_bf16_f32(comb)
    c_mid, c_lo = _split_bf16(c_rest)
    lane_ch = lax.broadcasted_iota(jnp.int32, (ch, LANES), 1)

    def body(k, carry):
        base = (jnp.zeros((1, 1), jnp.int32) + k * ch).astype(F32)
        slot_col = lax.broadcasted_iota(jnp.int32, (ch, 1), 0).astype(F32) + base
        gather = jnp.where(mem_row & (rank_row == slot_col), 1.0, 0.0).astype(BF16)
        xg = _dot(gather, xb).astype(BF16)
        cwg = _dot(gather, c_hi) + _dot(gather, c_mid) + _dot(gather, c_lo)
        y = jnp.zeros((ch, D_MODEL), F32)
        for e in range(EXP_PER_GROUP):
            cw = jnp.sum(jnp.where(lane_ch == grp * EXP_PER_GROUP + e, cwg, 0.0), axis=-1, keepdims=True)
            h1 = _dot(xg, w1_ref[e])
            h = (h1 * _sigmoid(h1)) * _dot(xg, w3_ref[e]) * cw
            y = y + _dot(h.astype(BF16), w2_ref[e])
        slot_row = lax.broadcasted_iota(jnp.int32, (1, ch), 1).astype(F32) + base
        scatter = jnp.where(member & (rank == slot_row), 1.0, 0.0).astype(BF16)
        y_hi, y_lo = _split_bf16(y)
        acc[...] += _dot(scatter, y_hi) + _dot(scatter, y_lo)
        return carry

    lax.fori_loop(0, (n_members + ch - 1) // ch, body, 0)

    @pl.when(grp == n_grp - 1)
    def _():
        o_ref[...] = _layer_norm(ALPHA * x_ref[...] + acc[...], g_ref[...], b_ref[...])


def _moe_grouped(x1, x1b, comb, w1, w3, w2, g, b):
    n = x1.shape[0]
    tm = min(MOE_TM, n)
    ch = min(MOE_CHUNK, tm)
    row = lambda w: pl.BlockSpec((tm, w), lambda i, e: (i, 0))
    vec = pl.BlockSpec((1, D_MODEL), lambda i, e: (0, 0))
    wspec = lambda a, c: pl.BlockSpec((EXP_PER_GROUP, a, c), lambda i, e: (e, 0, 0), pipeline_mode=pl.Buffered(1))
    return pl.pallas_call(
        functools.partial(_moe_grouped_kernel, tm=tm, ch=ch),
        grid=(n // tm, N_GROUPS),
        in_specs=[row(D_MODEL), row(D_MODEL), row(LANES),
                  wspec(D_MODEL, D_EXPERT), wspec(D_MODEL, D_EXPERT), wspec(D_EXPERT, D_MODEL), vec, vec],
        out_specs=row(D_MODEL),
        out_shape=jax.ShapeDtypeStruct((n, D_MODEL), F32),
        scratch_shapes=[pltpu.VMEM((tm, D_MODEL), F32), pltpu.VMEM((tm, tm), BF16)],
        compiler_params=_cparams(("arbitrary", "arbitrary")),
        name="moe_grouped",
    )(x1, x1b, comb, w1, w3, w2, g, b)


def _block_diag4(w):
    w4 = w.reshape(N_RNN_BLOCKS // 4, 4, RNN_BLOCK, RNN_BLOCK)
    eye = jnp.eye(4, dtype=w.dtype)
    return jnp.einsum("kaij,ab->kaibj", w4, eye).reshape(N_RNN_BLOCKS // 4, 256, 256).astype(BF16)


def _layer_params(l, w_in, conv_w, conv_b, lru_wa, lru_ba, lru_wx, lru_bx, lru_lambda, cmp_w,
                  w_br_a, w_br_b, w_out, ln1_g, ln1_b, ln2_g, ln2_b,
                  router_wg, router_bg, router_we, router_be, exp_w1, exp_w3, exp_w2):
    n_main = 3 * D_RNN + 3 * KV_W
    n_ng = 3 * N_HEADS
    w = w_in[l]
    vec = lambda a: a[l].reshape(1, -1)
    wr = jnp.concatenate([router_wg[l], router_we[l]], axis=1)
    br = jnp.concatenate([router_bg[l], router_be[l]])
    npad = LANES - wr.shape[1]
    return dict(
        wm=w[:, :n_main].astype(BF16),
        wng=jnp.pad(w[:, n_main:n_main + n_ng], ((0, 0), (0, LANES - n_ng))).astype(BF16),
        wmg=w[:, n_main + n_ng:].astype(BF16),
        wt=jnp.concatenate([w[:, 2 * D_RNN:3 * D_RNN], w[:, 3 * D_RNN + KV_W + KV_W // 2:3 * D_RNN + 2 * KV_W],
                            w[:, 3 * D_RNN + 2 * KV_W + KV_W // 2:n_main],
                            jnp.pad(w[:, n_main:n_main + n_ng], ((0, 0), (0, LANES - n_ng)))], axis=1).T.astype(BF16),
        cw=conv_w[l], cb=vec(conv_b),
        wa=_block_diag4(lru_wa[l]), ba=vec(lru_ba), wx=_block_diag4(lru_wx[l]), bx=vec(lru_bx),
        lam=vec(lru_lambda),
        cmpw=jnp.broadcast_to(cmp_w[l][:, :, None, :], (CMP_LEN, 2, KV_HEADS, HEAD_DIM)).reshape(CMP_LEN, KV_W),
        cmpwt=jnp.tile(cmp_w[l].reshape(2, CMP_STRIDE, 2, HEAD_DIM).transpose(0, 2, 3, 1), (1, 1, 1, LANES // CMP_STRIDE)),
        wbra=w_br_a[l].astype(BF16), wbrb=w_br_b[l].astype(BF16), wout=w_out[l].astype(BF16),
        ln1g=vec(ln1_g), ln1b=vec(ln1_b), ln2g=vec(ln2_g), ln2b=vec(ln2_b),
        wr=jnp.pad(wr, ((0, 0), (0, npad))).astype(BF16), br=jnp.pad(br, (0, npad)).reshape(1, LANES),
        w1=exp_w1[l].astype(BF16), w3=exp_w3[l].astype(BF16), w2=exp_w2[l].astype(BF16),
    )


def _mixer_tail(p, x2d, y_rnn, y_att, mg):
    n = x2d.shape[0]
    x1, x1b, comb = _merge_out(x2d, y_rnn.reshape(n, D_RNN), y_att.reshape(n, -1), mg,
                               p["wbra"], p["wbrb"], p["wout"], p["ln1g"], p["ln1b"], p["wr"], p["br"])
    return _moe_grouped(x1, x1b, comb, p["w1"], p["w3"], p["w2"], p["ln2g"], p["ln2b"])


def kernel(x_prompt, x_sample, cache_cmp, cache_sel, cache_win, state_conv, state_lru, page_table,
           w_in, conv_w, conv_b, lru_wa, lru_ba, lru_wx, lru_bx, lru_lambda, cmp_w,
           w_br_a, w_br_b, w_out, ln1_g, ln1_b, ln2_g, ln2_b,
           router_wg, router_bg, router_we, router_be, exp_w1, exp_w3, exp_w2):
    bp, tp, _ = x_prompt.shape
    bs, ts, _ = x_sample.shape
    n_pages = page_table.shape[1]
    past = n_pages * PAGE_SIZE
    assert ts <= SEL_BLOCK and n_pages % CMP_PAGES_PER_STEP == 0 and tp % SEL_BLOCK == 0
    depth = w_in.shape[0]
    rows_minor = lambda a: a.transpose(0, 1, 3, 4, 5, 2)
    ccmp = rows_minor(cache_cmp)
    csel = rows_minor(cache_sel)
    cwin = rows_minor(cache_win)
    pt_flat = page_table.reshape(-1).astype(jnp.int32)
    pad = LANES
    kv6 = lambda a, b, t: a.reshape(b, t, 2, KV_HEADS, HEAD_DIM)

    xp = x_prompt.reshape(bp * tp, D_MODEL)
    xs = x_sample.reshape(bs * ts, D_MODEL)
    outs = [[] for _ in range(10)]
    for l in range(depth):
        p = _layer_params(l, w_in, conv_w, conv_b, lru_wa, lru_ba, lru_wx, lru_bx, lru_lambda, cmp_w,
                          w_br_a, w_br_b, w_out, ln1_g, ln1_b, ln2_g, ln2_b,
                          router_wg, router_bg, router_we, router_be, exp_w1, exp_w3, exp_w2)
        lru = (p["cw"], p["cb"], p["wa"], p["ba"], p["wx"], p["bx"], p["lam"])

        xr, gr, q, kvc, kvs, kvw, kvsb, kvwb, ng, mg = _project_in(xp, p["wm"], p["wng"], p["wmg"])
        r3 = lambda a: a.reshape(bp, tp, -1)
        y_rnn, cbuf, ht = _rglru(r3(xr), r3(gr), jnp.zeros((bp, CONV_W - 1, D_RNN), F32),
                                 jnp.zeros((bp, 1, D_RNN), F32), *lru, pos0=0)
        ca, cb = _compress_rows(r3(kvc), p["cmpw"])
        qt, vst, vwt, ngt = _project_t(xp, p["wt"])
        y_att = _nsa_prompt_t(qt, ngt, ca, cb, r3(kvsb), r3(kvwb), vst, vwt, bp, tp)
        xp = _mixer_tail(p, xp, y_rnn, y_att, mg)
        wlen = min(WINDOW, tp)
        outs[0].append(kv6(kvc, bp, tp))
        outs[2].append(kv6(kvs, bp, tp))
        outs[4].append(kv6(kvw, bp, tp)[:, tp - wlen:])
        outs[6].append(cbuf)
        outs[8].append(ht.reshape(bp, D_RNN))

        xr, gr, q, kvc, kvs, kvw, kvsb, kvwb, ng, mg = _project_in(xs, p["wm"], p["wng"], p["wmg"])
        r3 = lambda a: a.reshape(bs, ts, -1)
        y_rnn, cbuf, ht = _rglru(r3(xr), r3(gr), state_conv[l], state_lru[l].reshape(bs, 1, D_RNN),
                                 *lru, pos0=past)
        ca, cb = _compress_pages(ccmp, pt_flat, p["cmpwt"], l, bs, n_pages)
        oc, sel = _nsa_sample_select(r3(q), ca, cb, past)
        padrows = lambda a: jnp.pad(r3(a), ((0, 0), (0, pad - ts), (0, 0)))
        y_att = _nsa_sample_attend(csel, pt_flat, l, r3(q), r3(ng), oc, sel, padrows(kvsb), cwin, padrows(kvwb),
                                   past)
        xs = _mixer_tail(p, xs, y_rnn, y_att, mg)
        outs[1].append(kv6(kvc, bs, ts))
        outs[3].append(kv6(kvs, bs, ts))
        kw_all = jnp.concatenate([cache_win[l], kv6(kvw, bs, ts)], axis=1)
        outs[5].append(kw_all[:, ts:])
        outs[7].append(cbuf)
        outs[9].append(ht.reshape(bs, D_RNN))

    st = [jnp.stack(o) for o in outs]
    return (xp.reshape(bp, tp, D_MODEL), xs.reshape(bs, ts, D_MODEL),
            st[0], st[1], st[2], st[3], st[4], st[5], st[6], st[7], st[8], st[9])
```

```python
import functools

import jax
import jax.numpy as jnp
import numpy as np
from jax import lax
from jax.experimental import pallas as pl
from jax.experimental.pallas import tpu as pltpu

F32 = jnp.float32
BF16 = jnp.bfloat16
NEG_INF = float("-inf")
MASK_NEG = -1e30

D_MODEL = 1024
D_RNN = 1024
N_RNN_BLOCKS = 16
RNN_BLOCK = D_RNN // N_RNN_BLOCKS
CONV_W = 4
LRU_C = 8.0
N_HEADS = 16
HEAD_DIM = 64
KV_HEADS = 4
GROUP = N_HEADS // KV_HEADS
CMP_LEN = 32
CMP_STRIDE = 16
SEL_BLOCK = 64
N_SELECT = 16
WINDOW = 512
FORCE_BONUS = 1e4
PAGE_SIZE = 128
N_GROUPS = 4
EXP_PER_GROUP = 8
N_EXPERTS = N_GROUPS * EXP_PER_GROUP
D_EXPERT = 256
DEPTH = 2
ALPHA = (2.0 * DEPTH) ** 0.25
LN_EPS = 1e-5
KV_W = 2 * KV_HEADS * HEAD_DIM
QK_SCALE = HEAD_DIM ** -0.5
PER_SEL = SEL_BLOCK // CMP_STRIDE
SLOPES = [2.0 ** (-8.0 * (h + 1) / N_HEADS) for h in range(N_HEADS)]

LANES = 128
VMEM_LIMIT_BYTES = 56 * 1024 * 1024

PROJ_TM = 256
LRU_TC = 256
MERGE_TM = 512
MOE_TM = 1024
MOE_CHUNK = 320
GID_LANE = 64
ATT_QB = 128
ATT_KC = 512
AUX_BLOCK_ROWS = 16
PAGES_PER_STEP = 32
CMP_PAGES_PER_STEP = 16


def _cparams(sem):
    return pltpu.CompilerParams(dimension_semantics=sem, vmem_limit_bytes=VMEM_LIMIT_BYTES)


def _full_spec(shape):
    nd = len(shape)
    return pl.BlockSpec(shape, lambda *_: (0,) * nd)


def _dot(a, b):
    return jnp.dot(a, b, preferred_element_type=F32)


def _dot_nt(a, b):
    return lax.dot_general(a, b, (((1,), (1,)), ((), ())), preferred_element_type=F32)


def _sigmoid(x):
    return 1.0 / (1.0 + jnp.exp(-x))


def _proj_kernel(x_ref, wm_ref, wng_ref, wmg_ref,
                 xr_ref, gr_ref, q_ref, kvc_ref, kvs_ref, kvw_ref, kvsb_ref, kvwb_ref, ng_ref, mg_ref):
    xb = x_ref[...].astype(BF16)
    xr_ref[...] = _dot(xb, wm_ref[:, 0:D_RNN])
    gr_ref[...] = _dot(xb, wm_ref[:, D_RNN:2 * D_RNN])
    q_ref[...] = _dot(xb, wm_ref[:, 2 * D_RNN:3 * D_RNN]).astype(BF16)
    o = 3 * D_RNN
    kvc_ref[...] = _dot(xb, wm_ref[:, o:o + KV_W])
    kvs = _dot(xb, wm_ref[:, o + KV_W:o + 2 * KV_W])
    kvs_ref[...] = kvs
    kvsb_ref[...] = kvs.astype(BF16)
    kvw = _dot(xb, wm_ref[:, o + 2 * KV_W:o + 3 * KV_W])
    kvw_ref[...] = kvw
    kvwb_ref[...] = kvw.astype(BF16)
    ng_ref[...] = _sigmoid(_dot(xb, wng_ref[...]))
    mg_ref[...] = _sigmoid(_dot(xb, wmg_ref[...]))


def _project_in(x2d, wm, wng, wmg):
    n = x2d.shape[0]
    tm = min(PROJ_TM, n)
    row = lambda w: pl.BlockSpec((tm, w), lambda i: (i, 0))
    outs = [(D_RNN, F32), (D_RNN, F32), (N_HEADS * HEAD_DIM, BF16), (KV_W, F32), (KV_W, F32), (KV_W, F32),
            (KV_W, BF16), (KV_W, BF16), (LANES, F32), (2 * D_MODEL, F32)]
    return pl.pallas_call(
        _proj_kernel,
        grid=(n // tm,),
        in_specs=[row(D_MODEL), _full_spec(wm.shape), _full_spec(wng.shape), _full_spec(wmg.shape)],
        out_specs=[row(w) for w, _ in outs],
        out_shape=[jax.ShapeDtypeStruct((n, w), dt) for w, dt in outs],
        compiler_params=_cparams(("arbitrary",)),
        name="project_in",
    )(x2d, wm, wng, wmg)


def _shift_rows(x, s, fill):
    row = lax.broadcasted_iota(jnp.int32, x.shape, 0)
    return jnp.where(row >= s, pltpu.roll(x, s, axis=0), fill)


def _gelu_tanh(x):
    c = np.float32(np.sqrt(2.0 / np.pi))
    return 0.5 * x * (1.0 + jnp.tanh(c * (x + np.float32(0.044715) * (x * x * x))))


def _rglru_kernel(xr_ref, gr_ref, cbuf_ref, h0_ref, cw_ref, cb_ref, wa_ref, ba_ref, wx_ref, bx_ref, lam_ref,
                  y_ref, nbuf_ref, ht_ref, xbuf, hcar, *, tc, pos0):
    c = pl.program_id(1)
    nc = pl.num_programs(1)

    @pl.when(c == 0)
    def _():
        xbuf[0:8, :] = jnp.zeros((8, D_RNN), F32)
        xbuf[8 - (CONV_W - 1):8, :] = cbuf_ref[...]
        hcar[...] = h0_ref[...]

    x = xr_ref[...]
    xbuf[8:8 + tc, :] = x
    conv = cb_ref[...] + cw_ref[CONV_W - 1:CONV_W, :] * x
    for j in range(1, CONV_W):
        conv = conv + cw_ref[CONV_W - 1 - j:CONV_W - j, :] * xbuf[8 - j:8 - j + tc, :]
    nbuf = xbuf[8 + tc - (CONV_W - 1):8 + tc, :]
    xbuf[0:8, :] = xbuf[tc:tc + 8, :]

    cvb = conv.astype(BF16)
    nb4 = D_RNN // 256
    za = jnp.concatenate([_dot(cvb[:, k * 256:(k + 1) * 256], wa_ref[k]) for k in range(nb4)], axis=1)
    zx = jnp.concatenate([_dot(cvb[:, k * 256:(k + 1) * 256], wx_ref[k]) for k in range(nb4)], axis=1)
    r = _sigmoid(za + ba_ref[...])
    i = _sigmoid(zx + bx_ref[...])
    nl = -lam_ref[...]
    softplus = jnp.maximum(nl, 0.0) + jnp.log(1.0 + jnp.exp(-jnp.abs(nl)))
    log_a = (-LRU_C) * r * softplus
    a = jnp.exp(log_a)
    pos = pos0 + c * tc + lax.broadcasted_iota(jnp.int32, (tc, 1), 0)
    mult = jnp.where(pos == 0, 1.0, jnp.sqrt(1.0 - jnp.exp(2.0 * log_a)))
    b = mult * i * conv

    s = 1
    while s < tc:
        a_sh = _shift_rows(a, s, 1.0)
        b_sh = _shift_rows(b, s, 0.0)
        b = a * b_sh + b
        a = a * a_sh
        s *= 2
    h = b + a * hcar[...]
    hcar[...] = h[tc - 1:tc, :]
    y_ref[...] = (_gelu_tanh(gr_ref[...]) * h).astype(BF16)

    @pl.when(c == nc - 1)
    def _():
        nbuf_ref[...] = nbuf
        ht_ref[...] = h[tc - 1:tc, :]


def _rglru(xr, gr, conv_buf, h0, cw, cb, wa_bd, ba, wx_bd, bx, lam, pos0):
    bsz, t, _ = xr.shape
    tc = min(LRU_TC, t)
    seq = pl.BlockSpec((None, tc, D_RNN), lambda b, c: (b, c, 0))
    per_b = lambda r: pl.BlockSpec((None, r, D_RNN), lambda b, c: (b, 0, 0))
    vec = pl.BlockSpec((1, D_RNN), lambda b, c: (0, 0))
    bd = pl.BlockSpec(wa_bd.shape, lambda b, c: (0, 0, 0))
    return pl.pallas_call(
        functools.partial(_rglru_kernel, tc=tc, pos0=pos0),
        grid=(bsz, t // tc),
        in_specs=[seq, seq, per_b(CONV_W - 1), per_b(1),
                  pl.BlockSpec((CONV_W, D_RNN), lambda b, c: (0, 0)), vec, bd, vec, bd, vec, vec],
        out_specs=[seq, per_b(CONV_W - 1), per_b(1)],
        out_shape=[jax.ShapeDtypeStruct((bsz, t, D_RNN), BF16),
                   jax.ShapeDtypeStruct((bsz, CONV_W - 1, D_RNN), F32),
                   jax.ShapeDtypeStruct((bsz, 1, D_RNN), F32)],
        scratch_shapes=[pltpu.VMEM((tc + 8, D_RNN), F32), pltpu.VMEM((1, D_RNN), F32)],
        compiler_params=_cparams(("arbitrary", "arbitrary")),
        name="rglru",
    )(xr, gr, conv_buf, h0, cw, cb, wa_bd, ba, wx_bd, bx, lam)


def _segment_sums(x, w1, w2):
    r = x.shape[0]
    x3 = x.reshape(r // CMP_STRIDE, CMP_STRIDE, KV_W)
    return jnp.sum(x3 * w1[None], axis=1), jnp.sum(x3 * w2[None], axis=1)


def _compress_kernel(kv_ref, w_ref, a_ref, b_ref):
    a, b = _segment_sums(kv_ref[...], w_ref[0:CMP_STRIDE, :], w_ref[CMP_STRIDE:CMP_LEN, :])
    a_ref[...] = a
    b_ref[...] = b


def _compress_rows(kv, w):
    bsz, t, _ = kv.shape
    tr = min(1024, t)
    ns = tr // CMP_STRIDE
    return pl.pallas_call(
        _compress_kernel,
        grid=(bsz, t // tr),
        in_specs=[pl.BlockSpec((None, tr, KV_W), lambda b, c: (b, c, 0)), _full_spec(w.shape)],
        out_specs=[pl.BlockSpec((None, ns, KV_W), lambda b, c: (b, c, 0))] * 2,
        out_shape=[jax.ShapeDtypeStruct((bsz, t // CMP_STRIDE, KV_W), F32)] * 2,
        compiler_params=_cparams(("arbitrary", "arbitrary")),
        name="compress_rows",
    )(kv, w)


def _split_bf16_f32(y):
    hi = y.astype(BF16)
    return hi, y - hi.astype(F32)


def _split_bf16(y):
    hi, rest = _split_bf16_f32(y)
    return hi, rest.astype(BF16)


def _compress_pages_kernel(pt_ref, *refs):
    del pt_ref
    pages = refs[:CMP_PAGES_PER_STEP]
    w_ref, seg_ref, a_ref, b_ref = refs[CMP_PAGES_PER_STEP:]
    seg = seg_ref[...]
    splits = []
    for role in range(2):
        rows = []
        for x in range(2):
            wt = w_ref[role, x]
            for g in range(KV_HEADS):
                rows.append(jnp.concatenate([p[x, g] * wt for p in pages], axis=1))
        splits.append(_split_bf16(jnp.concatenate(rows, axis=0)))
    zs = [_dot(hi, seg) + _dot(lo, seg) for hi, lo in splits]
    for z, o_ref in zip(zs, (a_ref, b_ref)):
        for x in range(2):
            for g in range(KV_HEADS):
                i = x * KV_HEADS + g
                o_ref[x, g] = z[i * HEAD_DIM:(i + 1) * HEAD_DIM, :]


def _page_specs(layer, n_pages, per_step):
    def spec(k):
        return pl.BlockSpec((None, None, 2, KV_HEADS, HEAD_DIM, PAGE_SIZE),
                            lambda b, j, pt: (layer, pt[b * n_pages + j * per_step + k], 0, 0, 0, 0))
    return [spec(k) for k in range(per_step)]


def _compress_pages(cache_t, pt_flat, wt, layer, bsz, n_pages):
    spp = PAGE_SIZE // CMP_STRIDE
    per_step = CMP_PAGES_PER_STEP
    assert per_step * spp == LANES
    steps = n_pages // per_step
    kdim = per_step * PAGE_SIZE
    lrow = lax.broadcasted_iota(jnp.int32, (kdim, LANES), 0)
    lcol = lax.broadcasted_iota(jnp.int32, (kdim, LANES), 1)
    seg = jnp.where(lrow // CMP_STRIDE == lcol, 1.0, 0.0).astype(BF16)
    out_block = pl.BlockSpec((None, 2, KV_HEADS, HEAD_DIM, LANES), lambda b, j, pt: (b, 0, 0, 0, j))
    grid_spec = pltpu.PrefetchScalarGridSpec(
        num_scalar_prefetch=1,
        grid=(bsz, steps),
        in_specs=_page_specs(layer, n_pages, per_step) + [
            pl.BlockSpec(wt.shape, lambda b, j, pt: (0, 0, 0, 0)), pl.BlockSpec(seg.shape, lambda b, j, pt: (0, 0))],
        out_specs=[out_block] * 2,
    )
    return pl.pallas_call(
        _compress_pages_kernel,
        grid_spec=grid_spec,
        out_shape=[jax.ShapeDtypeStruct((bsz, 2, KV_HEADS, HEAD_DIM, n_pages * spp), F32)] * 2,
        compiler_params=_cparams(("arbitrary", "arbitrary")),
        name="compress_pages",
    )(pt_flat, *([cache_t] * per_step), wt, seg)


def _row_slopes(g, n_q):
    r = lax.broadcasted_iota(jnp.int32, (GROUP * n_q, 1), 0) // n_q
    out = jnp.full((GROUP * n_q, 1), np.float32(SLOPES[g * GROUP + GROUP - 1]), F32)
    for k in range(GROUP - 2, -1, -1):
        out = jnp.where(r == k, np.float32(SLOPES[g * GROUP + k]), out)
    return out


def _stack_heads(q, g):
    return jnp.concatenate(
        [q[:, (g * GROUP + r) * HEAD_DIM:(g * GROUP + r + 1) * HEAD_DIM] for r in range(GROUP)], axis=0)


def _masked_softmax_rows(s, valid):
    s = jnp.where(valid, s, NEG_INF)
    m = jnp.max(s, axis=-1, keepdims=True)
    m = jnp.where(m == NEG_INF, 0.0, m)
    p = jnp.exp(s - m)
    return p / jnp.maximum(jnp.sum(p, axis=-1, keepdims=True), 1e-30)


def _flash_step(carry, s, valid, v, v_transposed=False):
    m, l, acc = carry
    s = jnp.where(valid, s, NEG_INF)
    m_new = jnp.maximum(m, jnp.max(s, axis=-1, keepdims=True))
    m_safe = jnp.where(m_new == NEG_INF, 0.0, m_new)
    alpha = jnp.exp(m - m_safe)
    p = jnp.exp(s - m_safe)
    l = alpha * l + jnp.sum(p, axis=-1, keepdims=True)
    pv = _dot_nt(p.astype(BF16), v) if v_transposed else _dot(p.astype(BF16), v)
    return m_new, l, alpha * acc + pv


def _flash_init(rows):
    return (jnp.full((rows, 1), NEG_INF, F32), jnp.zeros((rows, 1), F32), jnp.zeros((rows, HEAD_DIM), F32))


def _flash_out(carry):
    _, l, acc = carry
    return acc / jnp.maximum(l, 1e-30)


def _select_blocks(score, n_extra_above=None):
    n_blk = score.shape[1]
    lane = lax.broadcasted_iota(jnp.int32, score.shape, 1)
    cnt = jnp.zeros(score.shape, jnp.int32) if n_extra_above is None else n_extra_above
    for k in range(n_blk):
        col = score[:, k:k + 1]
        cnt = cnt + jnp.where(lane > k, jnp.where(col >= score, 1, 0), jnp.where(col > score, 1, 0))
    return cnt < N_SELECT


def _block_sum_matrix(n_cmp, n_blk):
    c = lax.broadcasted_iota(jnp.int32, (n_cmp, n_blk), 0)
    j = lax.broadcasted_iota(jnp.int32, (n_cmp, n_blk), 1)
    return jnp.where(c // PER_SEL == j, 1.0, 0.0).astype(F32)


def _expand_matrix(n_blk, blk0, k0, kc):
    j = lax.broadcasted_iota(jnp.int32, (n_blk, kc), 0) + blk0
    c = lax.broadcasted_iota(jnp.int32, (n_blk, kc), 1) + k0
    return jnp.where(c // SEL_BLOCK == j, 1.0, 0.0).astype(BF16)


def _sum_heads(p, n_q):
    out = p[0:n_q]
    for r in range(1, GROUP):
        out = out + p[r * n_q:(r + 1) * n_q]
    return out


def _tile_rows(x):
    return jnp.concatenate([x] * GROUP, axis=0)


def _gate_and_pack(ng, branches, n_q):
    pieces = []
    for g in range(KV_HEADS):
        for r in range(GROUP):
            idx = g * GROUP + r
            acc = None
            for br, o in enumerate(branches[g]):
                term = ng[:, br * N_HEADS + idx:br * N_HEADS + idx + 1] * o[r * n_q:(r + 1) * n_q]
                acc = term if acc is None else acc + term
            pieces.append(acc)
    return jnp.concatenate(pieces, axis=1)


def _select_blocks_t(score_t):
    n_blk = score_t.shape[0]
    slab = 8
    slabs = [score_t[i:i + slab] for i in range(0, n_blk, slab)]
    cnts = [jnp.zeros(s.shape, jnp.int32) for s in slabs]
    row = lax.broadcasted_iota(jnp.int32, slabs[0].shape, 0)
    for k in range(n_blk):
        pivot = score_t[k:k + 1, :]
        for i, s in enumerate(slabs):
            lo = i * slab
            if lo > k:
                beats = jnp.where(pivot >= s, 1, 0)
            elif lo + slab - 1 < k:
                beats = jnp.where(pivot > s, 1, 0)
            else:
                beats = jnp.where(row + lo > k, jnp.where(pivot >= s, 1, 0), jnp.where(pivot > s, 1, 0))
            cnts[i] = cnts[i] + beats
    cnt = jnp.concatenate(cnts, axis=0)
    return jnp.where(cnt < N_SELECT, 1.0, 0.0)


def _project_t_kernel(x_ref, wt_ref, qt_ref, vst_ref, vwt_ref, ngt_ref):
    xb = x_ref[...].astype(BF16)
    nq = N_HEADS * HEAD_DIM
    nv = KV_W // 2
    qt_ref[...] = (_dot_nt(wt_ref[0:nq, :], xb) * QK_SCALE).astype(BF16)
    vst_ref[...] = _dot_nt(wt_ref[nq:nq + nv, :], xb).astype(BF16)
    vwt_ref[...] = _dot_nt(wt_ref[nq + nv:nq + 2 * nv, :], xb).astype(BF16)
    ngt_ref[...] = _sigmoid(_dot_nt(wt_ref[nq + 2 * nv:nq + 2 * nv + LANES, :], xb))


def _project_t(x2d, wt):
    n = x2d.shape[0]
    tm = min(PROJ_TM, n)
    outs = [(N_HEADS * HEAD_DIM, BF16), (KV_W // 2, BF16), (KV_W // 2, BF16), (LANES, F32)]
    return pl.pallas_call(
        _project_t_kernel,
        grid=(n // tm,),
        in_specs=[pl.BlockSpec((tm, D_MODEL), lambda i: (i, 0)), _full_spec(wt.shape)],
        out_specs=[pl.BlockSpec((r, tm), lambda i: (0, i)) for r, _ in outs],
        out_shape=[jax.ShapeDtypeStruct((r, n), dt) for r, dt in outs],
        compiler_params=_cparams(("arbitrary",)),
        name="project_t",
    )(x2d, wt)


def _exp_cols(s):
    m = jnp.max(s, axis=0, keepdims=True)
    m = jnp.where(m == NEG_INF, 0.0, m)
    e = jnp.exp(s - m)
    return e, 1.0 / jnp.maximum(jnp.sum(e, axis=0, keepdims=True), 1e-30)


def _add_head_bias_t(s, table_ref, g, mask, n_q):
    return jnp.concatenate(
        [s[:, r * n_q:(r + 1) * n_q] + (table_ref[g * GROUP + r] + mask) for r in range(GROUP)], axis=1)


def _lane_slopes(g, n_q):
    r = lax.broadcasted_iota(jnp.int32, (1, GROUP * n_q), 1) // n_q
    out = jnp.full((1, GROUP * n_q), np.float32(SLOPES[g * GROUP + GROUP - 1]), F32)
    for k in range(GROUP - 2, -1, -1):
        out = jnp.where(r == k, np.float32(SLOPES[g * GROUP + k]), out)
    return out


def _nsa_prompt_t_kernel(qt_ref, ngt_ref, ca_ref, cb_ref, ks_ref, kw_ref, vst_ref, vwt_ref, o_ref,
                         aux_s, bwin, bcmp, rhs_s, seln_s, oc_s, ow_s, m_s, l_s, acc_s, *, t, qb, kc, wk):
    n_seg = ca_ref.shape[0]
    n_cmp_valid = (t - CMP_LEN) // CMP_STRIDE + 1
    n_blk = t // SEL_BLOCK
    nv = KV_W // 2
    ql = lax.broadcasted_iota(jnp.int32, (1, qb), 1)

    def krow(n):
        return lax.broadcasted_iota(jnp.int32, (n, 1), 0)

    @pl.when((pl.program_id(0) == 0) & (pl.program_id(1) == 0))
    def _():
        d_win = (ql - krow(wk)).astype(F32)
        d_cmp = (ql - (krow(n_seg) * CMP_STRIDE + (CMP_LEN - 1))).astype(F32)
        for h in range(N_HEADS):
            slope = np.float32(SLOPES[h])
            bwin[h] = -(slope * d_win)
            bcmp[h] = -(slope * d_cmp)
        c = lax.broadcasted_iota(jnp.int32, (kc, LANES), 0)
        ln = lax.broadcasted_iota(jnp.int32, (kc, LANES), 1)
        feat = jnp.where(ln < AUX_BLOCK_ROWS, jnp.where(c // SEL_BLOCK == ln, 1, 0),
                         jnp.where((ln >= AUX_BLOCK_ROWS) & (ln < AUX_BLOCK_ROWS + 3), (c // 16) * 16,
                                   jnp.where((ln >= AUX_BLOCK_ROWS + 3) & (ln < AUX_BLOCK_ROWS + 6), c % 16, 0)))
        aux_s[...] = feat.astype(F32).astype(BF16)
        for g in range(KV_HEADS):
            slope = _lane_slopes(g, qb)
            s1, r1 = _split_bf16_f32(slope)
            s2, r2 = _split_bf16_f32(r1)
            s3 = r2.astype(BF16)
            rhs_s[g, 0:AUX_BLOCK_ROWS, :] = jnp.zeros((AUX_BLOCK_ROWS, GROUP * qb), BF16)
            rhs_s[g, AUX_BLOCK_ROWS:2 * AUX_BLOCK_ROWS, :] = jnp.concatenate(
                [s1, s2, s3, s1, s2, s3, jnp.zeros((AUX_BLOCK_ROWS - 6, GROUP * qb), BF16)], axis=0)
            rhs_s[g, 2 * AUX_BLOCK_ROWS:LANES, :] = jnp.zeros((LANES - 2 * AUX_BLOCK_ROWS, GROUP * qb), BF16)

    s0 = pl.program_id(1) * qb

    def q_t(g):
        return jnp.concatenate(
            [qt_ref[(g * GROUP + r) * HEAD_DIM:(g * GROUP + r + 1) * HEAD_DIM, :] for r in range(GROUP)], axis=1)

    ckv = ca_ref[...] + pltpu.roll(cb_ref[...], n_seg - 1, axis=0)
    ck = ckv[:, 0:nv].astype(BF16)
    cv_t = ckv[:, nv:2 * nv].T.astype(BF16)

    cc = krow(n_seg)
    cmask = jnp.where((s0 + ql - (cc * CMP_STRIDE + (CMP_LEN - 1)) >= 0) & (cc < n_cmp_valid), 0.0, NEG_INF)
    kstart = pl.multiple_of(jnp.clip(s0 + qb - wk, 0, t - wk), qb)
    wd = (s0 - kstart) + ql - krow(wk)
    wmask = jnp.where((wd >= 0) & (wd < WINDOW), 0.0, NEG_INF)

    gj = lax.broadcasted_iota(jnp.int32, (n_blk, n_seg), 0)
    gc = lax.broadcasted_iota(jnp.int32, (n_blk, n_seg), 1)
    gsum_t = jnp.where(gc // PER_SEL == gj, 1.0, 0.0).astype(BF16)
    blk_t = krow(n_blk)
    cur_t = (s0 + ql) // SEL_BLOCK
    forced_t = (blk_t == 0) | (blk_t == cur_t) | (blk_t == cur_t - 1)

    heads = range(KV_HEADS)
    hsl = [slice(g * HEAD_DIM, (g + 1) * HEAD_DIM) for g in heads]
    qgs = [q_t(g) for g in heads]
    for g in heads:
        rhs_s[g, LANES:LANES + HEAD_DIM, :] = qgs[g]
    s_c = [_add_head_bias_t(_dot(ck[:, hsl[g]], qgs[g]), bcmp, g, cmask, qb) for g in heads]
    s_w = [_add_head_bias_t(_dot(kw_ref[pl.ds(kstart, wk), hsl[g]], qgs[g]), bwin, g, wmask, qb) for g in heads]
    ex_c = [_exp_cols(s) for s in s_c]
    ex_w = [_exp_cols(s) for s in s_w]
    imps = []
    for g in heads:
        e_hi, e_lo = _split_bf16(ex_c[g][0])
        oc_s[g] = _dot(cv_t[hsl[g], :], e_hi) * ex_c[g][1]
        imps.append((_dot(gsum_t, e_hi) + _dot(gsum_t, e_lo)) * ex_c[g][1])
        ow_s[g] = _dot(vwt_ref[hsl[g], pl.ds(kstart, wk)], ex_w[g][0].astype(BF16)) * ex_w[g][1]
    for g in heads:
        imp_t = imps[g][:, 0:qb]
        for r in range(1, GROUP):
            imp_t = imp_t + imps[g][:, r * qb:(r + 1) * qb]
        score_t = jnp.where(forced_t, FORCE_BONUS, imp_t)
        score_t = jnp.where(blk_t <= cur_t, score_t, NEG_INF)
        sel_t = jnp.where(blk_t <= cur_t, _select_blocks_t(score_t), 0.0)
        seln_s[g] = jnp.concatenate([jnp.where(sel_t > 0.5, 0.0, MASK_NEG)] * GROUP, axis=1)

    m_s[...] = jnp.full(m_s.shape, NEG_INF, F32)
    l_s[...] = jnp.zeros(l_s.shape, F32)
    acc_s[...] = jnp.zeros(acc_s.shape, F32)

    bpc = kc // SEL_BLOCK
    ql4 = jnp.concatenate([ql] * GROUP, axis=1)

    def chunk(c, diagonal):
        k0 = pl.multiple_of(c * kc, kc)
        delta = jnp.zeros((1, 1), jnp.int32) + (s0 - k0)
        qoff = (delta + ql4).astype(F32)
        if diagonal:
            causal = (delta + ql4 - krow(kc)) >= 0
        aux = aux_s[...]
        state = [(m_s[g], l_s[g], acc_s[g]) for g in range(KV_HEADS)]
        scores = []
        for g in range(KV_HEADS):
            blocks = seln_s[g, pl.ds(pl.multiple_of(c * bpc, bpc), bpc), :]
            rhs_s[g, 0:AUX_BLOCK_ROWS, :] = jnp.concatenate(
                [blocks, jnp.zeros((AUX_BLOCK_ROWS - bpc, GROUP * qb), F32)], axis=0).astype(BF16)
        for g in range(KV_HEADS):
            lhs = jnp.concatenate([aux, ks_ref[pl.ds(k0, kc), g * HEAD_DIM:(g + 1) * HEAD_DIM]], axis=1)
            s = _dot(lhs, rhs_s[g])
            scores.append(jnp.where(causal, s, MASK_NEG) if diagonal else s)
        probs, stats = [], []
        for g in range(KV_HEADS):
            m_prev, l_prev, _ = state[g]
            off = _lane_slopes(g, qb) * qoff
            m_new = jnp.maximum(m_prev, jnp.max(scores[g], axis=0, keepdims=True) - off)
            m_safe = jnp.where(m_new == NEG_INF, 0.0, m_new)
            alpha = jnp.exp(m_prev - m_safe)
            p = jnp.exp(scores[g] - (m_safe + off))
            stats.append((m_new, alpha * l_prev + jnp.sum(p, axis=0, keepdims=True), alpha))
            probs.append(p.astype(BF16))
        for g in range(KV_HEADS):
            m_new, l_new, alpha = stats[g]
            pv = _dot(vst_ref[g * HEAD_DIM:(g + 1) * HEAD_DIM, pl.ds(k0, kc)], probs[g])
            m_s[g] = m_new
            l_s[g] = l_new
            acc_s[g] = alpha * state[g][2] + pv

    def body(c, carry):
        chunk(c, False)
        return carry

    last = (s0 + qb + kc - 1) // kc - 1
    lax.fori_loop(0, last, body, 0)
    chunk(last, True)

    ngt = ngt_ref[...]
    pieces = []
    for g in range(KV_HEADS):
        o_s = acc_s[g] / jnp.maximum(l_s[g], 1e-30)
        o_c = oc_s[g]
        o_w = ow_s[g]
        for r in range(GROUP):
            h = g * GROUP + r
            sl = slice(r * qb, (r + 1) * qb)
            pieces.append(ngt[h:h + 1, :] * o_c[:, sl] + ngt[N_HEADS + h:N_HEADS + h + 1, :] * o_s[:, sl]
                          + ngt[2 * N_HEADS + h:2 * N_HEADS + h + 1, :] * o_w[:, sl])
    o_ref[...] = jnp.concatenate(pieces, axis=0).T.astype(BF16)


def _nsa_prompt_t(qt, ngt, ca, cb, ksb, kwb, vst, vwt, bsz, t):
    qb = min(ATT_QB, t)
    kc = min(ATT_KC, t)
    wk = min(WINDOW + qb, t)
    n_seg = ca.shape[1]
    n_blk = t // SEL_BLOCK
    nq = t // qb
    cols = GROUP * qb
    per_b = lambda r, w: pl.BlockSpec((None, r, w), lambda b, i: (b, 0, 0))
    tok = lambda r: pl.BlockSpec((r, qb), lambda b, i: (0, b * nq + i))
    seq = lambda r: pl.BlockSpec((r, t), lambda b, i: (0, b))
    return pl.pallas_call(
        functools.partial(_nsa_prompt_t_kernel, t=t, qb=qb, kc=kc, wk=wk),
        grid=(bsz, nq),
        in_specs=[tok(N_HEADS * HEAD_DIM), tok(LANES), per_b(n_seg, KV_W), per_b(n_seg, KV_W),
                  per_b(t, KV_W), per_b(t, KV_W), seq(KV_W // 2), seq(KV_W // 2)],
        out_specs=pl.BlockSpec((None, qb, N_HEADS * HEAD_DIM), lambda b, i: (b, i, 0)),
        out_shape=jax.ShapeDtypeStruct((bsz, t, N_HEADS * HEAD_DIM), BF16),
        scratch_shapes=[pltpu.VMEM((kc, LANES), BF16), pltpu.VMEM((N_HEADS, wk, qb), F32),
                        pltpu.VMEM((N_HEADS, n_seg, qb), F32), pltpu.VMEM((KV_HEADS, LANES + HEAD_DIM, cols), BF16),
                        pltpu.VMEM((KV_HEADS, n_blk, cols), F32),
                        pltpu.VMEM((KV_HEADS, HEAD_DIM, cols), F32), pltpu.VMEM((KV_HEADS, HEAD_DIM, cols), F32),
                        pltpu.VMEM((KV_HEADS, 1, cols), F32), pltpu.VMEM((KV_HEADS, 1, cols), F32),
                        pltpu.VMEM((KV_HEADS, HEAD_DIM, cols), F32)],
        compiler_params=_cparams(("arbitrary", "arbitrary")),
        name="nsa_prompt",
    )(qt, ngt, ca, cb, ksb, kwb, vst, vwt)


def _nsa_sample_select_kernel(q_ref, ca_ref, cb_ref, oc_ref, sel_ref, *, ts, past):
    n_seg = ca_ref.shape[-1]
    n_cmp_valid = n_seg - 1
    n_pblk = past // SEL_BLOCK
    qrow = past + lax.broadcasted_iota(jnp.int32, (ts, 1), 0)
    qpos = _tile_rows(qrow)
    q = q_ref[...]
    gsum = _block_sum_matrix(n_seg, n_pblk)
    blk = lax.broadcasted_iota(jnp.int32, (1, n_pblk), 1)
    cur = qrow // SEL_BLOCK
    forced = (blk == 0) | (blk == cur) | (blk == cur - 1)
    col = lax.broadcasted_iota(jnp.int32, (1, n_seg), 1)
    d = qpos - (col * CMP_STRIDE + (CMP_LEN - 1))
    valid = (d >= 0) & (col < n_cmp_valid)
    df = d.astype(F32)
    outs = []
    for g in range(KV_HEADS):
        qg = _stack_heads(q, g)
        ck_t = (ca_ref[0, g] + pltpu.roll(cb_ref[0, g], n_seg - 1, axis=1)).astype(BF16)
        cv_t = (ca_ref[1, g] + pltpu.roll(cb_ref[1, g], n_seg - 1, axis=1)).astype(BF16)
        s = _dot(qg, ck_t) * QK_SCALE - _row_slopes(g, ts) * df
        p_c = _masked_softmax_rows(s, valid)
        outs.append(_dot_nt(p_c.astype(BF16), cv_t))
        imp = jnp.dot(_sum_heads(p_c, ts), gsum, preferred_element_type=F32, precision=lax.Precision.HIGHEST)
        score = jnp.where(forced, FORCE_BONUS, imp)
        above = jnp.where(FORCE_BONUS > score, 1, 0)
        sel_ref[g * ts:(g + 1) * ts, :] = jnp.where(_select_blocks(score, above), 1.0, 0.0)
    for g in range(KV_HEADS):
        oc_ref[g * GROUP * ts:(g + 1) * GROUP * ts, :] = outs[g]


def _nsa_sample_select(q, ca, cb, past):
    bsz, ts, _ = q.shape
    n_seg = ca.shape[-1]
    n_pblk = past // SEL_BLOCK
    per_b = lambda r, w: pl.BlockSpec((None, r, w), lambda b: (b, 0, 0))
    seg_t = pl.BlockSpec((None, 2, KV_HEADS, HEAD_DIM, n_seg), lambda b: (b, 0, 0, 0, 0))
    return pl.pallas_call(
        functools.partial(_nsa_sample_select_kernel, ts=ts, past=past),
        grid=(bsz,),
        in_specs=[per_b(ts, N_HEADS * HEAD_DIM), seg_t, seg_t],
        out_specs=[per_b(N_HEADS * ts, HEAD_DIM), per_b(KV_HEADS * ts, n_pblk)],
        out_shape=[jax.ShapeDtypeStruct((bsz, N_HEADS * ts, HEAD_DIM), F32),
                   jax.ShapeDtypeStruct((bsz, KV_HEADS * ts, n_pblk), F32)],
        compiler_params=_cparams(("arbitrary",)),
        name="nsa_sample_select",
    )(q, ca, cb)


def _nsa_sample_attend_kernel(pt_ref, *refs, ts, past, pad, n_pg):
    del pt_ref
    pages = refs[:n_pg]
    (q_ref, ng_ref, oc_ref, sel_ref, tail_ref, wbuf_ref, wnew_ref, o_ref, m_s, l_s, acc_s) = refs[n_pg:]
    j = pl.program_id(1)
    nj = pl.num_programs(1)
    rows = GROUP * ts
    kc = n_pg * PAGE_SIZE
    bps = kc // SEL_BLOCK
    qrow = past + lax.broadcasted_iota(jnp.int32, (ts, 1), 0)
    qpos = _tile_rows(qrow)
    q = q_ref[...]

    @pl.when(j == 0)
    def _():
        m_s[...] = jnp.full(m_s.shape, NEG_INF, F32)
        l_s[...] = jnp.zeros(l_s.shape, F32)
        acc_s[...] = jnp.zeros(acc_s.shape, F32)

    k0 = j * kc
    d = qpos - (k0 + lax.broadcasted_iota(jnp.int32, (1, kc), 1))
    df = d.astype(F32)
    expand = _expand_matrix(bps, 0, 0, kc)
    heads = range(KV_HEADS)
    state = [(m_s[g], l_s[g], acc_s[g]) for g in heads]
    scores = []
    for g in heads:
        k_t = jnp.concatenate([p[0, g] for p in pages], axis=1).astype(BF16)
        s = _dot(_stack_heads(q, g), k_t) * QK_SCALE - _row_slopes(g, ts) * df
        selk = _tile_rows(_dot(sel_ref[g * ts:(g + 1) * ts, :].astype(BF16), expand))
        scores.append(jnp.where((d >= 0) & (selk > 0.5), s, NEG_INF))
    m_new = [jnp.maximum(state[g][0], jnp.max(scores[g], axis=-1, keepdims=True)) for g in heads]
    m_safe = [jnp.where(m == NEG_INF, 0.0, m) for m in m_new]
    probs = [jnp.exp(scores[g] - m_safe[g]) for g in heads]
    alpha = [jnp.exp(state[g][0] - m_safe[g]) for g in heads]
    l_new = [alpha[g] * state[g][1] + jnp.sum(probs[g], axis=-1, keepdims=True) for g in heads]
    for g in heads:
        v_t = jnp.concatenate([p[1, g] for p in pages], axis=1).astype(BF16)
        acc_s[g] = alpha[g] * state[g][2] + _dot_nt(probs[g].astype(BF16), v_t)
        m_s[g] = m_new[g]
        l_s[g] = l_new[g]

    @pl.when(j == nj - 1)
    def _():
        ng = ng_ref[...]
        tail = tail_ref[...]
        wnew = wnew_ref[...]
        wb = wbuf_ref.shape[-1]
        colp = lax.broadcasted_iota(jnp.int32, (1, pad), 1)
        d_new = qpos - (past + colp)
        ok_new = (d_new >= 0) & (colp < ts)
        d_old = qpos - (past - wb + lax.broadcasted_iota(jnp.int32, (1, wb), 1))
        ok_old = (d_old >= 0) & (d_old < WINDOW)
        branches = []
        for g in range(KV_HEADS):
            qg = _stack_heads(q, g)
            slope = _row_slopes(g, ts)
            ksl = slice(g * HEAD_DIM, (g + 1) * HEAD_DIM)
            vsl = slice(KV_W // 2 + g * HEAD_DIM, KV_W // 2 + (g + 1) * HEAD_DIM)
            sl = slice(g * rows, (g + 1) * rows)
            s = _dot_nt(qg, tail[:, ksl]) * QK_SCALE - slope * d_new.astype(F32)
            o_s = _flash_out(_flash_step((m_s[g], l_s[g], acc_s[g]), s, ok_new, tail[:, vsl]))
            s = _dot(qg, wbuf_ref[0, g].astype(BF16)) * QK_SCALE - slope * d_old.astype(F32)
            carry = _flash_step(_flash_init(rows), s, ok_old, wbuf_ref[1, g].astype(BF16), v_transposed=True)
            s = _dot_nt(qg, wnew[:, ksl]) * QK_SCALE - slope * d_new.astype(F32)
            o_w = _flash_out(_flash_step(carry, s, ok_new & (d_new < WINDOW), wnew[:, vsl]))
            branches.append([oc_ref[sl, :], o_s, o_w])
        o_ref[...] = _gate_and_pack(ng, branches, ts).astype(BF16)


def _nsa_sample_attend(cache_t, pt_flat, layer, q, ng, oc, sel, tail, wbuf_t, wnew, past):
    bsz, ts, _ = q.shape
    n_pages = past // PAGE_SIZE
    n_pg = min(PAGES_PER_STEP, n_pages)
    steps = n_pages // n_pg
    pad = tail.shape[1]
    wb = wbuf_t.shape[-1]
    bps = n_pg * PAGE_SIZE // SEL_BLOCK
    sel4 = sel.reshape(bsz, KV_HEADS * ts, steps, bps).transpose(0, 2, 1, 3)
    per_b = lambda r, w: pl.BlockSpec((None, r, w), lambda b, j, pt: (b, 0, 0))
    grid_spec = pltpu.PrefetchScalarGridSpec(
        num_scalar_prefetch=1,
        grid=(bsz, steps),
        in_specs=_page_specs(layer, n_pages, n_pg) + [
            per_b(ts, N_HEADS * HEAD_DIM), per_b(ts, LANES), per_b(N_HEADS * ts, HEAD_DIM),
            pl.BlockSpec((None, None, KV_HEADS * ts, bps), lambda b, j, pt: (b, j, 0, 0)),
            per_b(pad, KV_W),
            pl.BlockSpec((None, None, 2, KV_HEADS, HEAD_DIM, wb), lambda b, j, pt: (layer, b, 0, 0, 0, 0)),
            per_b(pad, KV_W)],
        out_specs=per_b(ts, N_HEADS * HEAD_DIM),
        scratch_shapes=[pltpu.VMEM((KV_HEADS, GROUP * ts, 1), F32), pltpu.VMEM((KV_HEADS, GROUP * ts, 1), F32),
                        pltpu.VMEM((KV_HEADS, GROUP * ts, HEAD_DIM), F32)],
    )
    return pl.pallas_call(
        functools.partial(_nsa_sample_attend_kernel, ts=ts, past=past, pad=pad, n_pg=n_pg),
        grid_spec=grid_spec,
        out_shape=jax.ShapeDtypeStruct((bsz, ts, N_HEADS * HEAD_DIM), BF16),
        compiler_params=_cparams(("arbitrary", "arbitrary")),
        name="nsa_sample_attend",
    )(pt_flat, *([cache_t] * n_pg), q, ng, oc, sel4, tail, wbuf_t, wnew)


def _layer_norm(x, g, b):
    mu = jnp.mean(x, axis=-1, keepdims=True)
    xc = x - mu
    var = jnp.mean(xc * xc, axis=-1, keepdims=True)
    return xc * lax.rsqrt(var + LN_EPS) * g + b


def _merge_kernel(x_ref, yr_ref, ya_ref, mg_ref, wa_ref, wb_ref, wo_ref, g_ref, b_ref, wr_ref, br_ref,
                  o_ref, ob_ref, comb_ref):
    a = _dot(yr_ref[...], wa_ref[...])
    b = _dot(ya_ref[...], wb_ref[...])
    merged = mg_ref[:, 0:D_MODEL] * a + mg_ref[:, D_MODEL:2 * D_MODEL] * b
    mix = _dot(merged.astype(BF16), wo_ref[...])
    x1 = _layer_norm(ALPHA * x_ref[...] + mix, g_ref[...], b_ref[...])
    o_ref[...] = x1
    x1b = x1.astype(BF16)
    ob_ref[...] = x1b
    comb_ref[...] = _route(_dot(x1b, wr_ref[...]) + br_ref[...])


def _route(logits):
    n = logits.shape[0]
    lane = lax.broadcasted_iota(jnp.int32, (n, LANES), 1)
    is_g = lane < N_GROUPS
    gl = jnp.where(is_g, logits, NEG_INF)
    gmax = jnp.max(gl, axis=-1, keepdims=True)
    grp = jnp.min(jnp.where(gl == gmax, lane, LANES), axis=-1, keepdims=True)
    gw = 1.0 / jnp.sum(jnp.exp(gl - gmax), axis=-1, keepdims=True)
    e_idx = lane - N_GROUPS
    in_grp = (e_idx >= grp * EXP_PER_GROUP) & (e_idx < (grp + 1) * EXP_PER_GROUP)
    el = jnp.where(in_grp, logits, NEG_INF)
    v1 = jnp.max(el, axis=-1, keepdims=True)
    i1 = jnp.min(jnp.where(el == v1, lane, LANES), axis=-1, keepdims=True)
    el2 = jnp.where(lane == i1, NEG_INF, el)
    v2 = jnp.max(el2, axis=-1, keepdims=True)
    i2 = jnp.min(jnp.where(el2 == v2, lane, LANES), axis=-1, keepdims=True)
    e2 = jnp.exp(v2 - v1)
    w1 = gw / (1.0 + e2)
    w2 = gw * e2 / (1.0 + e2)
    comb = jnp.where(lane == i1, w1, 0.0) + jnp.where(lane == i2, w2, 0.0)
    comb = pltpu.roll(comb, LANES - N_GROUPS, axis=1)
    return comb + jnp.where(lane == GID_LANE, grp.astype(F32), 0.0)


def _merge_out(x2d, yr, ya, mg, wa, wb, wo, g, b, wr, br):
    n = x2d.shape[0]
    tm = min(MERGE_TM, n)
    row = lambda w: pl.BlockSpec((tm, w), lambda i: (i, 0))
    sq = _full_spec((D_MODEL, D_MODEL))
    vec = _full_spec((1, D_MODEL))
    return pl.pallas_call(
        _merge_kernel,
        grid=(n // tm,),
        in_specs=[row(D_MODEL), row(D_RNN), row(N_HEADS * HEAD_DIM), row(2 * D_MODEL), sq, sq, sq, vec, vec,
                  _full_spec((D_MODEL, LANES)), _full_spec((1, LANES))],
        out_specs=[row(D_MODEL), row(D_MODEL), row(LANES)],
        out_shape=[jax.ShapeDtypeStruct((n, D_MODEL), F32), jax.ShapeDtypeStruct((n, D_MODEL), BF16),
                   jax.ShapeDtypeStruct((n, LANES), F32)],
        compiler_params=_cparams(("arbitrary",)),
        name="merge_out",
    )(x2d, yr, ya, mg, wa, wb, wo, g, b, wr, br)


def _moe_grouped_kernel(x_ref, xb_ref, comb_ref, w1_ref, w3_ref, w2_ref, g_ref, b_ref, o_ref, ltri, *, tm, ch):
    grp = pl.program_id(1)
    n_grp = pl.num_programs(1)

    @pl.when((pl.program_id(0) == 0) & (grp == 0))
    def _():
        r = lax.broadcasted_iota(jnp.int32, (tm, tm), 0)
        c = lax.broadcasted_iota(jnp.int32, (tm, tm), 1)
        ltri[...] = jnp.where(c < r, 1.0, 0.0).astype(BF16)

    @pl.when(grp == 0)
    def _():
        o_ref[...] = jnp.zeros(o_ref.shape, F32)

    comb = comb_ref[...]
    lane = lax.broadcasted_iota(jnp.int32, (tm, LANES), 1)
    gid = jnp.sum(jnp.where(lane == GID_LANE, comb, 0.0), axis=-1, keepdims=True).astype(jnp.int32)
    member = gid == grp
    memf = jnp.where(member, 1.0, 0.0)
    rank = _dot(ltri[...], jnp.broadcast_to(memf, (tm, LANES)).astype(BF16))[:, 0:1]
    n_members = jnp.sum(memf).astype(jnp.int32)
    rows_t = jnp.where(lane == 0, rank, jnp.where(lane == 1, memf, 0.0)).T
    rank_row = rows_t[0:1, :]
    mem_row = rows_t[1:2, :] > 0.5
    xb = xb_ref[...]
    c_hi, c_rest = _split_bf16_f32(comb)
    c_mid, c_lo = _split_bf16(c_rest)
    lane_ch = lax.broadcasted_iota(jnp.int32, (ch, LANES), 1)

    def body(k, carry):
        base = (jnp.zeros((1, 1), jnp.int32) + k * ch).astype(F32)
        slot_col = lax.broadcasted_iota(jnp.int32, (ch, 1), 0).astype(F32) + base
        gather = jnp.where(mem_row & (rank_row == slot_col), 1.0, 0.0).astype(BF16)
        xg = _dot(gather, xb).astype(BF16)
        cwg = _dot(gather, c_hi) + _dot(gather, c_mid) + _dot(gather, c_lo)
        y = jnp.zeros((ch, D_MODEL), F32)
        for e in range(EXP_PER_GROUP):
            cw = jnp.sum(jnp.where(lane_ch == grp * EXP_PER_GROUP + e, cwg, 0.0), axis=-1, keepdims=True)
            h1 = _dot(xg, w1_ref[e])
            h = (h1 * _sigmoid(h1)) * _dot(xg, w3_ref[e]) * cw
            y = y + _dot(h.astype(BF16), w2_ref[e])
        slot_row = lax.broadcasted_iota(jnp.int32, (1, ch), 1).astype(F32) + base
        scatter = jnp.where(member & (rank == slot_row), 1.0, 0.0).astype(BF16)
        y_hi, y_lo = _split_bf16(y)
        o_ref[...] += _dot(scatter, y_hi) + _dot(scatter, y_lo)
        return carry

    lax.fori_loop(0, (n_members + ch - 1) // ch, body, 0)

    @pl.when(grp == n_grp - 1)
    def _():
        o_ref[...] = _layer_norm(ALPHA * x_ref[...] + o_ref[...], g_ref[...], b_ref[...])


def _moe_grouped(x1, x1b, comb, w1, w3, w2, g, b):
    n = x1.shape[0]
    tm = min(MOE_TM, n)
    ch = min(MOE_CHUNK, tm)
    row = lambda w: pl.BlockSpec((tm, w), lambda i, e: (i, 0))
    vec = pl.BlockSpec((1, D_MODEL), lambda i, e: (0, 0))
    wspec = lambda a, c: pl.BlockSpec((EXP_PER_GROUP, a, c), lambda i, e: (e, 0, 0))
    resid = pl.BlockSpec((tm, D_MODEL), lambda i, e: (i, 0), pipeline_mode=pl.Buffered(1))
    return pl.pallas_call(
        functools.partial(_moe_grouped_kernel, tm=tm, ch=ch),
        grid=(n // tm, N_GROUPS),
        in_specs=[resid, row(D_MODEL), row(LANES),
                  wspec(D_MODEL, D_EXPERT), wspec(D_MODEL, D_EXPERT), wspec(D_EXPERT, D_MODEL), vec, vec],
        out_specs=row(D_MODEL),
        out_shape=jax.ShapeDtypeStruct((n, D_MODEL), F32),
        scratch_shapes=[pltpu.VMEM((tm, tm), BF16)],
        compiler_params=_cparams(("arbitrary", "arbitrary")),
        name="moe_grouped",
    )(x1, x1b, comb, w1, w3, w2, g, b)


def _block_diag4(w):
    w4 = w.reshape(N_RNN_BLOCKS // 4, 4, RNN_BLOCK, RNN_BLOCK)
    eye = jnp.eye(4, dtype=w.dtype)
    return jnp.einsum("kaij,ab->kaibj", w4, eye).reshape(N_RNN_BLOCKS // 4, 256, 256).astype(BF16)


def _layer_params(l, w_in, conv_w, conv_b, lru_wa, lru_ba, lru_wx, lru_bx, lru_lambda, cmp_w,
                  w_br_a, w_br_b, w_out, ln1_g, ln1_b, ln2_g, ln2_b,
                  router_wg, router_bg, router_we, router_be, exp_w1, exp_w3, exp_w2):
    n_main = 3 * D_RNN + 3 * KV_W
    n_ng = 3 * N_HEADS
    w = w_in[l]
    vec = lambda a: a[l].reshape(1, -1)
    wr = jnp.concatenate([router_wg[l], router_we[l]], axis=1)
    br = jnp.concatenate([router_bg[l], router_be[l]])
    npad = LANES - wr.shape[1]
    return dict(
        wm=w[:, :n_main].astype(BF16),
        wng=jnp.pad(w[:, n_main:n_main + n_ng], ((0, 0), (0, LANES - n_ng))).astype(BF16),
        wmg=w[:, n_main + n_ng:].astype(BF16),
        wt=jnp.concatenate([w[:, 2 * D_RNN:3 * D_RNN], w[:, 3 * D_RNN + KV_W + KV_W // 2:3 * D_RNN + 2 * KV_W],
                            w[:, 3 * D_RNN + 2 * KV_W + KV_W // 2:n_main],
                            jnp.pad(w[:, n_main:n_main + n_ng], ((0, 0), (0, LANES - n_ng)))], axis=1).T.astype(BF16),
        cw=conv_w[l], cb=vec(conv_b),
        wa=_block_diag4(lru_wa[l]), ba=vec(lru_ba), wx=_block_diag4(lru_wx[l]), bx=vec(lru_bx),
        lam=vec(lru_lambda),
        cmpw=jnp.broadcast_to(cmp_w[l][:, :, None, :], (CMP_LEN, 2, KV_HEADS, HEAD_DIM)).reshape(CMP_LEN, KV_W),
        cmpwt=jnp.tile(cmp_w[l].reshape(2, CMP_STRIDE, 2, HEAD_DIM).transpose(0, 2, 3, 1), (1, 1, 1, LANES // CMP_STRIDE)),
        wbra=w_br_a[l].astype(BF16), wbrb=w_br_b[l].astype(BF16), wout=w_out[l].astype(BF16),
        ln1g=vec(ln1_g), ln1b=vec(ln1_b), ln2g=vec(ln2_g), ln2b=vec(ln2_b),
        wr=jnp.pad(wr, ((0, 0), (0, npad))).astype(BF16), br=jnp.pad(br, (0, npad)).reshape(1, LANES),
        w1=exp_w1[l].astype(BF16), w3=exp_w3[l].astype(BF16), w2=exp_w2[l].astype(BF16),
    )


def _mixer_tail(p, x2d, y_rnn, y_att, mg):
    n = x2d.shape[0]
    x1, x1b, comb = _merge_out(x2d, y_rnn.reshape(n, D_RNN), y_att.reshape(n, -1), mg,
                               p["wbra"], p["wbrb"], p["wout"], p["ln1g"], p["ln1b"], p["wr"], p["br"])
    return _moe_grouped(x1, x1b, comb, p["w1"], p["w3"], p["w2"], p["ln2g"], p["ln2b"])


def kernel(x_prompt, x_sample, cache_cmp, cache_sel, cache_win, state_conv, state_lru, page_table,
           w_in, conv_w, conv_b, lru_wa, lru_ba, lru_wx, lru_bx, lru_lambda, cmp_w,
           w_br_a, w_br_b, w_out, ln1_g, ln1_b, ln2_g, ln2_b,
           router_wg, router_bg, router_we, router_be, exp_w1, exp_w3, exp_w2):
    bp, tp, _ = x_prompt.shape
    bs, ts, _ = x_sample.shape
    n_pages = page_table.shape[1]
    past = n_pages * PAGE_SIZE
    assert ts <= SEL_BLOCK and n_pages % CMP_PAGES_PER_STEP == 0 and tp % SEL_BLOCK == 0
    depth = w_in.shape[0]
    rows_minor = lambda a: a.transpose(0, 1, 3, 4, 5, 2)
    ccmp = rows_minor(cache_cmp)
    csel = rows_minor(cache_sel)
    cwin = rows_minor(cache_win)
    pt_flat = page_table.reshape(-1).astype(jnp.int32)
    pad = LANES
    kv6 = lambda a, b, t: a.reshape(b, t, 2, KV_HEADS, HEAD_DIM)

    xp = x_prompt.reshape(bp * tp, D_MODEL)
    xs = x_sample.reshape(bs * ts, D_MODEL)
    outs = [[] for _ in range(10)]
    for l in range(depth):
        p = _layer_params(l, w_in, conv_w, conv_b, lru_wa, lru_ba, lru_wx, lru_bx, lru_lambda, cmp_w,
                          w_br_a, w_br_b, w_out, ln1_g, ln1_b, ln2_g, ln2_b,
                          router_wg, router_bg, router_we, router_be, exp_w1, exp_w3, exp_w2)
        lru = (p["cw"], p["cb"], p["wa"], p["ba"], p["wx"], p["bx"], p["lam"])

        xr, gr, q, kvc, kvs, kvw, kvsb, kvwb, ng, mg = _project_in(xp, p["wm"], p["wng"], p["wmg"])
        r3 = lambda a: a.reshape(bp, tp, -1)
        y_rnn, cbuf, ht = _rglru(r3(xr), r3(gr), jnp.zeros((bp, CONV_W - 1, D_RNN), F32),
                                 jnp.zeros((bp, 1, D_RNN), F32), *lru, pos0=0)
        ca, cb = _compress_rows(r3(kvc), p["cmpw"])
        qt, vst, vwt, ngt = _project_t(xp, p["wt"])
        y_att = _nsa_prompt_t(qt, ngt, ca, cb, r3(kvsb), r3(kvwb), vst, vwt, bp, tp)
        xp = _mixer_tail(p, xp, y_rnn, y_att, mg)
        wlen = min(WINDOW, tp)
        outs[0].append(kv6(kvc, bp, tp))
        outs[2].append(kv6(kvs, bp, tp))
        outs[4].append(kv6(kvw, bp, tp)[:, tp - wlen:])
        outs[6].append(cbuf)
        outs[8].append(ht.reshape(bp, D_RNN))

        xr, gr, q, kvc, kvs, kvw, kvsb, kvwb, ng, mg = _project_in(xs, p["wm"], p["wng"], p["wmg"])
        r3 = lambda a: a.reshape(bs, ts, -1)
        y_rnn, cbuf, ht = _rglru(r3(xr), r3(gr), state_conv[l], state_lru[l].reshape(bs, 1, D_RNN),
                                 *lru, pos0=past)
        ca, cb = _compress_pages(ccmp, pt_flat, p["cmpwt"], l, bs, n_pages)
        oc, sel = _nsa_sample_select(r3(q), ca, cb, past)
        padrows = lambda a: jnp.pad(r3(a), ((0, 0), (0, pad - ts), (0, 0)))
        y_att = _nsa_sample_attend(csel, pt_flat, l, r3(q), r3(ng), oc, sel, padrows(kvsb), cwin, padrows(kvwb),
                                   past)
        xs = _mixer_tail(p, xs, y_rnn, y_att, mg)
        outs[1].append(kv6(kvc, bs, ts))
        outs[3].append(kv6(kvs, bs, ts))
        kw_all = jnp.concatenate([cache_win[l], kv6(kvw, bs, ts)], axis=1)
        outs[5].append(kw_all[:, ts:])
        outs[7].append(cbuf)
        outs[9].append(ht.reshape(bs, D_RNN))

    st = [jnp.stack(o) for o in outs]
    return (xp.reshape(bp, tp, D_MODEL), xs.reshape(bs, ts, D_MODEL),
            st[0], st[1], st[2], st[3], st[4], st[5], st[6], st[7], st[8], st[9])
```

```python
import functools

import jax
import jax.numpy as jnp
import numpy as np
from jax import lax
from jax.experimental import pallas as pl
from jax.experimental.pallas import tpu as pltpu

F32 = jnp.float32
BF16 = jnp.bfloat16
NEG_INF = float("-inf")
MASK_NEG = -1e30

D_MODEL = 1024
D_RNN = 1024
N_RNN_BLOCKS = 16
RNN_BLOCK = D_RNN // N_RNN_BLOCKS
CONV_W = 4
LRU_C = 8.0
N_HEADS = 16
HEAD_DIM = 64
KV_HEADS = 4
GROUP = N_HEADS // KV_HEADS
CMP_LEN = 32
CMP_STRIDE = 16
SEL_BLOCK = 64
N_SELECT = 16
WINDOW = 512
FORCE_BONUS = 1e4
PAGE_SIZE = 128
N_GROUPS = 4
EXP_PER_GROUP = 8
N_EXPERTS = N_GROUPS * EXP_PER_GROUP
D_EXPERT = 256
DEPTH = 2
ALPHA = (2.0 * DEPTH) ** 0.25
LN_EPS = 1e-5
KV_W = 2 * KV_HEADS * HEAD_DIM
QK_SCALE = HEAD_DIM ** -0.5
PER_SEL = SEL_BLOCK // CMP_STRIDE
SLOPES = [2.0 ** (-8.0 * (h + 1) / N_HEADS) for h in range(N_HEADS)]

LANES = 128
VMEM_LIMIT_BYTES = 56 * 1024 * 1024

PROJ_TM = 256
LRU_TC = 256
MERGE_TM = 512
MOE_TM = 1024
MOE_CHUNK = 320
GID_LANE = 64
ATT_QB = 128
ATT_KC = 512
AUX_BLOCK_ROWS = 16
PAGES_PER_STEP = 64
CMP_PAGES_PER_STEP = 16


def _cparams(sem):
    return pltpu.CompilerParams(dimension_semantics=sem, vmem_limit_bytes=VMEM_LIMIT_BYTES)


def _full_spec(shape):
    nd = len(shape)
    return pl.BlockSpec(shape, lambda *_: (0,) * nd)


def _dot(a, b):
    return jnp.dot(a, b, preferred_element_type=F32)


def _dot_nt(a, b):
    return lax.dot_general(a, b, (((1,), (1,)), ((), ())), preferred_element_type=F32)


def _sigmoid(x):
    return 1.0 / (1.0 + jnp.exp(-x))


def _proj_kernel(x_ref, wm_ref, wng_ref, wmg_ref, *out_refs, with_q):
    if with_q:
        xr_ref, gr_ref, q_ref, kvc_ref, kvs_ref, kvw_ref, kvsb_ref, kvwb_ref, ng_ref, mg_ref = out_refs
    else:
        xr_ref, gr_ref, kvc_ref, kvs_ref, kvw_ref, kvsb_ref, kvwb_ref, mg_ref = out_refs
    xb = x_ref[...].astype(BF16)
    xr_ref[...] = _dot(xb, wm_ref[:, 0:D_RNN])
    gr_ref[...] = _dot(xb, wm_ref[:, D_RNN:2 * D_RNN])
    if with_q:
        q_ref[...] = _dot(xb, wm_ref[:, 2 * D_RNN:3 * D_RNN]).astype(BF16)
        ng_ref[...] = _sigmoid(_dot(xb, wng_ref[...]))
    o = 3 * D_RNN
    kvc_ref[...] = _dot(xb, wm_ref[:, o:o + KV_W])
    kvs = _dot(xb, wm_ref[:, o + KV_W:o + 2 * KV_W])
    kvs_ref[...] = kvs
    kvsb_ref[...] = kvs.astype(BF16)
    kvw = _dot(xb, wm_ref[:, o + 2 * KV_W:o + 3 * KV_W])
    kvw_ref[...] = kvw
    kvwb_ref[...] = kvw.astype(BF16)
    mg_ref[...] = _sigmoid(_dot(xb, wmg_ref[...]))


def _project_in(x2d, wm, wng, wmg, with_q):
    n = x2d.shape[0]
    tm = min(PROJ_TM, n)
    row = lambda w: pl.BlockSpec((tm, w), lambda i: (i, 0))
    q_out = [(N_HEADS * HEAD_DIM, BF16)] if with_q else []
    ng_out = [(LANES, F32)] if with_q else []
    outs = ([(D_RNN, F32), (D_RNN, F32)] + q_out + [(KV_W, F32), (KV_W, F32), (KV_W, F32), (KV_W, BF16), (KV_W, BF16)]
            + ng_out + [(2 * D_MODEL, F32)])
    return pl.pallas_call(
        functools.partial(_proj_kernel, with_q=with_q),
        grid=(n // tm,),
        in_specs=[row(D_MODEL), _full_spec(wm.shape), _full_spec(wng.shape), _full_spec(wmg.shape)],
        out_specs=[row(w) for w, _ in outs],
        out_shape=[jax.ShapeDtypeStruct((n, w), dt) for w, dt in outs],
        compiler_params=_cparams(("arbitrary",)),
        name="project_in",
    )(x2d, wm, wng, wmg)


def _shift_rows(x, s, fill):
    row = lax.broadcasted_iota(jnp.int32, x.shape, 0)
    return jnp.where(row >= s, pltpu.roll(x, s, axis=0), fill)


def _gelu_tanh(x):
    c = np.float32(np.sqrt(2.0 / np.pi))
    return 0.5 * x * (1.0 + jnp.tanh(c * (x + np.float32(0.044715) * (x * x * x))))


def _rglru_kernel(xr_ref, gr_ref, cbuf_ref, h0_ref, cw_ref, cb_ref, wa_ref, ba_ref, wx_ref, bx_ref, lam_ref,
                  y_ref, nbuf_ref, ht_ref, xbuf, hcar, *, tc, pos0):
    c = pl.program_id(1)
    nc = pl.num_programs(1)

    @pl.when(c == 0)
    def _():
        xbuf[0:8, :] = jnp.zeros((8, D_RNN), F32)
        xbuf[8 - (CONV_W - 1):8, :] = cbuf_ref[...]
        hcar[...] = h0_ref[...]

    x = xr_ref[...]
    xbuf[8:8 + tc, :] = x
    conv = cb_ref[...] + cw_ref[CONV_W - 1:CONV_W, :] * x
    for j in range(1, CONV_W):
        conv = conv + cw_ref[CONV_W - 1 - j:CONV_W - j, :] * xbuf[8 - j:8 - j + tc, :]
    nbuf = xbuf[8 + tc - (CONV_W - 1):8 + tc, :]
    xbuf[0:8, :] = xbuf[tc:tc + 8, :]

    cvb = conv.astype(BF16)
    nb4 = D_RNN // 256
    za = jnp.concatenate([_dot(cvb[:, k * 256:(k + 1) * 256], wa_ref[k]) for k in range(nb4)], axis=1)
    zx = jnp.concatenate([_dot(cvb[:, k * 256:(k + 1) * 256], wx_ref[k]) for k in range(nb4)], axis=1)
    r = _sigmoid(za + ba_ref[...])
    i = _sigmoid(zx + bx_ref[...])
    nl = -lam_ref[...]
    softplus = jnp.maximum(nl, 0.0) + jnp.log(1.0 + jnp.exp(-jnp.abs(nl)))
    log_a = (-LRU_C) * r * softplus
    a = jnp.exp(log_a)
    pos = pos0 + c * tc + lax.broadcasted_iota(jnp.int32, (tc, 1), 0)
    mult = jnp.where(pos == 0, 1.0, jnp.sqrt(1.0 - jnp.exp(2.0 * log_a)))
    b = mult * i * conv

    s = 1
    while s < tc:
        a_sh = _shift_rows(a, s, 1.0)
        b_sh = _shift_rows(b, s, 0.0)
        b = a * b_sh + b
        a = a * a_sh
        s *= 2
    h = b + a * hcar[...]
    hcar[...] = h[tc - 1:tc, :]
    y_ref[...] = (_gelu_tanh(gr_ref[...]) * h).astype(BF16)

    @pl.when(c == nc - 1)
    def _():
        nbuf_ref[...] = nbuf
        ht_ref[...] = h[tc - 1:tc, :]


def _rglru(xr, gr, conv_buf, h0, cw, cb, wa_bd, ba, wx_bd, bx, lam, pos0):
    bsz, t, _ = xr.shape
    tc = min(LRU_TC, t)
    seq = pl.BlockSpec((None, tc, D_RNN), lambda b, c: (b, c, 0))
    per_b = lambda r: pl.BlockSpec((None, r, D_RNN), lambda b, c: (b, 0, 0))
    vec = pl.BlockSpec((1, D_RNN), lambda b, c: (0, 0))
    bd = pl.BlockSpec(wa_bd.shape, lambda b, c: (0, 0, 0))
    return pl.pallas_call(
        functools.partial(_rglru_kernel, tc=tc, pos0=pos0),
        grid=(bsz, t // tc),
        in_specs=[seq, seq, per_b(CONV_W - 1), per_b(1),
                  pl.BlockSpec((CONV_W, D_RNN), lambda b, c: (0, 0)), vec, bd, vec, bd, vec, vec],
        out_specs=[seq, per_b(CONV_W - 1), per_b(1)],
        out_shape=[jax.ShapeDtypeStruct((bsz, t, D_RNN), BF16),
                   jax.ShapeDtypeStruct((bsz, CONV_W - 1, D_RNN), F32),
                   jax.ShapeDtypeStruct((bsz, 1, D_RNN), F32)],
        scratch_shapes=[pltpu.VMEM((tc + 8, D_RNN), F32), pltpu.VMEM((1, D_RNN), F32)],
        compiler_params=_cparams(("arbitrary", "arbitrary")),
        name="rglru",
    )(xr, gr, conv_buf, h0, cw, cb, wa_bd, ba, wx_bd, bx, lam)


def _segment_sums(x, w1, w2):
    r = x.shape[0]
    x3 = x.reshape(r // CMP_STRIDE, CMP_STRIDE, KV_W)
    return jnp.sum(x3 * w1[None], axis=1), jnp.sum(x3 * w2[None], axis=1)


def _compress_kernel(kv_ref, w_ref, a_ref, b_ref):
    a, b = _segment_sums(kv_ref[...], w_ref[0:CMP_STRIDE, :], w_ref[CMP_STRIDE:CMP_LEN, :])
    a_ref[...] = a
    b_ref[...] = b


def _compress_rows(kv, w):
    bsz, t, _ = kv.shape
    tr = min(1024, t)
    ns = tr // CMP_STRIDE
    return pl.pallas_call(
        _compress_kernel,
        grid=(bsz, t // tr),
        in_specs=[pl.BlockSpec((None, tr, KV_W), lambda b, c: (b, c, 0)), _full_spec(w.shape)],
        out_specs=[pl.BlockSpec((None, ns, KV_W), lambda b, c: (b, c, 0))] * 2,
        out_shape=[jax.ShapeDtypeStruct((bsz, t // CMP_STRIDE, KV_W), F32)] * 2,
        compiler_params=_cparams(("arbitrary", "arbitrary")),
        name="compress_rows",
    )(kv, w)


def _split_bf16_f32(y):
    hi = y.astype(BF16)
    return hi, y - hi.astype(F32)


def _split_bf16(y):
    hi, rest = _split_bf16_f32(y)
    return hi, rest.astype(BF16)


def _compress_pages_kernel(pt_ref, *refs):
    del pt_ref
    pages = refs[:CMP_PAGES_PER_STEP]
    w_ref, seg_ref, a_ref, b_ref = refs[CMP_PAGES_PER_STEP:]
    seg = seg_ref[...]
    splits = []
    for role in range(2):
        rows = []
        for x in range(2):
            wt = w_ref[role, x]
            for g in range(KV_HEADS):
                rows.append(jnp.concatenate([p[x, g] * wt for p in pages], axis=1))
        splits.append(_split_bf16(jnp.concatenate(rows, axis=0)))
    zs = [_dot(hi, seg) + _dot(lo, seg) for hi, lo in splits]
    for z, o_ref in zip(zs, (a_ref, b_ref)):
        for x in range(2):
            for g in range(KV_HEADS):
                i = x * KV_HEADS + g
                o_ref[x, g] = z[i * HEAD_DIM:(i + 1) * HEAD_DIM, :]


def _page_specs(layer, n_pages, per_step):
    def spec(k):
        return pl.BlockSpec((None, None, 2, KV_HEADS, HEAD_DIM, PAGE_SIZE),
                            lambda b, j, pt: (layer, pt[b * n_pages + j * per_step + k], 0, 0, 0, 0))
    return [spec(k) for k in range(per_step)]


def _compress_pages(cache_t, pt_flat, wt, layer, bsz, n_pages):
    spp = PAGE_SIZE // CMP_STRIDE
    per_step = CMP_PAGES_PER_STEP
    assert per_step * spp == LANES
    steps = n_pages // per_step
    kdim = per_step * PAGE_SIZE
    lrow = lax.broadcasted_iota(jnp.int32, (kdim, LANES), 0)
    lcol = lax.broadcasted_iota(jnp.int32, (kdim, LANES), 1)
    seg = jnp.where(lrow // CMP_STRIDE == lcol, 1.0, 0.0).astype(BF16)
    out_block = pl.BlockSpec((None, 2, KV_HEADS, HEAD_DIM, LANES), lambda b, j, pt: (b, 0, 0, 0, j))
    grid_spec = pltpu.PrefetchScalarGridSpec(
        num_scalar_prefetch=1,
        grid=(bsz, steps),
        in_specs=_page_specs(layer, n_pages, per_step) + [
            pl.BlockSpec(wt.shape, lambda b, j, pt: (0, 0, 0, 0)), pl.BlockSpec(seg.shape, lambda b, j, pt: (0, 0))],
        out_specs=[out_block] * 2,
    )
    return pl.pallas_call(
        _compress_pages_kernel,
        grid_spec=grid_spec,
        out_shape=[jax.ShapeDtypeStruct((bsz, 2, KV_HEADS, HEAD_DIM, n_pages * spp), F32)] * 2,
        compiler_params=_cparams(("arbitrary", "arbitrary")),
        name="compress_pages",
    )(pt_flat, *([cache_t] * per_step), wt, seg)


def _row_slopes(g, n_q):
    r = lax.broadcasted_iota(jnp.int32, (GROUP * n_q, 1), 0) // n_q
    out = jnp.full((GROUP * n_q, 1), np.float32(SLOPES[g * GROUP + GROUP - 1]), F32)
    for k in range(GROUP - 2, -1, -1):
        out = jnp.where(r == k, np.float32(SLOPES[g * GROUP + k]), out)
    return out


def _stack_heads(q, g):
    return jnp.concatenate(
        [q[:, (g * GROUP + r) * HEAD_DIM:(g * GROUP + r + 1) * HEAD_DIM] for r in range(GROUP)], axis=0)


def _masked_softmax_rows(s, valid):
    s = jnp.where(valid, s, NEG_INF)
    m = jnp.max(s, axis=-1, keepdims=True)
    m = jnp.where(m == NEG_INF, 0.0, m)
    p = jnp.exp(s - m)
    return p / jnp.maximum(jnp.sum(p, axis=-1, keepdims=True), 1e-30)


def _flash_step(carry, s, valid, v, v_transposed=False):
    m, l, acc = carry
    s = jnp.where(valid, s, NEG_INF)
    m_new = jnp.maximum(m, jnp.max(s, axis=-1, keepdims=True))
    m_safe = jnp.where(m_new == NEG_INF, 0.0, m_new)
    alpha = jnp.exp(m - m_safe)
    p = jnp.exp(s - m_safe)
    l = alpha * l + jnp.sum(p, axis=-1, keepdims=True)
    pv = _dot_nt(p.astype(BF16), v) if v_transposed else _dot(p.astype(BF16), v)
    return m_new, l, alpha * acc + pv


def _flash_init(rows):
    return (jnp.full((rows, 1), NEG_INF, F32), jnp.zeros((rows, 1), F32), jnp.zeros((rows, HEAD_DIM), F32))


def _flash_out(carry):
    _, l, acc = carry
    return acc / jnp.maximum(l, 1e-30)


def _select_blocks(score, n_extra_above=None):
    n_blk = score.shape[1]
    lane = lax.broadcasted_iota(jnp.int32, score.shape, 1)
    cnt = jnp.zeros(score.shape, jnp.int32) if n_extra_above is None else n_extra_above
    for k in range(n_blk):
        col = score[:, k:k + 1]
        cnt = cnt + jnp.where(lane > k, jnp.where(col >= score, 1, 0), jnp.where(col > score, 1, 0))
    return cnt < N_SELECT


def _block_sum_matrix(n_cmp, n_blk):
    c = lax.broadcasted_iota(jnp.int32, (n_cmp, n_blk), 0)
    j = lax.broadcasted_iota(jnp.int32, (n_cmp, n_blk), 1)
    return jnp.where(c // PER_SEL == j, 1.0, 0.0).astype(F32)


def _expand_matrix(n_blk, blk0, k0, kc):
    j = lax.broadcasted_iota(jnp.int32, (n_blk, kc), 0) + blk0
    c = lax.broadcasted_iota(jnp.int32, (n_blk, kc), 1) + k0
    return jnp.where(c // SEL_BLOCK == j, 1.0, 0.0).astype(BF16)


def _sum_heads(p, n_q):
    out = p[0:n_q]
    for r in range(1, GROUP):
        out = out + p[r * n_q:(r + 1) * n_q]
    return out


def _tile_rows(x):
    return jnp.concatenate([x] * GROUP, axis=0)


def _gate_and_pack(ng, branches, n_q):
    pieces = []
    for g in range(KV_HEADS):
        for r in range(GROUP):
            idx = g * GROUP + r
            acc = None
            for br, o in enumerate(branches[g]):
                term = ng[:, br * N_HEADS + idx:br * N_HEADS + idx + 1] * o[r * n_q:(r + 1) * n_q]
                acc = term if acc is None else acc + term
            pieces.append(acc)
    return jnp.concatenate(pieces, axis=1)


def _select_blocks_t(score_t):
    n_blk = score_t.shape[0]
    slab = 8
    slabs = [score_t[i:i + slab] for i in range(0, n_blk, slab)]
    cnts = [jnp.zeros(s.shape, jnp.int32) for s in slabs]
    row = lax.broadcasted_iota(jnp.int32, slabs[0].shape, 0)
    for k in range(n_blk):
        pivot = score_t[k:k + 1, :]
        for i, s in enumerate(slabs):
            lo = i * slab
            if lo > k:
                beats = jnp.where(pivot >= s, 1, 0)
            elif lo + slab - 1 < k:
                beats = jnp.where(pivot > s, 1, 0)
            else:
                beats = jnp.where(row + lo > k, jnp.where(pivot >= s, 1, 0), jnp.where(pivot > s, 1, 0))
            cnts[i] = cnts[i] + beats
    cnt = jnp.concatenate(cnts, axis=0)
    return jnp.where(cnt < N_SELECT, 1.0, 0.0)


def _project_t_kernel(x_ref, wt_ref, qt_ref, vst_ref, vwt_ref, ngt_ref):
    xb = x_ref[...].astype(BF16)
    nq = N_HEADS * HEAD_DIM
    nv = KV_W // 2
    qt_ref[...] = (_dot_nt(wt_ref[0:nq, :], xb) * QK_SCALE).astype(BF16)
    vst_ref[...] = _dot_nt(wt_ref[nq:nq + nv, :], xb).astype(BF16)
    vwt_ref[...] = _dot_nt(wt_ref[nq + nv:nq + 2 * nv, :], xb).astype(BF16)
    ngt_ref[...] = _sigmoid(_dot_nt(wt_ref[nq + 2 * nv:nq + 2 * nv + LANES, :], xb))


def _project_t(x2d, wt):
    n = x2d.shape[0]
    tm = min(PROJ_TM, n)
    outs = [(N_HEADS * HEAD_DIM, BF16), (KV_W // 2, BF16), (KV_W // 2, BF16), (LANES, F32)]
    return pl.pallas_call(
        _project_t_kernel,
        grid=(n // tm,),
        in_specs=[pl.BlockSpec((tm, D_MODEL), lambda i: (i, 0)), _full_spec(wt.shape)],
        out_specs=[pl.BlockSpec((r, tm), lambda i: (0, i)) for r, _ in outs],
        out_shape=[jax.ShapeDtypeStruct((r, n), dt) for r, dt in outs],
        compiler_params=_cparams(("arbitrary",)),
        name="project_t",
    )(x2d, wt)


def _exp_cols(s):
    m = jnp.max(s, axis=0, keepdims=True)
    m = jnp.where(m == NEG_INF, 0.0, m)
    e = jnp.exp(s - m)
    return e, 1.0 / jnp.maximum(jnp.sum(e, axis=0, keepdims=True), 1e-30)


def _add_head_bias_t(s, table_ref, g, mask, n_q):
    return jnp.concatenate(
        [s[:, r * n_q:(r + 1) * n_q] + (table_ref[g * GROUP + r] + mask) for r in range(GROUP)], axis=1)


def _lane_slopes(g, n_q):
    r = lax.broadcasted_iota(jnp.int32, (1, GROUP * n_q), 1) // n_q
    out = jnp.full((1, GROUP * n_q), np.float32(SLOPES[g * GROUP + GROUP - 1]), F32)
    for k in range(GROUP - 2, -1, -1):
        out = jnp.where(r == k, np.float32(SLOPES[g * GROUP + k]), out)
    return out


def _nsa_prompt_t_kernel(qt_ref, ngt_ref, ca_ref, cb_ref, ks_ref, kw_ref, vst_ref, vwt_ref, o_ref,
                         aux_s, bwin, bcmp, rhs_s, seln_s, oc_s, ow_s, m_s, l_s, acc_s, *, t, qb, kc, wk):
    n_seg = ca_ref.shape[0]
    n_cmp_valid = (t - CMP_LEN) // CMP_STRIDE + 1
    n_blk = t // SEL_BLOCK
    nv = KV_W // 2
    ql = lax.broadcasted_iota(jnp.int32, (1, qb), 1)

    def krow(n):
        return lax.broadcasted_iota(jnp.int32, (n, 1), 0)

    @pl.when((pl.program_id(0) == 0) & (pl.program_id(1) == 0))
    def _():
        d_win = (ql - krow(wk)).astype(F32)
        d_cmp = (ql - (krow(n_seg) * CMP_STRIDE + (CMP_LEN - 1))).astype(F32)
        for h in range(N_HEADS):
            slope = np.float32(SLOPES[h])
            bwin[h] = -(slope * d_win)
            bcmp[h] = -(slope * d_cmp)
        c = lax.broadcasted_iota(jnp.int32, (kc, LANES), 0)
        ln = lax.broadcasted_iota(jnp.int32, (kc, LANES), 1)
        feat = jnp.where(ln < AUX_BLOCK_ROWS, jnp.where(c // SEL_BLOCK == ln, 1, 0),
                         jnp.where((ln >= AUX_BLOCK_ROWS) & (ln < AUX_BLOCK_ROWS + 3), (c // 16) * 16,
                                   jnp.where((ln >= AUX_BLOCK_ROWS + 3) & (ln < AUX_BLOCK_ROWS + 6), c % 16, 0)))
        aux_s[...] = feat.astype(F32).astype(BF16)
        for g in range(KV_HEADS):
            slope = _lane_slopes(g, qb)
            s1, r1 = _split_bf16_f32(slope)
            s2, r2 = _split_bf16_f32(r1)
            s3 = r2.astype(BF16)
            rhs_s[g, 0:AUX_BLOCK_ROWS, :] = jnp.zeros((AUX_BLOCK_ROWS, GROUP * qb), BF16)
            rhs_s[g, AUX_BLOCK_ROWS:2 * AUX_BLOCK_ROWS, :] = jnp.concatenate(
                [s1, s2, s3, s1, s2, s3, jnp.zeros((AUX_BLOCK_ROWS - 6, GROUP * qb), BF16)], axis=0)
            rhs_s[g, 2 * AUX_BLOCK_ROWS:LANES, :] = jnp.zeros((LANES - 2 * AUX_BLOCK_ROWS, GROUP * qb), BF16)

    s0 = pl.program_id(1) * qb

    def q_t(g):
        return jnp.concatenate(
            [qt_ref[(g * GROUP + r) * HEAD_DIM:(g * GROUP + r + 1) * HEAD_DIM, :] for r in range(GROUP)], axis=1)

    ckv = ca_ref[...] + pltpu.roll(cb_ref[...], n_seg - 1, axis=0)
    ck = ckv[:, 0:nv].astype(BF16)
    cv_t = ckv[:, nv:2 * nv].T.astype(BF16)

    cc = krow(n_seg)
    cmask = jnp.where((s0 + ql - (cc * CMP_STRIDE + (CMP_LEN - 1)) >= 0) & (cc < n_cmp_valid), 0.0, NEG_INF)
    kstart = pl.multiple_of(jnp.clip(s0 + qb - wk, 0, t - wk), qb)
    wd = (s0 - kstart) + ql - krow(wk)
    wmask = jnp.where((wd >= 0) & (wd < WINDOW), 0.0, NEG_INF)

    gj = lax.broadcasted_iota(jnp.int32, (n_blk, n_seg), 0)
    gc = lax.broadcasted_iota(jnp.int32, (n_blk, n_seg), 1)
    gsum_t = jnp.where(gc // PER_SEL == gj, 1.0, 0.0).astype(BF16)
    blk_t = krow(n_blk)
    cur_t = (s0 + ql) // SEL_BLOCK
    forced_t = (blk_t == 0) | (blk_t == cur_t) | (blk_t == cur_t - 1)

    heads = range(KV_HEADS)
    hsl = [slice(g * HEAD_DIM, (g + 1) * HEAD_DIM) for g in heads]
    qgs = [q_t(g) for g in heads]
    for g in heads:
        rhs_s[g, LANES:LANES + HEAD_DIM, :] = qgs[g]
    s_c = [_add_head_bias_t(_dot(ck[:, hsl[g]], qgs[g]), bcmp, g, cmask, qb) for g in heads]
    s_w = [_add_head_bias_t(_dot(kw_ref[pl.ds(kstart, wk), hsl[g]], qgs[g]), bwin, g, wmask, qb) for g in heads]
    ex_c = [_exp_cols(s) for s in s_c]
    ex_w = [_exp_cols(s) for s in s_w]
    imps = []
    for g in heads:
        e_hi, e_lo = _split_bf16(ex_c[g][0])
        oc_s[g] = _dot(cv_t[hsl[g], :], e_hi) * ex_c[g][1]
        imps.append((_dot(gsum_t, e_hi) + _dot(gsum_t, e_lo)) * ex_c[g][1])
        ow_s[g] = _dot(vwt_ref[hsl[g], pl.ds(kstart, wk)], ex_w[g][0].astype(BF16)) * ex_w[g][1]
    for g in heads:
        imp_t = imps[g][:, 0:qb]
        for r in range(1, GROUP):
            imp_t = imp_t + imps[g][:, r * qb:(r + 1) * qb]
        score_t = jnp.where(forced_t, FORCE_BONUS, imp_t)
        score_t = jnp.where(blk_t <= cur_t, score_t, NEG_INF)
        sel_t = jnp.where(blk_t <= cur_t, _select_blocks_t(score_t), 0.0)
        seln_s[g] = jnp.concatenate([jnp.where(sel_t > 0.5, 0.0, MASK_NEG)] * GROUP, axis=1)

    m_s[...] = jnp.full(m_s.shape, NEG_INF, F32)
    l_s[...] = jnp.zeros(l_s.shape, F32)
    acc_s[...] = jnp.zeros(acc_s.shape, F32)

    bpc = kc // SEL_BLOCK
    ql4 = jnp.concatenate([ql] * GROUP, axis=1)

    def chunk(c, diagonal):
        k0 = pl.multiple_of(c * kc, kc)
        delta = jnp.zeros((1, 1), jnp.int32) + (s0 - k0)
        qoff = (delta + ql4).astype(F32)
        if diagonal:
            causal = (delta + ql4 - krow(kc)) >= 0
        aux = aux_s[...]
        state = [(m_s[g], l_s[g], acc_s[g]) for g in range(KV_HEADS)]
        scores = []
        for g in range(KV_HEADS):
            blocks = seln_s[g, pl.ds(pl.multiple_of(c * bpc, bpc), bpc), :]
            rhs_s[g, 0:AUX_BLOCK_ROWS, :] = jnp.concatenate(
                [blocks, jnp.zeros((AUX_BLOCK_ROWS - bpc, GROUP * qb), F32)], axis=0).astype(BF16)
        for g in range(KV_HEADS):
            lhs = jnp.concatenate([aux, ks_ref[pl.ds(k0, kc), g * HEAD_DIM:(g + 1) * HEAD_DIM]], axis=1)
            s = _dot(lhs, rhs_s[g])
            scores.append(jnp.where(causal, s, MASK_NEG) if diagonal else s)
        probs, stats = [], []
        for g in range(KV_HEADS):
            m_prev, l_prev, _ = state[g]
            off = _lane_slopes(g, qb) * qoff
            m_new = jnp.maximum(m_prev, jnp.max(scores[g], axis=0, keepdims=True) - off)
            m_safe = jnp.where(m_new == NEG_INF, 0.0, m_new)
            alpha = jnp.exp(m_prev - m_safe)
            p = jnp.exp(scores[g] - (m_safe + off))
            stats.append((m_new, alpha * l_prev + jnp.sum(p, axis=0, keepdims=True), alpha))
            probs.append(p.astype(BF16))
        for g in range(KV_HEADS):
            m_new, l_new, alpha = stats[g]
            pv = _dot(vst_ref[g * HEAD_DIM:(g + 1) * HEAD_DIM, pl.ds(k0, kc)], probs[g])
            m_s[g] = m_new
            l_s[g] = l_new
            acc_s[g] = alpha * state[g][2] + pv

    def body(c, carry):
        chunk(c, False)
        return carry

    last = (s0 + qb + kc - 1) // kc - 1
    lax.fori_loop(0, last, body, 0)
    chunk(last, True)

    ngt = ngt_ref[...]
    pieces = []
    for g in range(KV_HEADS):
        o_s = acc_s[g] / jnp.maximum(l_s[g], 1e-30)
        o_c = oc_s[g]
        o_w = ow_s[g]
        for r in range(GROUP):
            h = g * GROUP + r
            sl = slice(r * qb, (r + 1) * qb)
            pieces.append(ngt[h:h + 1, :] * o_c[:, sl] + ngt[N_HEADS + h:N_HEADS + h + 1, :] * o_s[:, sl]
                          + ngt[2 * N_HEADS + h:2 * N_HEADS + h + 1, :] * o_w[:, sl])
    o_ref[...] = jnp.concatenate(pieces, axis=0).T.astype(BF16)


def _nsa_prompt_t(qt, ngt, ca, cb, ksb, kwb, vst, vwt, bsz, t):
    qb = min(ATT_QB, t)
    kc = min(ATT_KC, t)
    wk = min(WINDOW + qb, t)
    n_seg = ca.shape[1]
    n_blk = t // SEL_BLOCK
    nq = t // qb
    cols = GROUP * qb
    per_b = lambda r, w: pl.BlockSpec((None, r, w), lambda b, i: (b, 0, 0))
    tok = lambda r: pl.BlockSpec((r, qb), lambda b, i: (0, b * nq + i))
    seq = lambda r: pl.BlockSpec((r, t), lambda b, i: (0, b))
    return pl.pallas_call(
        functools.partial(_nsa_prompt_t_kernel, t=t, qb=qb, kc=kc, wk=wk),
        grid=(bsz, nq),
        in_specs=[tok(N_HEADS * HEAD_DIM), tok(LANES), per_b(n_seg, KV_W), per_b(n_seg, KV_W),
                  per_b(t, KV_W), per_b(t, KV_W), seq(KV_W // 2), seq(KV_W // 2)],
        out_specs=pl.BlockSpec((None, qb, N_HEADS * HEAD_DIM), lambda b, i: (b, i, 0)),
        out_shape=jax.ShapeDtypeStruct((bsz, t, N_HEADS * HEAD_DIM), BF16),
        scratch_shapes=[pltpu.VMEM((kc, LANES), BF16), pltpu.VMEM((N_HEADS, wk, qb), F32),
                        pltpu.VMEM((N_HEADS, n_seg, qb), F32), pltpu.VMEM((KV_HEADS, LANES + HEAD_DIM, cols), BF16),
                        pltpu.VMEM((KV_HEADS, n_blk, cols), F32),
                        pltpu.VMEM((KV_HEADS, HEAD_DIM, cols), F32), pltpu.VMEM((KV_HEADS, HEAD_DIM, cols), F32),
                        pltpu.VMEM((KV_HEADS, 1, cols), F32), pltpu.VMEM((KV_HEADS, 1, cols), F32),
                        pltpu.VMEM((KV_HEADS, HEAD_DIM, cols), F32)],
        compiler_params=_cparams(("arbitrary", "arbitrary")),
        name="nsa_prompt",
    )(qt, ngt, ca, cb, ksb, kwb, vst, vwt)


def _nsa_sample_select_kernel(q_ref, ca_ref, cb_ref, oc_ref, sel_ref, *, ts, past):
    n_seg = ca_ref.shape[-1]
    n_cmp_valid = n_seg - 1
    n_pblk = past // SEL_BLOCK
    qrow = past + lax.broadcasted_iota(jnp.int32, (ts, 1), 0)
    qpos = _tile_rows(qrow)
    q = q_ref[...]
    gsum = _block_sum_matrix(n_seg, n_pblk)
    blk = lax.broadcasted_iota(jnp.int32, (1, n_pblk), 1)
    cur = qrow // SEL_BLOCK
    forced = (blk == 0) | (blk == cur) | (blk == cur - 1)
    col = lax.broadcasted_iota(jnp.int32, (1, n_seg), 1)
    d = qpos - (col * CMP_STRIDE + (CMP_LEN - 1))
    valid = (d >= 0) & (col < n_cmp_valid)
    df = d.astype(F32)
    outs = []
    for g in range(KV_HEADS):
        qg = _stack_heads(q, g)
        ck_t = (ca_ref[0, g] + pltpu.roll(cb_ref[0, g], n_seg - 1, axis=1)).astype(BF16)
        cv_t = (ca_ref[1, g] + pltpu.roll(cb_ref[1, g], n_seg - 1, axis=1)).astype(BF16)
        s = _dot(qg, ck_t) * QK_SCALE - _row_slopes(g, ts) * df
        p_c = _masked_softmax_rows(s, valid)
        outs.append(_dot_nt(p_c.astype(BF16), cv_t))
        imp = jnp.dot(_sum_heads(p_c, ts), gsum, preferred_element_type=F32, precision=lax.Precision.HIGHEST)
        score = jnp.where(forced, FORCE_BONUS, imp)
        above = jnp.where(FORCE_BONUS > score, 1, 0)
        sel_ref[g * ts:(g + 1) * ts, :] = jnp.where(_select_blocks(score, above), 1.0, 0.0)
    for g in range(KV_HEADS):
        oc_ref[g * GROUP * ts:(g + 1) * GROUP * ts, :] = outs[g]


def _nsa_sample_select(q, ca, cb, past):
    bsz, ts, _ = q.shape
    n_seg = ca.shape[-1]
    n_pblk = past // SEL_BLOCK
    per_b = lambda r, w: pl.BlockSpec((None, r, w), lambda b: (b, 0, 0))
    seg_t = pl.BlockSpec((None, 2, KV_HEADS, HEAD_DIM, n_seg), lambda b: (b, 0, 0, 0, 0))
    return pl.pallas_call(
        functools.partial(_nsa_sample_select_kernel, ts=ts, past=past),
        grid=(bsz,),
        in_specs=[per_b(ts, N_HEADS * HEAD_DIM), seg_t, seg_t],
        out_specs=[per_b(N_HEADS * ts, HEAD_DIM), per_b(KV_HEADS * ts, n_pblk)],
        out_shape=[jax.ShapeDtypeStruct((bsz, N_HEADS * ts, HEAD_DIM), F32),
                   jax.ShapeDtypeStruct((bsz, KV_HEADS * ts, n_pblk), F32)],
        compiler_params=_cparams(("arbitrary",)),
        name="nsa_sample_select",
    )(q, ca, cb)


def _nsa_sample_attend_kernel(pt_ref, *refs, ts, past, pad, n_pg):
    del pt_ref
    pages = refs[:n_pg]
    (q_ref, ng_ref, oc_ref, sel_ref, tail_ref, wbuf_ref, wnew_ref, o_ref, m_s, l_s, acc_s) = refs[n_pg:]
    j = pl.program_id(1)
    nj = pl.num_programs(1)
    rows = GROUP * ts
    kc = n_pg * PAGE_SIZE
    bps = kc // SEL_BLOCK
    qrow = past + lax.broadcasted_iota(jnp.int32, (ts, 1), 0)
    qpos = _tile_rows(qrow)
    q = q_ref[...]

    @pl.when(j == 0)
    def _():
        m_s[...] = jnp.full(m_s.shape, NEG_INF, F32)
        l_s[...] = jnp.zeros(l_s.shape, F32)
        acc_s[...] = jnp.zeros(acc_s.shape, F32)

    k0 = j * kc
    d = qpos - (k0 + lax.broadcasted_iota(jnp.int32, (1, kc), 1))
    df = d.astype(F32)
    expand = _expand_matrix(bps, 0, 0, kc)
    heads = range(KV_HEADS)
    state = [(m_s[g], l_s[g], acc_s[g]) for g in heads]
    scores = []
    for g in heads:
        k_t = jnp.concatenate([p[0, g] for p in pages], axis=1).astype(BF16)
        s = _dot(_stack_heads(q, g), k_t) * QK_SCALE - _row_slopes(g, ts) * df
        selk = _tile_rows(_dot(sel_ref[g * ts:(g + 1) * ts, :].astype(BF16), expand))
        scores.append(jnp.where((d >= 0) & (selk > 0.5), s, NEG_INF))
    m_new = [jnp.maximum(state[g][0], jnp.max(scores[g], axis=-1, keepdims=True)) for g in heads]
    m_safe = [jnp.where(m == NEG_INF, 0.0, m) for m in m_new]
    probs = [jnp.exp(scores[g] - m_safe[g]) for g in heads]
    alpha = [jnp.exp(state[g][0] - m_safe[g]) for g in heads]
    l_new = [alpha[g] * state[g][1] + jnp.sum(probs[g], axis=-1, keepdims=True) for g in heads]
    for g in heads:
        v_t = jnp.concatenate([p[1, g] for p in pages], axis=1).astype(BF16)
        acc_s[g] = alpha[g] * state[g][2] + _dot_nt(probs[g].astype(BF16), v_t)
        m_s[g] = m_new[g]
        l_s[g] = l_new[g]

    @pl.when(j == nj - 1)
    def _():
        ng = ng_ref[...]
        tail = tail_ref[...]
        wnew = wnew_ref[...]
        wb = wbuf_ref.shape[-1]
        colp = lax.broadcasted_iota(jnp.int32, (1, pad), 1)
        d_new = qpos - (past + colp)
        ok_new = (d_new >= 0) & (colp < ts)
        d_old = qpos - (past - wb + lax.broadcasted_iota(jnp.int32, (1, wb), 1))
        ok_old = (d_old >= 0) & (d_old < WINDOW)
        branches = []
        for g in range(KV_HEADS):
            qg = _stack_heads(q, g)
            slope = _row_slopes(g, ts)
            ksl = slice(g * HEAD_DIM, (g + 1) * HEAD_DIM)
            vsl = slice(KV_W // 2 + g * HEAD_DIM, KV_W // 2 + (g + 1) * HEAD_DIM)
            sl = slice(g * rows, (g + 1) * rows)
            s = _dot_nt(qg, tail[:, ksl]) * QK_SCALE - slope * d_new.astype(F32)
            o_s = _flash_out(_flash_step((m_s[g], l_s[g], acc_s[g]), s, ok_new, tail[:, vsl]))
            s = _dot(qg, wbuf_ref[0, g].astype(BF16)) * QK_SCALE - slope * d_old.astype(F32)
            carry = _flash_step(_flash_init(rows), s, ok_old, wbuf_ref[1, g].astype(BF16), v_transposed=True)
            s = _dot_nt(qg, wnew[:, ksl]) * QK_SCALE - slope * d_new.astype(F32)
            o_w = _flash_out(_flash_step(carry, s, ok_new & (d_new < WINDOW), wnew[:, vsl]))
            branches.append([oc_ref[sl, :], o_s, o_w])
        o_ref[...] = _gate_and_pack(ng, branches, ts).astype(BF16)


def _nsa_sample_attend(cache_t, pt_flat, layer, q, ng, oc, sel, tail, wbuf_t, wnew, past):
    bsz, ts, _ = q.shape
    n_pages = past // PAGE_SIZE
    n_pg = min(PAGES_PER_STEP, n_pages)
    steps = n_pages // n_pg
    pad = tail.shape[1]
    wb = wbuf_t.shape[-1]
    bps = n_pg * PAGE_SIZE // SEL_BLOCK
    sel4 = sel.reshape(bsz, KV_HEADS * ts, steps, bps).transpose(0, 2, 1, 3)
    per_b = lambda r, w: pl.BlockSpec((None, r, w), lambda b, j, pt: (b, 0, 0))
    grid_spec = pltpu.PrefetchScalarGridSpec(
        num_scalar_prefetch=1,
        grid=(bsz, steps),
        in_specs=_page_specs(layer, n_pages, n_pg) + [
            per_b(ts, N_HEADS * HEAD_DIM), per_b(ts, LANES), per_b(N_HEADS * ts, HEAD_DIM),
            pl.BlockSpec((None, None, KV_HEADS * ts, bps), lambda b, j, pt: (b, j, 0, 0)),
            per_b(pad, KV_W),
            pl.BlockSpec((None, None, 2, KV_HEADS, HEAD_DIM, wb), lambda b, j, pt: (layer, b, 0, 0, 0, 0)),
            per_b(pad, KV_W)],
        out_specs=per_b(ts, N_HEADS * HEAD_DIM),
        scratch_shapes=[pltpu.VMEM((KV_HEADS, GROUP * ts, 1), F32), pltpu.VMEM((KV_HEADS, GROUP * ts, 1), F32),
                        pltpu.VMEM((KV_HEADS, GROUP * ts, HEAD_DIM), F32)],
    )
    return pl.pallas_call(
        functools.partial(_nsa_sample_attend_kernel, ts=ts, past=past, pad=pad, n_pg=n_pg),
        grid_spec=grid_spec,
        out_shape=jax.ShapeDtypeStruct((bsz, ts, N_HEADS * HEAD_DIM), BF16),
        compiler_params=_cparams(("arbitrary", "arbitrary")),
        name="nsa_sample_attend",
    )(pt_flat, *([cache_t] * n_pg), q, ng, oc, sel4, tail, wbuf_t, wnew)


def _layer_norm(x, g, b):
    mu = jnp.mean(x, axis=-1, keepdims=True)
    xc = x - mu
    var = jnp.mean(xc * xc, axis=-1, keepdims=True)
    return xc * lax.rsqrt(var + LN_EPS) * g + b


def _merge_kernel(x_ref, yr_ref, ya_ref, mg_ref, wa_ref, wb_ref, wo_ref, g_ref, b_ref, wr_ref, br_ref,
                  o_ref, ob_ref, comb_ref):
    a = _dot(yr_ref[...], wa_ref[...])
    b = _dot(ya_ref[...], wb_ref[...])
    merged = mg_ref[:, 0:D_MODEL] * a + mg_ref[:, D_MODEL:2 * D_MODEL] * b
    mix = _dot(merged.astype(BF16), wo_ref[...])
    x1 = _layer_norm(ALPHA * x_ref[...] + mix, g_ref[...], b_ref[...])
    o_ref[...] = x1
    x1b = x1.astype(BF16)
    ob_ref[...] = x1b
    comb_ref[...] = _route(_dot(x1b, wr_ref[...]) + br_ref[...])


def _route(logits):
    n = logits.shape[0]
    lane = lax.broadcasted_iota(jnp.int32, (n, LANES), 1)
    is_g = lane < N_GROUPS
    gl = jnp.where(is_g, logits, NEG_INF)
    gmax = jnp.max(gl, axis=-1, keepdims=True)
    grp = jnp.min(jnp.where(gl == gmax, lane, LANES), axis=-1, keepdims=True)
    gw = 1.0 / jnp.sum(jnp.exp(gl - gmax), axis=-1, keepdims=True)
    e_idx = lane - N_GROUPS
    in_grp = (e_idx >= grp * EXP_PER_GROUP) & (e_idx < (grp + 1) * EXP_PER_GROUP)
    el = jnp.where(in_grp, logits, NEG_INF)
    v1 = jnp.max(el, axis=-1, keepdims=True)
    i1 = jnp.min(jnp.where(el == v1, lane, LANES), axis=-1, keepdims=True)
    el2 = jnp.where(lane == i1, NEG_INF, el)
    v2 = jnp.max(el2, axis=-1, keepdims=True)
    i2 = jnp.min(jnp.where(el2 == v2, lane, LANES), axis=-1, keepdims=True)
    e2 = jnp.exp(v2 - v1)
    w1 = gw / (1.0 + e2)
    w2 = gw * e2 / (1.0 + e2)
    comb = jnp.where(lane == i1, w1, 0.0) + jnp.where(lane == i2, w2, 0.0)
    comb = pltpu.roll(comb, LANES - N_GROUPS, axis=1)
    return comb + jnp.where(lane == GID_LANE, grp.astype(F32), 0.0)


def _merge_out(x2d, yr, ya, mg, wa, wb, wo, g, b, wr, br):
    n = x2d.shape[0]
    tm = min(MERGE_TM, n)
    row = lambda w: pl.BlockSpec((tm, w), lambda i: (i, 0))
    sq = _full_spec((D_MODEL, D_MODEL))
    vec = _full_spec((1, D_MODEL))
    return pl.pallas_call(
        _merge_kernel,
        grid=(n // tm,),
        in_specs=[row(D_MODEL), row(D_RNN), row(N_HEADS * HEAD_DIM), row(2 * D_MODEL), sq, sq, sq, vec, vec,
                  _full_spec((D_MODEL, LANES)), _full_spec((1, LANES))],
        out_specs=[row(D_MODEL), row(D_MODEL), row(LANES)],
        out_shape=[jax.ShapeDtypeStruct((n, D_MODEL), F32), jax.ShapeDtypeStruct((n, D_MODEL), BF16),
                   jax.ShapeDtypeStruct((n, LANES), F32)],
        compiler_params=_cparams(("arbitrary",)),
        name="merge_out",
    )(x2d, yr, ya, mg, wa, wb, wo, g, b, wr, br)


def _moe_grouped_kernel(x_ref, xb_ref, comb_ref, w1_ref, w3_ref, w2_ref, g_ref, b_ref, o_ref, ltri, *, tm, ch):
    grp = pl.program_id(1)
    n_grp = pl.num_programs(1)

    @pl.when((pl.program_id(0) == 0) & (grp == 0))
    def _():
        r = lax.broadcasted_iota(jnp.int32, (tm, tm), 0)
        c = lax.broadcasted_iota(jnp.int32, (tm, tm), 1)
        ltri[...] = jnp.where(c < r, 1.0, 0.0).astype(BF16)

    @pl.when(grp == 0)
    def _():
        o_ref[...] = jnp.zeros(o_ref.shape, F32)

    comb = comb_ref[...]
    lane = lax.broadcasted_iota(jnp.int32, (tm, LANES), 1)
    gid = jnp.sum(jnp.where(lane == GID_LANE, comb, 0.0), axis=-1, keepdims=True).astype(jnp.int32)
    member = gid == grp
    memf = jnp.where(member, 1.0, 0.0)
    rank = _dot(ltri[...], jnp.broadcast_to(memf, (tm, LANES)).astype(BF16))[:, 0:1]
    n_members = jnp.sum(memf).astype(jnp.int32)
    rows_t = jnp.where(lane == 0, rank, jnp.where(lane == 1, memf, 0.0)).T
    rank_row = rows_t[0:1, :]
    mem_row = rows_t[1:2, :] > 0.5
    xb = xb_ref[...]
    c_hi, c_rest = _split_bf16_f32(comb)
    c_mid, c_lo = _split_bf16(c_rest)
    lane_ch = lax.broadcasted_iota(jnp.int32, (ch, LANES), 1)

    def body(k, carry):
        base = (jnp.zeros((1, 1), jnp.int32) + k * ch).astype(F32)
        slot_col = lax.broadcasted_iota(jnp.int32, (ch, 1), 0).astype(F32) + base
        gather = jnp.where(mem_row & (rank_row == slot_col), 1.0, 0.0).astype(BF16)
        xg = _dot(gather, xb).astype(BF16)
        cwg = _dot(gather, c_hi) + _dot(gather, c_mid) + _dot(gather, c_lo)
        y = jnp.zeros((ch, D_MODEL), F32)
        for e in range(EXP_PER_GROUP):
            cw = jnp.sum(jnp.where(lane_ch == grp * EXP_PER_GROUP + e, cwg, 0.0), axis=-1, keepdims=True)
            h1 = _dot(xg, w1_ref[e])
            h = (h1 * _sigmoid(h1)) * _dot(xg, w3_ref[e]) * cw
            y = y + _dot(h.astype(BF16), w2_ref[e])
        slot_row = lax.broadcasted_iota(jnp.int32, (1, ch), 1).astype(F32) + base
        scatter = jnp.where(member & (rank == slot_row), 1.0, 0.0).astype(BF16)
        y_hi, y_lo = _split_bf16(y)
        o_ref[...] += _dot(scatter, y_hi) + _dot(scatter, y_lo)
        return carry

    lax.fori_loop(0, (n_members + ch - 1) // ch, body, 0)

    @pl.when(grp == n_grp - 1)
    def _():
        o_ref[...] = _layer_norm(ALPHA * x_ref[...] + o_ref[...], g_ref[...], b_ref[...])


def _moe_grouped(x1, x1b, comb, w1, w3, w2, g, b):
    n = x1.shape[0]
    tm = min(MOE_TM, n)
    ch = min(MOE_CHUNK, tm)
    row = lambda w: pl.BlockSpec((tm, w), lambda i, e: (i, 0))
    vec = pl.BlockSpec((1, D_MODEL), lambda i, e: (0, 0))
    wspec = lambda a, c: pl.BlockSpec((EXP_PER_GROUP, a, c), lambda i, e: (e, 0, 0))
    resid = pl.BlockSpec((tm, D_MODEL), lambda i, e: (i, 0), pipeline_mode=pl.Buffered(1))
    return pl.pallas_call(
        functools.partial(_moe_grouped_kernel, tm=tm, ch=ch),
        grid=(n // tm, N_GROUPS),
        in_specs=[resid, row(D_MODEL), row(LANES),
                  wspec(D_MODEL, D_EXPERT), wspec(D_MODEL, D_EXPERT), wspec(D_EXPERT, D_MODEL), vec, vec],
        out_specs=row(D_MODEL),
        out_shape=jax.ShapeDtypeStruct((n, D_MODEL), F32),
        scratch_shapes=[pltpu.VMEM((tm, tm), BF16)],
        compiler_params=_cparams(("arbitrary", "arbitrary")),
        name="moe_grouped",
    )(x1, x1b, comb, w1, w3, w2, g, b)


def _block_diag4(w):
    w4 = w.reshape(N_RNN_BLOCKS // 4, 4, RNN_BLOCK, RNN_BLOCK)
    eye = jnp.eye(4, dtype=w.dtype)
    return jnp.einsum("kaij,ab->kaibj", w4, eye).reshape(N_RNN_BLOCKS // 4, 256, 256).astype(BF16)


def _layer_params(l, w_in, conv_w, conv_b, lru_wa, lru_ba, lru_wx, lru_bx, lru_lambda, cmp_w,
                  w_br_a, w_br_b, w_out, ln1_g, ln1_b, ln2_g, ln2_b,
                  router_wg, router_bg, router_we, router_be, exp_w1, exp_w3, exp_w2):
    n_main = 3 * D_RNN + 3 * KV_W
    n_ng = 3 * N_HEADS
    w = w_in[l]
    vec = lambda a: a[l].reshape(1, -1)
    wr = jnp.concatenate([router_wg[l], router_we[l]], axis=1)
    br = jnp.concatenate([router_bg[l], router_be[l]])
    npad = LANES - wr.shape[1]
    return dict(
        wm=w[:, :n_main].astype(BF16),
        wng=jnp.pad(w[:, n_main:n_main + n_ng], ((0, 0), (0, LANES - n_ng))).astype(BF16),
        wmg=w[:, n_main + n_ng:].astype(BF16),
        wt=jnp.concatenate([w[:, 2 * D_RNN:3 * D_RNN], w[:, 3 * D_RNN + KV_W + KV_W // 2:3 * D_RNN + 2 * KV_W],
                            w[:, 3 * D_RNN + 2 * KV_W + KV_W // 2:n_main],
                            jnp.pad(w[:, n_main:n_main + n_ng], ((0, 0), (0, LANES - n_ng)))], axis=1).T.astype(BF16),
        cw=conv_w[l], cb=vec(conv_b),
        wa=_block_diag4(lru_wa[l]), ba=vec(lru_ba), wx=_block_diag4(lru_wx[l]), bx=vec(lru_bx),
        lam=vec(lru_lambda),
        cmpw=jnp.broadcast_to(cmp_w[l][:, :, None, :], (CMP_LEN, 2, KV_HEADS, HEAD_DIM)).reshape(CMP_LEN, KV_W),
        cmpwt=jnp.tile(cmp_w[l].reshape(2, CMP_STRIDE, 2, HEAD_DIM).transpose(0, 2, 3, 1), (1, 1, 1, LANES // CMP_STRIDE)),
        wbra=w_br_a[l].astype(BF16), wbrb=w_br_b[l].astype(BF16), wout=w_out[l].astype(BF16),
        ln1g=vec(ln1_g), ln1b=vec(ln1_b), ln2g=vec(ln2_g), ln2b=vec(ln2_b),
        wr=jnp.pad(wr, ((0, 0), (0, npad))).astype(BF16), br=jnp.pad(br, (0, npad)).reshape(1, LANES),
        w1=exp_w1[l].astype(BF16), w3=exp_w3[l].astype(BF16), w2=exp_w2[l].astype(BF16),
    )


def _mixer_tail(p, x2d, y_rnn, y_att, mg):
    n = x2d.shape[0]
    x1, x1b, comb = _merge_out(x2d, y_rnn.reshape(n, D_RNN), y_att.reshape(n, -1), mg,
                               p["wbra"], p["wbrb"], p["wout"], p["ln1g"], p["ln1b"], p["wr"], p["br"])
    return _moe_grouped(x1, x1b, comb, p["w1"], p["w3"], p["w2"], p["ln2g"], p["ln2b"])


def kernel(x_prompt, x_sample, cache_cmp, cache_sel, cache_win, state_conv, state_lru, page_table,
           w_in, conv_w, conv_b, lru_wa, lru_ba, lru_wx, lru_bx, lru_lambda, cmp_w,
           w_br_a, w_br_b, w_out, ln1_g, ln1_b, ln2_g, ln2_b,
           router_wg, router_bg, router_we, router_be, exp_w1, exp_w3, exp_w2):
    bp, tp, _ = x_prompt.shape
    bs, ts, _ = x_sample.shape
    n_pages = page_table.shape[1]
    past = n_pages * PAGE_SIZE
    assert ts <= SEL_BLOCK and n_pages % CMP_PAGES_PER_STEP == 0 and tp % SEL_BLOCK == 0
    depth = w_in.shape[0]
    rows_minor = lambda a: a.transpose(0, 1, 3, 4, 5, 2)
    ccmp = rows_minor(cache_cmp)
    csel = rows_minor(cache_sel)
    cwin = rows_minor(cache_win)
    pt_flat = page_table.reshape(-1).astype(jnp.int32)
    pad = LANES
    kv6 = lambda a, b, t: a.reshape(b, t, 2, KV_HEADS, HEAD_DIM)

    xp = x_prompt.reshape(bp * tp, D_MODEL)
    xs = x_sample.reshape(bs * ts, D_MODEL)
    outs = [[] for _ in range(10)]
    for l in range(depth):
        p = _layer_params(l, w_in, conv_w, conv_b, lru_wa, lru_ba, lru_wx, lru_bx, lru_lambda, cmp_w,
                          w_br_a, w_br_b, w_out, ln1_g, ln1_b, ln2_g, ln2_b,
                          router_wg, router_bg, router_we, router_be, exp_w1, exp_w3, exp_w2)
        lru = (p["cw"], p["cb"], p["wa"], p["ba"], p["wx"], p["bx"], p["lam"])

        xr, gr, kvc, kvs, kvw, kvsb, kvwb, mg = _project_in(xp, p["wm"], p["wng"], p["wmg"], with_q=False)
        r3 = lambda a: a.reshape(bp, tp, -1)
        y_rnn, cbuf, ht = _rglru(r3(xr), r3(gr), jnp.zeros((bp, CONV_W - 1, D_RNN), F32),
                                 jnp.zeros((bp, 1, D_RNN), F32), *lru, pos0=0)
        ca, cb = _compress_rows(r3(kvc), p["cmpw"])
        qt, vst, vwt, ngt = _project_t(xp, p["wt"])
        y_att = _nsa_prompt_t(qt, ngt, ca, cb, r3(kvsb), r3(kvwb), vst, vwt, bp, tp)
        xp = _mixer_tail(p, xp, y_rnn, y_att, mg)
        wlen = min(WINDOW, tp)
        outs[0].append(kv6(kvc, bp, tp))
        outs[2].append(kv6(kvs, bp, tp))
        outs[4].append(kv6(kvw, bp, tp)[:, tp - wlen:])
        outs[6].append(cbuf)
        outs[8].append(ht.reshape(bp, D_RNN))

        xr, gr, q, kvc, kvs, kvw, kvsb, kvwb, ng, mg = _project_in(xs, p["wm"], p["wng"], p["wmg"], with_q=True)
        r3 = lambda a: a.reshape(bs, ts, -1)
        y_rnn, cbuf, ht = _rglru(r3(xr), r3(gr), state_conv[l], state_lru[l].reshape(bs, 1, D_RNN),
                                 *lru, pos0=past)
        ca, cb = _compress_pages(ccmp, pt_flat, p["cmpwt"], l, bs, n_pages)
        oc, sel = _nsa_sample_select(r3(q), ca, cb, past)
        padrows = lambda a: jnp.pad(r3(a), ((0, 0), (0, pad - ts), (0, 0)))
        y_att = _nsa_sample_attend(csel, pt_flat, l, r3(q), r3(ng), oc, sel, padrows(kvsb), cwin, padrows(kvwb),
                                   past)
        xs = _mixer_tail(p, xs, y_rnn, y_att, mg)
        outs[1].append(kv6(kvc, bs, ts))
        outs[3].append(kv6(kvs, bs, ts))
        kw_all = jnp.concatenate([cache_win[l], kv6(kvw, bs, ts)], axis=1)
        outs[5].append(kw_all[:, ts:])
        outs[7].append(cbuf)
        outs[9].append(ht.reshape(bs, D_RNN))

    st = [jnp.stack(o) for o in outs]
    return (xp.reshape(bp, tp, D_MODEL), xs.reshape(bs, ts, D_MODEL),
            st[0], st[1], st[2], st[3], st[4], st[5], st[6], st[7], st[8], st[9])
```
